```python
import jax, jax.numpy as jnp
from jax import lax
import numpy as np

D_MODEL = 2048
BATCH = 8
SEQ = 2048
DEPTH = 4

CHUNK = 64
GDN_HEAD_DIM = 128
GDN_HEADS = D_MODEL // 256
GDN_WIDTH = GDN_HEADS * GDN_HEAD_DIM
CONV_WIDTH = 4
GMLP_WIDTH = D_MODEL // 2
GMLP_GROUPS = 8
GMLP_GROUP_DIM = GMLP_WIDTH // GMLP_GROUPS
GMLP_BLOCK = 128
SBA_HEAD_DIM = 128
SBA_HEADS = D_MODEL // 256
SBA_WIDTH = SBA_HEADS * SBA_HEAD_DIM
QUERY_BLOCK = 128
N_BRANCHES = 3
D_FF = 4 * D_MODEL
EPS = 1e-6
PROJ_SIZES = (3 * GDN_WIDTH,
              GDN_HEADS,
              GDN_HEADS,
              GDN_WIDTH,
              2 * GMLP_WIDTH,
              3 * SBA_WIDTH,
              N_BRANCHES * D_MODEL)
D_IN = sum(PROJ_SIZES)

kernel_name = "hybrid_gdn_gmlp_stickbreak_trunk"


def rms_norm(x, gain):
    xf = x.astype(jnp.float32)
    y = xf * lax.rsqrt(jnp.mean(xf * xf, axis=-1, keepdims=True) + EPS)
    return (y * gain.astype(jnp.float32)).astype(x.dtype)


def layer_norm(x, gain):
    xf = x.astype(jnp.float32)
    mu = jnp.mean(xf, axis=-1, keepdims=True)
    xc = xf - mu
    y = xc * lax.rsqrt(jnp.mean(xc * xc, axis=-1, keepdims=True) + EPS)
    return (y * gain.astype(jnp.float32)).astype(x.dtype)


def l2_norm(x):
    return x * lax.rsqrt(jnp.sum(x * x, axis=-1, keepdims=True) + EPS)


def causal_depthwise_conv(x, w):
    K, C = w.shape
    return lax.conv_general_dilated(x, w[:, None, :].astype(x.dtype), window_strides=(1,),
                                    padding=[(K - 1, 0)],
                                    dimension_numbers=('NWC', 'WIO', 'NWC'),
                                    feature_group_count=C)


def gated_delta_rule_chunked(q, k, v, g, beta):
    out_dtype = v.dtype
    B, H, T, dk = q.shape
    dv = v.shape[-1]
    N = T // CHUNK
    f32 = jnp.float32
    q = q.astype(f32).reshape(B, H, N, CHUNK, dk)
    k = k.astype(f32).reshape(B, H, N, CHUNK, dk)
    v = v.astype(f32).reshape(B, H, N, CHUNK, dv)
    g = jnp.cumsum(g.astype(f32).reshape(B, H, N, CHUNK), axis=-1)
    beta = beta.astype(f32).reshape(B, H, N, CHUNK)
    incl = jnp.tril(jnp.ones((CHUNK, CHUNK), dtype=bool))
    strict = jnp.tril(jnp.ones((CHUNK, CHUNK), dtype=bool), -1)
    decay = jnp.exp(jnp.where(incl, g[..., :, None] - g[..., None, :], -jnp.inf))
    kb = k * beta[..., None]
    L = jnp.where(strict, jnp.einsum('bhncd,bhnsd->bhncs', kb, k) * decay, 0.0)
    eye = jnp.eye(CHUNK, dtype=f32)
    rhs = jnp.concatenate([v * beta[..., None], kb * jnp.exp(g)[..., None]], axis=-1)
    sol = lax.linalg.triangular_solve(L + eye, rhs, left_side=True, lower=True,
                                      unit_diagonal=True)
    u, w = sol[..., :dv], sol[..., dv:]
    intra = jnp.einsum('bhncd,bhnsd->bhncs', q, k) * decay
    q_dec = q * jnp.exp(g)[..., None]
    g_last = g[..., -1]
    k_dec = k * jnp.exp(g_last[..., None] - g)[..., None]
    xs = tuple(jnp.moveaxis(t, 2, 0) for t in (u, w, intra, q_dec, k_dec, g_last))

    def step(S, inp):
        u_n, w_n, a_n, qd_n, kd_n, gl_n = inp
        v_new = u_n - jnp.einsum('bhck,bhkv->bhcv', w_n, S)
        o_n = jnp.einsum('bhck,bhkv->bhcv', qd_n, S) + jnp.einsum('bhcs,bhsv->bhcv', a_n, v_new)
        S = S * jnp.exp(gl_n)[..., None, None] + jnp.einsum('bhck,bhcv->bhkv', kd_n, v_new)
        return S, o_n

    S0 = jnp.zeros((B, H, dk, dv), f32)
    _, o = lax.scan(step, S0, xs)
    return jnp.moveaxis(o, 0, 2).reshape(B, H, T, dv).astype(out_dtype)


def stick_breaking_attention(q, k, v):
    B, H, T, d = q.shape
    nb = T // QUERY_BLOCK
    qb = jnp.moveaxis(q.reshape(B, H, nb, QUERY_BLOCK, d), 2, 0)
    kpos = jnp.arange(T)
    scale = d ** -0.5

    def block(args):
        q_blk, i = args
        z = jnp.einsum('bhqd,bhkd->bhqk', q_blk, k).astype(jnp.float32) * scale
        qpos = i * QUERY_BLOCK + jnp.arange(QUERY_BLOCK)
        strict = kpos[None, :] < qpos[:, None]
        log_keep = jnp.where(strict, jax.nn.log_sigmoid(-z), 0.0)
        suffix = lax.cumsum(log_keep, axis=3, reverse=True) - log_keep
        A = jnp.where(strict, jnp.exp(jax.nn.log_sigmoid(z) + suffix), 0.0)
        return jnp.einsum('bhqk,bhkd->bhqd', A.astype(v.dtype), v)

    out = lax.map(block, (qb, jnp.arange(nb)))
    return jnp.moveaxis(out, 0, 2).reshape(B, H, T, d)


def hybrid_mixer(h, w_in, conv_w, a_log, dt_bias, gdn_norm_g, gmlp_ln_g, w_spatial, b_spatial,
                 sba_q_g, sba_k_g, w_out_a, w_out_b, w_out_c, w_out):
    B, T, _ = h.shape
    z = h @ w_in
    split_idx = np.cumsum(PROJ_SIZES)[:-1].tolist()
    gdn_qkv, gdn_a, gdn_b, gdn_gate, gmlp_uv, sba_qkv, gate_logits = jnp.split(z, split_idx, axis=-1)

    qkv = jax.nn.silu(causal_depthwise_conv(gdn_qkv, conv_w))
    qa, ka, va = jnp.split(qkv, 3, axis=-1)
    to_heads = lambda t, H, d: jnp.transpose(t.reshape(B, T, H, d), (0, 2, 1, 3))
    qa = l2_norm(to_heads(qa, GDN_HEADS, GDN_HEAD_DIM).astype(jnp.float32)) * GDN_HEAD_DIM ** -0.5
    ka = l2_norm(to_heads(ka, GDN_HEADS, GDN_HEAD_DIM).astype(jnp.float32))
    va = to_heads(va, GDN_HEADS, GDN_HEAD_DIM)
    beta = jnp.transpose(jax.nn.sigmoid(gdn_b.astype(jnp.float32)), (0, 2, 1))
    g = -jnp.exp(a_log.astype(jnp.float32)) * jax.nn.softplus(
        gdn_a.astype(jnp.float32) + dt_bias.astype(jnp.float32))
    g = jnp.transpose(g, (0, 2, 1))
    oa = gated_delta_rule_chunked(qa, ka, va, g, beta)
    oa = jnp.transpose(oa, (0, 2, 1, 3))
    oa = rms_norm(oa, gdn_norm_g) * jax.nn.silu(gdn_gate.reshape(B, T, GDN_HEADS, GDN_HEAD_DIM))
    branch_a = oa.reshape(B, T, GDN_WIDTH) @ w_out_a

    uv = jax.nn.gelu(gmlp_uv, approximate=False)
    u, vb = jnp.split(uv, 2, axis=-1)
    vb = layer_norm(vb, gmlp_ln_g).reshape(B, T // GMLP_BLOCK, GMLP_BLOCK, GMLP_GROUPS, GMLP_GROUP_DIM)
    pos = jnp.arange(GMLP_BLOCK) // CHUNK
    chunk_causal = pos[None, :] <= pos[:, None]
    ws = jnp.where(chunk_causal[None], w_spatial, 0.0).astype(vb.dtype)
    s = jnp.einsum('gts,bnsgc->bntgc', ws, vb) + jnp.transpose(b_spatial)[None, None, :, :, None]
    branch_b = (u * s.reshape(B, T, GMLP_WIDTH)) @ w_out_b

    qc, kc, vc = jnp.split(sba_qkv, 3, axis=-1)
    qc = to_heads(rms_norm(qc.reshape(B, T, SBA_HEADS, SBA_HEAD_DIM), sba_q_g).reshape(B, T, SBA_WIDTH), SBA_HEADS, SBA_HEAD_DIM)
    kc = to_heads(rms_norm(kc.reshape(B, T, SBA_HEADS, SBA_HEAD_DIM), sba_k_g).reshape(B, T, SBA_WIDTH), SBA_HEADS, SBA_HEAD_DIM)
    vc = to_heads(vc, SBA_HEADS, SBA_HEAD_DIM)
    oc = stick_breaking_attention(qc, kc, vc)
    branch_c = jnp.transpose(oc, (0, 2, 1, 3)).reshape(B, T, SBA_WIDTH) @ w_out_c

    gates = jax.nn.sigmoid(gate_logits).reshape(B, T, N_BRANCHES, D_MODEL)
    y = gates[:, :, 0] * branch_a + gates[:, :, 1] * branch_b + gates[:, :, 2] * branch_c
    return y @ w_out


def _fwd_setup_inputs(seed: int = 0) -> dict:
    key = jax.random.key(seed)
    ks = jax.random.split(key, 20)
    nrm = lambda k, shape, scale: jax.random.normal(k, shape, jnp.float32) * scale
    gain = lambda k, shape: 1.0 + 0.02 * jax.random.normal(k, shape, jnp.float32)
    dt = jnp.exp(jax.random.uniform(ks[4], (DEPTH, GDN_HEADS), jnp.float32,
                                    np.log(1e-3).astype(np.float32), np.log(1e-1).astype(np.float32)))
    return {
        "x": nrm(ks[0], (BATCH, SEQ, D_MODEL), 1.0),
        "w_in": nrm(ks[1], (DEPTH, D_MODEL, D_IN), D_MODEL ** -0.5),
        "conv_w": nrm(ks[2], (DEPTH, CONV_WIDTH, 3 * GDN_WIDTH), CONV_WIDTH ** -0.5),
        "a_log": jnp.log(jax.random.uniform(ks[3], (DEPTH, GDN_HEADS), jnp.float32, 1.0, 16.0)),
        "dt_bias": dt + jnp.log(-jnp.expm1(-dt)),
        "gdn_norm_g": gain(ks[5], (DEPTH, GDN_HEAD_DIM)),
        "gmlp_ln_g": gain(ks[6], (DEPTH, GMLP_WIDTH)),
        "w_spatial": nrm(ks[7], (DEPTH, GMLP_GROUPS, GMLP_BLOCK, GMLP_BLOCK), GMLP_BLOCK ** -0.5),
        "b_spatial": gain(ks[8], (DEPTH, GMLP_GROUPS, GMLP_BLOCK)),
        "sba_q_g": gain(ks[9], (DEPTH, SBA_HEAD_DIM)),
        "sba_k_g": gain(ks[10], (DEPTH, SBA_HEAD_DIM)),
        "w_out_a": nrm(ks[11], (DEPTH, GDN_WIDTH, D_MODEL), GDN_WIDTH ** -0.5),
        "w_out_b": nrm(ks[12], (DEPTH, GMLP_WIDTH, D_MODEL), GMLP_WIDTH ** -0.5),
        "w_out_c": nrm(ks[13], (DEPTH, SBA_WIDTH, D_MODEL), SBA_WIDTH ** -0.5),
        "w_out": nrm(ks[14], (DEPTH, D_MODEL, D_MODEL), D_MODEL ** -0.5),
        "norm_mix_g": gain(ks[15], (DEPTH, D_MODEL)),
        "norm_mlp_g": gain(ks[16], (DEPTH, D_MODEL)),
        "w_ff1": nrm(ks[17], (DEPTH, D_MODEL, D_FF), D_MODEL ** -0.5),
        "w_ff2": nrm(ks[18], (DEPTH, D_FF, D_MODEL), D_FF ** -0.5),
    }


def _fwd_reference(x, w_in, conv_w, a_log, dt_bias, gdn_norm_g, gmlp_ln_g, w_spatial, b_spatial,
              sba_q_g, sba_k_g, w_out_a, w_out_b, w_out_c, w_out, norm_mix_g, norm_mlp_g,
              w_ff1, w_ff2):
    for l in range(DEPTH):
        h = rms_norm(x, norm_mix_g[l])
        x = x + hybrid_mixer(h, w_in[l], conv_w[l], a_log[l], dt_bias[l], gdn_norm_g[l],
                             gmlp_ln_g[l], w_spatial[l], b_spatial[l], sba_q_g[l], sba_k_g[l],
                             w_out_a[l], w_out_b[l], w_out_c[l], w_out[l])
        h = rms_norm(x, norm_mlp_g[l])
        x = x + jnp.square(jax.nn.relu(h @ w_ff1[l])) @ w_ff2[l]
    return x


import jax as _jax
import jax.numpy as _jnp

TWIN_FORMAT = 'train_step'
FWD_PARAMS = ['x', 'w_in', 'conv_w', 'a_log', 'dt_bias', 'gdn_norm_g', 'gmlp_ln_g', 'w_spatial', 'b_spatial', 'sba_q_g', 'sba_k_g', 'w_out_a', 'w_out_b', 'w_out_c', 'w_out', 'norm_mix_g', 'norm_mlp_g', 'w_ff1', 'w_ff2']
TWIN_WEIGHTS = ['w_in', 'conv_w', 'a_log', 'dt_bias', 'gdn_norm_g', 'gmlp_ln_g', 'w_spatial', 'b_spatial', 'sba_q_g', 'sba_k_g', 'w_out_a', 'w_out_b', 'w_out_c', 'w_out', 'norm_mix_g', 'norm_mlp_g', 'w_ff1', 'w_ff2']
TWIN_DIFF_INPUT = 'x'
TWIN_INPUTS = ['x', 'w_in', 'conv_w', 'a_log', 'dt_bias', 'gdn_norm_g', 'gmlp_ln_g', 'w_spatial', 'b_spatial', 'sba_q_g', 'sba_k_g', 'w_out_a', 'w_out_b', 'w_out_c', 'w_out', 'norm_mix_g', 'norm_mlp_g', 'w_ff1', 'w_ff2', 'loss_target', 'm_w_in', 'm_conv_w', 'm_a_log', 'm_dt_bias', 'm_gdn_norm_g', 'm_gmlp_ln_g', 'm_w_spatial', 'm_b_spatial', 'm_sba_q_g', 'm_sba_k_g', 'm_w_out_a', 'm_w_out_b', 'm_w_out_c', 'm_w_out', 'm_norm_mix_g', 'm_norm_mlp_g', 'm_w_ff1', 'm_w_ff2', 'v_w_in', 'v_conv_w', 'v_a_log', 'v_dt_bias', 'v_gdn_norm_g', 'v_gmlp_ln_g', 'v_w_spatial', 'v_b_spatial', 'v_sba_q_g', 'v_sba_k_g', 'v_w_out_a', 'v_w_out_b', 'v_w_out_c', 'v_w_out', 'v_norm_mix_g', 'v_norm_mlp_g', 'v_w_ff1', 'v_w_ff2']
TWIN_OUTPUTS = ['loss', 'grad_x', 'grad_w_in', 'grad_conv_w', 'grad_a_log', 'grad_dt_bias', 'grad_gdn_norm_g', 'grad_gmlp_ln_g', 'grad_w_spatial', 'grad_b_spatial', 'grad_sba_q_g', 'grad_sba_k_g', 'grad_w_out_a', 'grad_w_out_b', 'grad_w_out_c', 'grad_w_out', 'grad_norm_mix_g', 'grad_norm_mlp_g', 'grad_w_ff1', 'grad_w_ff2', 'delta_w_in', 'delta_conv_w', 'delta_a_log', 'delta_dt_bias', 'delta_gdn_norm_g', 'delta_gmlp_ln_g', 'delta_w_spatial', 'delta_b_spatial', 'delta_sba_q_g', 'delta_sba_k_g', 'delta_w_out_a', 'delta_w_out_b', 'delta_w_out_c', 'delta_w_out', 'delta_norm_mix_g', 'delta_norm_mlp_g', 'delta_w_ff1', 'delta_w_ff2', 'new_m_w_in', 'new_m_conv_w', 'new_m_a_log', 'new_m_dt_bias', 'new_m_gdn_norm_g', 'new_m_gmlp_ln_g', 'new_m_w_spatial', 'new_m_b_spatial', 'new_m_sba_q_g', 'new_m_sba_k_g', 'new_m_w_out_a', 'new_m_w_out_b', 'new_m_w_out_c', 'new_m_w_out', 'new_m_norm_mix_g', 'new_m_norm_mlp_g', 'new_m_w_ff1', 'new_m_w_ff2', 'new_v_w_in', 'new_v_conv_w', 'new_v_a_log', 'new_v_dt_bias', 'new_v_gdn_norm_g', 'new_v_gmlp_ln_g', 'new_v_w_spatial', 'new_v_b_spatial', 'new_v_sba_q_g', 'new_v_sba_k_g', 'new_v_w_out_a', 'new_v_w_out_b', 'new_v_w_out_c', 'new_v_w_out', 'new_v_norm_mix_g', 'new_v_norm_mlp_g', 'new_v_w_ff1', 'new_v_w_ff2']
TWIN_LEAF_KINDS = {'loss': 'loss', 'grad_x': 'grad_x', 'grad_w_in': 'grad_w', 'grad_conv_w': 'grad_w', 'grad_a_log': 'grad_w', 'grad_dt_bias': 'grad_w', 'grad_gdn_norm_g': 'grad_w', 'grad_gmlp_ln_g': 'grad_w', 'grad_w_spatial': 'grad_w', 'grad_b_spatial': 'grad_w', 'grad_sba_q_g': 'grad_w', 'grad_sba_k_g': 'grad_w', 'grad_w_out_a': 'grad_w', 'grad_w_out_b': 'grad_w', 'grad_w_out_c': 'grad_w', 'grad_w_out': 'grad_w', 'grad_norm_mix_g': 'grad_w', 'grad_norm_mlp_g': 'grad_w', 'grad_w_ff1': 'grad_w', 'grad_w_ff2': 'grad_w', 'delta_w_in': 'delta_w', 'delta_conv_w': 'delta_w', 'delta_a_log': 'delta_w', 'delta_dt_bias': 'delta_w', 'delta_gdn_norm_g': 'delta_w', 'delta_gmlp_ln_g': 'delta_w', 'delta_w_spatial': 'delta_w', 'delta_b_spatial': 'delta_w', 'delta_sba_q_g': 'delta_w', 'delta_sba_k_g': 'delta_w', 'delta_w_out_a': 'delta_w', 'delta_w_out_b': 'delta_w', 'delta_w_out_c': 'delta_w', 'delta_w_out': 'delta_w', 'delta_norm_mix_g': 'delta_w', 'delta_norm_mlp_g': 'delta_w', 'delta_w_ff1': 'delta_w', 'delta_w_ff2': 'delta_w', 'new_m_w_in': 'new_m', 'new_m_conv_w': 'new_m', 'new_m_a_log': 'new_m', 'new_m_dt_bias': 'new_m', 'new_m_gdn_norm_g': 'new_m', 'new_m_gmlp_ln_g': 'new_m', 'new_m_w_spatial': 'new_m', 'new_m_b_spatial': 'new_m', 'new_m_sba_q_g': 'new_m', 'new_m_sba_k_g': 'new_m', 'new_m_w_out_a': 'new_m', 'new_m_w_out_b': 'new_m', 'new_m_w_out_c': 'new_m', 'new_m_w_out': 'new_m', 'new_m_norm_mix_g': 'new_m', 'new_m_norm_mlp_g': 'new_m', 'new_m_w_ff1': 'new_m', 'new_m_w_ff2': 'new_m', 'new_v_w_in': 'new_v', 'new_v_conv_w': 'new_v', 'new_v_a_log': 'new_v', 'new_v_dt_bias': 'new_v', 'new_v_gdn_norm_g': 'new_v', 'new_v_gmlp_ln_g': 'new_v', 'new_v_w_spatial': 'new_v', 'new_v_b_spatial': 'new_v', 'new_v_sba_q_g': 'new_v', 'new_v_sba_k_g': 'new_v', 'new_v_w_out_a': 'new_v', 'new_v_w_out_b': 'new_v', 'new_v_w_out_c': 'new_v', 'new_v_w_out': 'new_v', 'new_v_norm_mix_g': 'new_v', 'new_v_norm_mlp_g': 'new_v', 'new_v_w_ff1': 'new_v', 'new_v_w_ff2': 'new_v'}


def _forward(args):
    return _fwd_reference(*[args[k] for k in FWD_PARAMS])


def _output_shape():
    out = _jax.eval_shape(lambda: _forward(_fwd_setup_inputs(0)))
    return out.shape, out.dtype

N_MICROBATCH = 1
ADAM_LR = 0.001
ADAM_B1 = 0.9
ADAM_B2 = 0.999
ADAM_EPS = 1e-08
ADAM_WD = 0.01
ADAM_STEP = 10
PER_EXAMPLE_BATCH_AXIS = {'x': 0, 'loss_target': 0}
SHARED_INPUTS = []
_WEIGHT_DTYPES = {'w_in': _jnp.float32, 'conv_w': _jnp.float32, 'a_log': _jnp.float32, 'dt_bias': _jnp.float32, 'gdn_norm_g': _jnp.float32, 'gmlp_ln_g': _jnp.float32, 'w_spatial': _jnp.float32, 'b_spatial': _jnp.float32, 'sba_q_g': _jnp.float32, 'sba_k_g': _jnp.float32, 'w_out_a': _jnp.float32, 'w_out_b': _jnp.float32, 'w_out_c': _jnp.float32, 'w_out': _jnp.float32, 'norm_mix_g': _jnp.float32, 'norm_mlp_g': _jnp.float32, 'w_ff1': _jnp.float32, 'w_ff2': _jnp.float32}
MOMENT_SCALE = {'w_in': 9.081212e-01, 'conv_w': 8.894301e-01, 'a_log': 5.755508e+00, 'dt_bias': 5.511051e+00, 'gdn_norm_g': 1.605222e+01, 'gmlp_ln_g': 1.731223e+00, 'w_spatial': 2.602991e-01, 'b_spatial': 1.841488e+00, 'sba_q_g': 2.189299e+00, 'sba_k_g': 2.193193e+00, 'w_out_a': 1.173672e+00, 'w_out_b': 2.291562e+00, 'w_out_c': 1.865559e+00, 'w_out': 3.176534e+00, 'norm_mix_g': 5.029964e+00, 'norm_mlp_g': 2.447504e+01, 'w_ff1': 1.912708e+00, 'w_ff2': 6.986904e+00}


def _to_microbatches(a, axis):
    t = _jnp.moveaxis(a, axis, 0)
    t = t.reshape((N_MICROBATCH, t.shape[0] // N_MICROBATCH) + t.shape[1:])
    return _jnp.moveaxis(t, 1, axis + 1)


def setup_inputs(seed: int = 0) -> dict:
    inp = _fwd_setup_inputs(seed)
    key = _jax.random.fold_in(_jax.random.key(seed), 7919)
    shape, _ = _output_shape()
    out = dict(inp)
    out["loss_target"] = _jax.random.normal(_jax.random.fold_in(key, 0), shape, _jnp.float32)
    for i, name in enumerate(TWIN_WEIGHTS):
        w = inp[name].astype(_jnp.float32)
        if MOMENT_SCALE is None:
            s = _jnp.sqrt(_jnp.mean(_jnp.square(w)) + 1e-30)
        else:
            s = MOMENT_SCALE[name]
        km, kv = _jax.random.split(_jax.random.fold_in(key, i + 1))
        out[name] = w
        out["m_" + name] = s * _jax.random.normal(km, w.shape, _jnp.float32)
        out["v_" + name] = (s * s) * _jax.random.uniform(kv, w.shape, _jnp.float32, 0.5, 1.5)
    if N_MICROBATCH > 1:
        for name, axis in PER_EXAMPLE_BATCH_AXIS.items():
            out[name] = _to_microbatches(out[name], axis)
    return {'x': out['x'], 'w_in': out['w_in'], 'conv_w': out['conv_w'], 'a_log': out['a_log'], 'dt_bias': out['dt_bias'], 'gdn_norm_g': out['gdn_norm_g'], 'gmlp_ln_g': out['gmlp_ln_g'], 'w_spatial': out['w_spatial'], 'b_spatial': out['b_spatial'], 'sba_q_g': out['sba_q_g'], 'sba_k_g': out['sba_k_g'], 'w_out_a': out['w_out_a'], 'w_out_b': out['w_out_b'], 'w_out_c': out['w_out_c'], 'w_out': out['w_out'], 'norm_mix_g': out['norm_mix_g'], 'norm_mlp_g': out['norm_mlp_g'], 'w_ff1': out['w_ff1'], 'w_ff2': out['w_ff2'], 'loss_target': out['loss_target'], 'm_w_in': out['m_w_in'], 'm_conv_w': out['m_conv_w'], 'm_a_log': out['m_a_log'], 'm_dt_bias': out['m_dt_bias'], 'm_gdn_norm_g': out['m_gdn_norm_g'], 'm_gmlp_ln_g': out['m_gmlp_ln_g'], 'm_w_spatial': out['m_w_spatial'], 'm_b_spatial': out['m_b_spatial'], 'm_sba_q_g': out['m_sba_q_g'], 'm_sba_k_g': out['m_sba_k_g'], 'm_w_out_a': out['m_w_out_a'], 'm_w_out_b': out['m_w_out_b'], 'm_w_out_c': out['m_w_out_c'], 'm_w_out': out['m_w_out'], 'm_norm_mix_g': out['m_norm_mix_g'], 'm_norm_mlp_g': out['m_norm_mlp_g'], 'm_w_ff1': out['m_w_ff1'], 'm_w_ff2': out['m_w_ff2'], 'v_w_in': out['v_w_in'], 'v_conv_w': out['v_conv_w'], 'v_a_log': out['v_a_log'], 'v_dt_bias': out['v_dt_bias'], 'v_gdn_norm_g': out['v_gdn_norm_g'], 'v_gmlp_ln_g': out['v_gmlp_ln_g'], 'v_w_spatial': out['v_w_spatial'], 'v_b_spatial': out['v_b_spatial'], 'v_sba_q_g': out['v_sba_q_g'], 'v_sba_k_g': out['v_sba_k_g'], 'v_w_out_a': out['v_w_out_a'], 'v_w_out_b': out['v_w_out_b'], 'v_w_out_c': out['v_w_out_c'], 'v_w_out': out['v_w_out'], 'v_norm_mix_g': out['v_norm_mix_g'], 'v_norm_mlp_g': out['v_norm_mlp_g'], 'v_w_ff1': out['v_w_ff1'], 'v_w_ff2': out['v_w_ff2']}


def _loss(weights, diff, rest, loss_target):
    with _jax.named_scope("forward"):
        args = {**rest, TWIN_DIFF_INPUT: diff, **{k: w.astype(_WEIGHT_DTYPES[k]) for k, w in weights.items()}}
        y = _forward(args)
    with _jax.named_scope("loss_head"):
        err = _jnp.square(y.astype(_jnp.float32) - loss_target)
        return 0.5 * _jnp.sum(_jnp.mean(err, axis=-1)) if err.ndim else 0.5 * err


def _adamw(w, g, m, v):
    m = ADAM_B1 * m + (1.0 - ADAM_B1) * g
    v = ADAM_B2 * v + (1.0 - ADAM_B2) * _jnp.square(g)
    m_hat = m / (1.0 - ADAM_B1 ** ADAM_STEP)
    v_hat = v / (1.0 - ADAM_B2 ** ADAM_STEP)
    delta = -ADAM_LR * (m_hat / (_jnp.sqrt(v_hat) + ADAM_EPS) + ADAM_WD * w)
    return delta, m, v


def reference(x, w_in, conv_w, a_log, dt_bias, gdn_norm_g, gmlp_ln_g, w_spatial, b_spatial, sba_q_g, sba_k_g, w_out_a, w_out_b, w_out_c, w_out, norm_mix_g, norm_mlp_g, w_ff1, w_ff2, loss_target, m_w_in, m_conv_w, m_a_log, m_dt_bias, m_gdn_norm_g, m_gmlp_ln_g, m_w_spatial, m_b_spatial, m_sba_q_g, m_sba_k_g, m_w_out_a, m_w_out_b, m_w_out_c, m_w_out, m_norm_mix_g, m_norm_mlp_g, m_w_ff1, m_w_ff2, v_w_in, v_conv_w, v_a_log, v_dt_bias, v_gdn_norm_g, v_gmlp_ln_g, v_w_spatial, v_b_spatial, v_sba_q_g, v_sba_k_g, v_w_out_a, v_w_out_b, v_w_out_c, v_w_out, v_norm_mix_g, v_norm_mlp_g, v_w_ff1, v_w_ff2):
    given = dict(x=x, w_in=w_in, conv_w=conv_w, a_log=a_log, dt_bias=dt_bias, gdn_norm_g=gdn_norm_g, gmlp_ln_g=gmlp_ln_g, w_spatial=w_spatial, b_spatial=b_spatial, sba_q_g=sba_q_g, sba_k_g=sba_k_g, w_out_a=w_out_a, w_out_b=w_out_b, w_out_c=w_out_c, w_out=w_out, norm_mix_g=norm_mix_g, norm_mlp_g=norm_mlp_g, w_ff1=w_ff1, w_ff2=w_ff2, loss_target=loss_target, m_w_in=m_w_in, m_conv_w=m_conv_w, m_a_log=m_a_log, m_dt_bias=m_dt_bias, m_gdn_norm_g=m_gdn_norm_g, m_gmlp_ln_g=m_gmlp_ln_g, m_w_spatial=m_w_spatial, m_b_spatial=m_b_spatial, m_sba_q_g=m_sba_q_g, m_sba_k_g=m_sba_k_g, m_w_out_a=m_w_out_a, m_w_out_b=m_w_out_b, m_w_out_c=m_w_out_c, m_w_out=m_w_out, m_norm_mix_g=m_norm_mix_g, m_norm_mlp_g=m_norm_mlp_g, m_w_ff1=m_w_ff1, m_w_ff2=m_w_ff2, v_w_in=v_w_in, v_conv_w=v_conv_w, v_a_log=v_a_log, v_dt_bias=v_dt_bias, v_gdn_norm_g=v_gdn_norm_g, v_gmlp_ln_g=v_gmlp_ln_g, v_w_spatial=v_w_spatial, v_b_spatial=v_b_spatial, v_sba_q_g=v_sba_q_g, v_sba_k_g=v_sba_k_g, v_w_out_a=v_w_out_a, v_w_out_b=v_w_out_b, v_w_out_c=v_w_out_c, v_w_out=v_w_out, v_norm_mix_g=v_norm_mix_g, v_norm_mlp_g=v_norm_mlp_g, v_w_ff1=v_w_ff1, v_w_ff2=v_w_ff2)
    weights = {n: given[n] for n in TWIN_WEIGHTS}
    shared = {n: given[n] for n in SHARED_INPUTS}
    per_example = {n: given[n] for n in ['x']}
    grad_fn = _jax.value_and_grad(_loss, argnums=(0, 1))

    def one_microbatch(ex, loss_target):
        ex = dict(ex)
        diff = ex.pop(TWIN_DIFF_INPUT)
        return grad_fn(weights, diff, {**shared, **ex}, loss_target)

    if N_MICROBATCH == 1:
        loss, (grad_w, grad_x) = one_microbatch(per_example, given["loss_target"])
    else:
        def body(carry, xs):
            loss_sum, grad_sum = carry
            l_k, (gw_k, gx_k) = one_microbatch(xs[0], xs[1])
            with _jax.named_scope("update"):
                return (loss_sum + l_k, _jax.tree.map(_jnp.add, grad_sum, gw_k)), gx_k

        init = (_jnp.zeros((), _jnp.float32), _jax.tree.map(_jnp.zeros_like, weights))
        (loss, grad_w), grad_x = _jax.lax.scan(body, init, (per_example, given["loss_target"]))
    with _jax.named_scope("update"):
        delta_w, new_m, new_v = {}, {}, {}
        for n in TWIN_WEIGHTS:
            delta_w[n], new_m[n], new_v[n] = _adamw(weights[n], grad_w[n], given["m_" + n], given["v_" + n])
    return (loss, grad_x, *[grad_w[n] for n in TWIN_WEIGHTS], *[delta_w[n] for n in TWIN_WEIGHTS],
            *[new_m[n] for n in TWIN_WEIGHTS], *[new_v[n] for n in TWIN_WEIGHTS])
```

```python
import functools

import jax
import jax.numpy as jnp
from jax import lax
from jax.experimental import pallas as pl
from jax.experimental.pallas import tpu as pltpu

F32, BF16 = jnp.float32, jnp.bfloat16
MESH = pl.DeviceIdType.MESH
N_CHIPS = 4
N_DEV = 8

EPS = 1e-6
CHUNK = 64
HEAD_DIM = 128
CONV_WIDTH = 4
GMLP_GROUPS = 8
GMLP_BLOCK = 128
N_BRANCHES = 3
ADAM_LR, ADAM_B1, ADAM_B2, ADAM_EPS, ADAM_WD, ADAM_STEP = 0.001, 0.9, 0.999, 1e-08, 0.01, 10

V7X_VMEM_BYTES = 64 * 1024 * 1024
VMEM_LIMIT = V7X_VMEM_BYTES - 8 * 1024 * 1024
LANES = 128
MATMUL_TILE = 1024
ATTN_TILE = 256
ELEMENTWISE_BLOCK_ELEMS = 512 * 1024
SMALL_ROW_ALIGN = 256

WEIGHT_NAMES = ("w_in", "conv_w", "a_log", "dt_bias", "gdn_norm_g", "gmlp_ln_g", "w_spatial", "b_spatial",
                "sba_q_g", "sba_k_g", "w_out_a", "w_out_b", "w_out_c", "w_out", "norm_mix_g", "norm_mlp_g",
                "w_ff1", "w_ff2")
REPLICATED_NAMES = ("a_log", "dt_bias", "gdn_norm_g", "gmlp_ln_g", "w_spatial", "b_spatial", "sba_q_g",
                    "sba_k_g", "norm_mix_g", "norm_mlp_g")


def _round_up(n, m):
    return (n + m - 1) // m * m


def _tile(dim, pref):
    if dim <= pref:
        return dim
    t = pref // LANES * LANES
    while t >= LANES:
        if dim % t == 0:
            return t
        t -= LANES
    return dim


def _params(*semantics):
    return pltpu.CompilerParams(dimension_semantics=semantics, vmem_limit_bytes=VMEM_LIMIT)


_DOT_DIMS = {"nn": ((1,), (0,)), "nt": ((1,), (1,)), "tn": ((0,), (0,))}


def _matmul(a, b, mode, out_dtype, name):
    if mode == "tn":
        k_dim, m_dim = a.shape
    else:
        m_dim, k_dim = a.shape
    n_dim = b.shape[0] if mode == "nt" else b.shape[1]
    tm, tn, tk = _tile(m_dim, MATMUL_TILE), _tile(n_dim, MATMUL_TILE), _tile(k_dim, MATMUL_TILE)
    nk = k_dim // tk
    dims = (_DOT_DIMS[mode], ((), ()))
    if mode == "tn":
        a_spec = pl.BlockSpec((tk, tm), lambda i, j, k: (k, i))
    else:
        a_spec = pl.BlockSpec((tm, tk), lambda i, j, k: (i, k))
    if mode == "nt":
        b_spec = pl.BlockSpec((tn, tk), lambda i, j, k: (j, k))
    else:
        b_spec = pl.BlockSpec((tk, tn), lambda i, j, k: (k, j))

    def body(a_ref, b_ref, o_ref, *scratch):
        part = lax.dot_general(a_ref[...], b_ref[...], dims, preferred_element_type=F32)
        if nk == 1:
            o_ref[...] = part.astype(out_dtype)
            return
        acc_ref, = scratch
        k = pl.program_id(2)

        @pl.when(k == 0)
        def _():
            acc_ref[...] = part

        @pl.when(k > 0)
        def _():
            acc_ref[...] += part

        @pl.when(k == nk - 1)
        def _():
            o_ref[...] = acc_ref[...].astype(out_dtype)

    return pl.pallas_call(
        body,
        name=name,
        grid=(m_dim // tm, n_dim // tn, nk),
        in_specs=[a_spec, b_spec],
        out_specs=pl.BlockSpec((tm, tn), lambda i, j, k: (i, j)),
        out_shape=jax.ShapeDtypeStruct((m_dim, n_dim), out_dtype),
        scratch_shapes=[] if nk == 1 else [pltpu.VMEM((tm, tn), F32)],
        compiler_params=_params("parallel", "parallel", "arbitrary"),
    )(a, b)


@functools.partial(jax.custom_vjp, nondiff_argnums=(2,))
def _linear(a, w, name):
    return _matmul(a.astype(BF16), w, "nn", F32, name + "_fwd")


def _linear_fwd(a, w, name):
    a16 = a.astype(BF16)
    return _matmul(a16, w, "nn", F32, name + "_fwd"), (a16, w)


def _linear_bwd(name, res, g):
    a16, w = res
    g16 = g.astype(BF16)
    da = _matmul(g16, w, "nt", F32, name + "_dgrad")
    dw = _matmul(a16, g16, "tn", BF16, name + "_wgrad")
    return da, dw


_linear.defvjp(_linear_fwd, _linear_bwd)


def _split3_dot(x, ones_mat):
    hi = x.astype(BF16)
    rest = x - hi.astype(F32)
    mid = rest.astype(BF16)
    lo = (rest - mid.astype(F32)).astype(BF16)
    dot = lambda p: jnp.dot(p, ones_mat, preferred_element_type=F32)
    return dot(hi) + dot(mid) + dot(lo)


def _dot_nt(a, b):
    return lax.dot_general(a, b, (((1,), (1,)), ((), ())), preferred_element_type=F32)


def _dot_tn(a, b):
    return lax.dot_general(a, b, (((0,), (0,)), ((), ())), preferred_element_type=F32)


def _sba_scores(q16, k16, q_block, k_block, tile, scale):
    z = _dot_nt(q16, k16) * scale
    row = lax.broadcasted_iota(jnp.int32, (tile, tile), 0) + q_block * tile
    col = lax.broadcasted_iota(jnp.int32, (tile, tile), 1) + k_block * tile
    mask = col < row
    log_sig = jnp.minimum(z, 0.0) - jnp.log(1.0 + jnp.exp(-jnp.abs(z)))
    log_keep = jnp.where(mask, log_sig - z, 0.0)
    return mask, log_sig, log_keep


def _sba_forward(q, k, v):
    t_len, width = q.shape
    heads = width // HEAD_DIM
    tile = min(ATTN_TILE, t_len)
    scale = HEAD_DIM ** -0.5

    def body(q_ref, k_ref, v_ref, o_ref, r_ref):
        i = pl.program_id(1)
        q16 = q_ref[...].astype(BF16)
        ri = lax.broadcasted_iota(jnp.int32, (tile, tile), 0)
        ci = lax.broadcasted_iota(jnp.int32, (tile, tile), 1)
        later = (ri > ci).astype(BF16)

        def step(jj, carry):
            run, acc = carry
            j = i - jj
            off = pl.multiple_of(j * tile, tile)
            k16 = k_ref[pl.ds(off, tile), :].astype(BF16)
            v16 = v_ref[pl.ds(off, tile), :].astype(BF16)
            mask, log_sig, log_keep = _sba_scores(q16, k16, i, j, tile, scale)
            suffix = _split3_dot(log_keep, later) + run
            att = jnp.where(mask, jnp.exp(log_sig + suffix), 0.0)
            acc = acc + jnp.dot(att.astype(BF16), v16, preferred_element_type=F32)
            run = run + jnp.sum(log_keep, axis=1, keepdims=True)
            return run, acc

        run, acc = lax.fori_loop(0, i + 1, step, (jnp.zeros((tile, 1), F32), jnp.zeros((tile, HEAD_DIM), F32)))
        o_ref[...] = acc
        r_ref[...] = run

    return pl.pallas_call(
        body,
        name="sba_fwd",
        grid=(heads, t_len // tile),
        in_specs=[pl.BlockSpec((tile, HEAD_DIM), lambda h, i: (i, h)),
                  pl.BlockSpec((t_len, HEAD_DIM), lambda h, i: (0, h)),
                  pl.BlockSpec((t_len, HEAD_DIM), lambda h, i: (0, h))],
        out_specs=[pl.BlockSpec((tile, HEAD_DIM), lambda h, i: (i, h)),
                   pl.BlockSpec((None, tile, 1), lambda h, i: (h, i, 0))],
        out_shape=[jax.ShapeDtypeStruct((t_len, width), F32), jax.ShapeDtypeStruct((heads, t_len, 1), F32)],
        compiler_params=_params("parallel", "arbitrary"),
    )(q, k, v)


def _sba_backward(q, k, v, total, do):
    t_len, width = q.shape
    heads = width // HEAD_DIM
    tile = min(ATTN_TILE, t_len)
    scale = HEAD_DIM ** -0.5

    def body(q_ref, k_ref, v_ref, r_ref, do_ref, dq_ref, dk_ref, dv_ref):
        i = pl.program_id(1)

        @pl.when(i == 0)
        def _():
            dk_ref[...] = jnp.zeros_like(dk_ref)
            dv_ref[...] = jnp.zeros_like(dv_ref)

        q16 = q_ref[...].astype(BF16)
        do16 = do_ref[...].astype(BF16)
        tot = r_ref[...]
        ri = lax.broadcasted_iota(jnp.int32, (tile, tile), 0)
        ci = lax.broadcasted_iota(jnp.int32, (tile, tile), 1)
        upto = (ri <= ci).astype(BF16)
        before = (ri < ci).astype(BF16)

        def step(j, carry):
            keep_left, w_left, dq = carry
            off = pl.multiple_of(j * tile, tile)
            k16 = k_ref[pl.ds(off, tile), :].astype(BF16)
            v16 = v_ref[pl.ds(off, tile), :].astype(BF16)
            mask, log_sig, log_keep = _sba_scores(q16, k16, i, j, tile, scale)
            suffix = tot - keep_left - _split3_dot(log_keep, upto)
            att = jnp.where(mask, jnp.exp(log_sig + suffix), 0.0)
            w = att * _dot_nt(do16, v16)
            d_keep = w_left + _split3_dot(w, before)
            sig = jnp.exp(log_sig)
            dz = jnp.where(mask, w * (1.0 - sig) - sig * d_keep, 0.0) * scale
            dz16 = dz.astype(BF16)
            dq = dq + jnp.dot(dz16, k16, preferred_element_type=F32)
            dk_ref[pl.ds(off, tile), :] += _dot_tn(dz16, q16)
            dv_ref[pl.ds(off, tile), :] += _dot_tn(att.astype(BF16), do16)
            keep_left = keep_left + jnp.sum(log_keep, axis=1, keepdims=True)
            w_left = w_left + jnp.sum(w, axis=1, keepdims=True)
            return keep_left, w_left, dq

        zero = jnp.zeros((tile, 1), F32)
        _, _, dq = lax.fori_loop(0, i + 1, step, (zero, zero, jnp.zeros((tile, HEAD_DIM), F32)))
        dq_ref[...] = dq

    q_spec = pl.BlockSpec((tile, HEAD_DIM), lambda h, i: (i, h))
    kv_spec = pl.BlockSpec((t_len, HEAD_DIM), lambda h, i: (0, h))
    return pl.pallas_call(
        body,
        name="sba_bwd",
        grid=(heads, t_len // tile),
        in_specs=[q_spec, kv_spec, kv_spec, pl.BlockSpec((None, tile, 1), lambda h, i: (h, i, 0)), q_spec],
        out_specs=[q_spec, kv_spec, kv_spec],
        out_shape=[jax.ShapeDtypeStruct((t_len, width), F32)] * 3,
        compiler_params=_params("parallel", "arbitrary"),
    )(q, k, v, total, do)


@jax.custom_vjp
def _stick_breaking(q, k, v):
    return _sba_forward(q, k, v)[0]


def _stick_breaking_fwd(q, k, v):
    o, total = _sba_forward(q, k, v)
    return o, (q, k, v, total)


def _stick_breaking_bwd(res, do):
    q, k, v, total = res
    return tuple(_sba_backward(q, k, v, total, do))


_stick_breaking.defvjp(_stick_breaking_fwd, _stick_breaking_bwd)


def _contract(a16, b16, ca, cb):
    return lax.dot_general(a16, b16, (((ca,), (cb,)), ((), ())), preferred_element_type=F32)


def _pdot_raw(a, b, ca, cb):
    a_hi = a.astype(BF16)
    a_lo = (a - a_hi.astype(F32)).astype(BF16)
    b_hi = b.astype(BF16)
    b_lo = (b - b_hi.astype(F32)).astype(BF16)
    return _contract(a_hi, b_hi, ca, cb) + _contract(a_hi, b_lo, ca, cb) + _contract(a_lo, b_hi, ca, cb)


def _bdot_raw(a, b, ca, cb):
    return _contract(a.astype(BF16), b.astype(BF16), ca, cb)


def _make_dot(raw):
    @functools.partial(jax.custom_vjp, nondiff_argnums=(2, 3))
    def dot(a, b, ca, cb):
        return raw(a, b, ca, cb)

    def fwd(a, b, ca, cb):
        return raw(a, b, ca, cb), (a, b)

    def bwd(ca, cb, res, g):
        a, b = res
        da = raw(g, b, 1, 1 - cb) if ca == 1 else raw(b, g, 1 - cb, 1)
        db = raw(a, g, 1 - ca, 0) if cb == 0 else raw(g, a, 0, 1 - ca)
        return da, db

    dot.defvjp(fwd, bwd)
    return dot


_pdot = _make_dot(_pdot_raw)
_bdot = _make_dot(_bdot_raw)


@jax.custom_vjp
def _unit_lower_inverse(low):
    size = low.shape[0]
    eye = (lax.broadcasted_iota(jnp.int32, (size, size), 0) == lax.broadcasted_iota(jnp.int32, (size, size), 1)).astype(F32)
    inv = eye - low
    power = low
    span = 2
    while span < size:
        power = _pdot_raw(power, power, 1, 0)
        inv = inv + _pdot_raw(inv, power, 1, 0)
        span *= 2
    return inv


def _unit_lower_inverse_fwd(low):
    inv = _unit_lower_inverse(low)
    return inv, inv


def _unit_lower_inverse_bwd(inv, g):
    return (-_pdot_raw(_pdot_raw(inv, g, 0, 0), inv, 1, 1),)


_unit_lower_inverse.defvjp(_unit_lower_inverse_fwd, _unit_lower_inverse_bwd)


def _gdn_chunk(state, q, k, v, g_row, b_row):
    size = q.shape[0]
    ri = lax.broadcasted_iota(jnp.int32, (size, size), 0)
    ci = lax.broadcasted_iota(jnp.int32, (size, size), 1)
    eye, incl, strict = ri == ci, ci <= ri, ci < ri
    g_rowb = jnp.broadcast_to(g_row, (size, size))
    g_col = jnp.sum(jnp.where(eye, g_rowb, 0.0), axis=1, keepdims=True)
    b_col = jnp.sum(jnp.where(eye, jnp.broadcast_to(b_row, (size, size)), 0.0), axis=1, keepdims=True)
    gc_col = jnp.sum(jnp.where(incl, g_rowb, 0.0), axis=1, keepdims=True)
    gc_row = jnp.sum(jnp.where(ri <= ci, jnp.broadcast_to(g_col, (size, size)), 0.0), axis=0, keepdims=True)
    g_last = jnp.sum(g_row, axis=1, keepdims=True)
    decay = jnp.where(incl, jnp.exp(jnp.where(incl, gc_col - gc_row, 0.0)), 0.0)
    kb = k * b_col
    low = jnp.where(strict, _bdot(kb, k, 1, 1) * decay, 0.0)
    inv = _unit_lower_inverse(low)
    grow = jnp.exp(gc_col)
    u = _pdot(inv, v * b_col, 1, 0)
    w = _pdot(inv, kb * grow, 1, 0)
    intra = _bdot(q, k, 1, 1) * decay
    v_new = u - _bdot(w, state, 1, 0)
    o = _bdot(q * grow, state, 1, 0) + _bdot(intra, v_new, 1, 0)
    k_dec = k * jnp.exp(g_last - gc_col)
    new_state = state * jnp.exp(g_last) + _bdot(k_dec, v_new, 0, 0)
    return new_state, o


def _gdn_specs(t_len, n_chunks):
    seq = pl.BlockSpec((t_len, HEAD_DIM), lambda h: (0, h))
    gate = pl.BlockSpec((None, n_chunks, CHUNK), lambda h: (h, 0, 0))
    states = pl.BlockSpec((None, n_chunks, HEAD_DIM, HEAD_DIM), lambda h: (h, 0, 0, 0))
    return seq, gate, states


def _gdn_forward(q, k, v, g, beta):
    t_len, width = q.shape
    heads = width // HEAD_DIM
    n_chunks = t_len // CHUNK
    seq, gate, states = _gdn_specs(t_len, n_chunks)

    def body(q_ref, k_ref, v_ref, g_ref, b_ref, o_ref, s_ref):
        def step(n, state):
            rows = pl.ds(pl.multiple_of(n * CHUNK, CHUNK), CHUNK)
            s_ref[n] = state
            state, o = _gdn_chunk(state, q_ref[rows, :], k_ref[rows, :], v_ref[rows, :],
                                  g_ref[pl.ds(n, 1), :], b_ref[pl.ds(n, 1), :])
            o_ref[rows, :] = o
            return state

        lax.fori_loop(0, n_chunks, step, jnp.zeros((HEAD_DIM, HEAD_DIM), F32))

    return pl.pallas_call(
        body,
        name="gdn_fwd",
        grid=(heads,),
        in_specs=[seq, seq, seq, gate, gate],
        out_specs=[seq, states],
        out_shape=[jax.ShapeDtypeStruct((t_len, width), F32),
                   jax.ShapeDtypeStruct((heads, n_chunks, HEAD_DIM, HEAD_DIM), F32)],
        compiler_params=_params("parallel"),
    )(q, k, v, g, beta)


def _gdn_backward(q, k, v, g, beta, starts, do):
    t_len, width = q.shape
    heads = width // HEAD_DIM
    n_chunks = t_len // CHUNK
    seq, gate, states = _gdn_specs(t_len, n_chunks)

    def body(q_ref, k_ref, v_ref, g_ref, b_ref, s_ref, do_ref, dq_ref, dk_ref, dv_ref, dg_ref, db_ref):
        def step(m, d_state):
            n = n_chunks - 1 - m
            rows = pl.ds(pl.multiple_of(n * CHUNK, CHUNK), CHUNK)
            _, pull = jax.vjp(_gdn_chunk, s_ref[n], q_ref[rows, :], k_ref[rows, :], v_ref[rows, :],
                              g_ref[pl.ds(n, 1), :], b_ref[pl.ds(n, 1), :])
            d_state, dq, dk, dv, dg, db = pull((d_state, do_ref[rows, :]))
            dq_ref[rows, :] = dq
            dk_ref[rows, :] = dk
            dv_ref[rows, :] = dv
            dg_ref[pl.ds(n, 1), :] = dg
            db_ref[pl.ds(n, 1), :] = db
            return d_state

        lax.fori_loop(0, n_chunks, step, jnp.zeros((HEAD_DIM, HEAD_DIM), F32))

    return pl.pallas_call(
        body,
        name="gdn_bwd",
        grid=(heads,),
        in_specs=[seq, seq, seq, gate, gate, states, seq],
        out_specs=[seq, seq, seq, gate, gate],
        out_shape=[jax.ShapeDtypeStruct((t_len, width), F32)] * 3
        + [jax.ShapeDtypeStruct((heads, n_chunks, CHUNK), F32)] * 2,
        compiler_params=_params("parallel"),
    )(q, k, v, g, beta, starts, do)


@jax.custom_vjp
def _gated_delta_rule(q, k, v, g, beta):
    return _gdn_forward(q, k, v, g, beta)[0]


def _gated_delta_rule_fwd(q, k, v, g, beta):
    o, starts = _gdn_forward(q, k, v, g, beta)
    return o, (q, k, v, g, beta, starts)


def _gated_delta_rule_bwd(res, do):
    return tuple(_gdn_backward(*res, do))


_gated_delta_rule.defvjp(_gated_delta_rule_fwd, _gated_delta_rule_bwd)


def _rms_norm(x, gain):
    return x * lax.rsqrt(jnp.mean(x * x, axis=-1, keepdims=True) + EPS) * gain


def _layer_norm(x, gain):
    xc = x - jnp.mean(x, axis=-1, keepdims=True)
    return xc * lax.rsqrt(jnp.mean(xc * xc, axis=-1, keepdims=True) + EPS) * gain


def _l2_norm(x):
    return x * lax.rsqrt(jnp.sum(x * x, axis=-1, keepdims=True) + EPS)


def _causal_conv(x, w):
    t_len = x.shape[0]
    xp = jnp.pad(x, ((CONV_WIDTH - 1, 0), (0, 0)))
    return sum(xp[i:i + t_len] * w[i] for i in range(CONV_WIDTH))


def _packed_sections(d_model):
    heads = d_model // 256
    mixer = heads * HEAD_DIM
    gmlp = d_model // 2
    widths = (("gdn_qkv", 3 * mixer), ("gdn_gate", mixer), ("gmlp_uv", 2 * gmlp), ("sba_qkv", 3 * mixer),
              ("gates", N_BRANCHES * d_model), ("gdn_ab", 2 * heads))
    sections, off = {}, 0
    for name, width in widths:
        sections[name] = (off, width)
        off += width
    return sections, _round_up(off, 512)


def _pack_w_in(w, d_model):
    sections, total = _packed_sections(d_model)
    heads = d_model // 256
    mixer = heads * HEAD_DIM
    ref_off = {"gdn_qkv": 0, "gdn_ab": 3 * mixer, "gdn_gate": 3 * mixer + 2 * heads}
    ref_off["gmlp_uv"] = ref_off["gdn_gate"] + mixer
    ref_off["sba_qkv"] = ref_off["gmlp_uv"] + d_model
    ref_off["gates"] = ref_off["sba_qkv"] + 3 * mixer
    parts = [w[..., ref_off[name]:ref_off[name] + width] for name, (_, width) in sections.items()]
    used = sum(width for _, width in sections.values())
    parts.append(jnp.zeros(w.shape[:-1] + (total - used,), w.dtype))
    return jnp.concatenate(parts, axis=-1)


def _unpack_w_in(wp, d_model):
    sections, _ = _packed_sections(d_model)
    order = ("gdn_qkv", "gdn_ab", "gdn_gate", "gmlp_uv", "sba_qkv", "gates")
    return jnp.concatenate([wp[..., sections[n][0]:sections[n][0] + sections[n][1]] for n in order], axis=-1)


def _mixer(h, p, layer):
    t_len, d_model = h.shape
    heads = d_model // 256
    mixer = heads * HEAD_DIM
    gmlp = d_model // 2
    sections, _ = _packed_sections(d_model)
    z = _linear(h, p["w_in"][layer], "w_in")
    cut = lambda name: z[:, sections[name][0]:sections[name][0] + sections[name][1]]
    per_head = lambda t: t.reshape(t_len, heads, HEAD_DIM)

    qkv = jax.nn.silu(_causal_conv(cut("gdn_qkv"), p["conv_w"][layer]))
    qa = _l2_norm(per_head(qkv[:, :mixer])) * HEAD_DIM ** -0.5
    ka = _l2_norm(per_head(qkv[:, mixer:2 * mixer]))
    va = qkv[:, 2 * mixer:]
    ab = cut("gdn_ab")
    to_chunks = lambda t: jnp.transpose(t).reshape(heads, t_len // CHUNK, CHUNK)
    beta = to_chunks(jax.nn.sigmoid(ab[:, heads:]))
    g = to_chunks(-jnp.exp(p["a_log"][layer]) * jax.nn.softplus(ab[:, :heads] + p["dt_bias"][layer]))
    oa = _gated_delta_rule(qa.reshape(t_len, mixer), ka.reshape(t_len, mixer), va, g, beta)
    oa = _rms_norm(per_head(oa), p["gdn_norm_g"][layer]) * jax.nn.silu(per_head(cut("gdn_gate")))
    branch_a = _linear(oa.reshape(t_len, mixer), p["w_out_a"][layer], "w_out_branch")

    uv = jax.nn.gelu(cut("gmlp_uv"), approximate=False)
    u, vb = uv[:, :gmlp], uv[:, gmlp:]
    group_dim = gmlp // GMLP_GROUPS
    vb = _layer_norm(vb, p["gmlp_ln_g"][layer]).reshape(t_len // GMLP_BLOCK, GMLP_BLOCK, GMLP_GROUPS, group_dim)
    pos = jnp.arange(GMLP_BLOCK) // CHUNK
    ws = jnp.where((pos[None, :] <= pos[:, None])[None], p["w_spatial"][layer], 0.0)
    s = jnp.einsum("gts,nsgc->ntgc", ws.astype(BF16), vb.astype(BF16), preferred_element_type=F32)
    s = s + jnp.transpose(p["b_spatial"][layer])[None, :, :, None]
    branch_b = _linear(u * s.reshape(t_len, gmlp), p["w_out_b"][layer], "w_out_branch")

    sba = cut("sba_qkv")
    qc = _rms_norm(per_head(sba[:, :mixer]), p["sba_q_g"][layer]).reshape(t_len, mixer)
    kc = _rms_norm(per_head(sba[:, mixer:2 * mixer]), p["sba_k_g"][layer]).reshape(t_len, mixer)
    oc = _stick_breaking(qc, kc, sba[:, 2 * mixer:])
    branch_c = _linear(oc, p["w_out_c"][layer], "w_out_branch")

    gates = jax.nn.sigmoid(cut("gates"))
    y = (gates[:, :d_model] * branch_a + gates[:, d_model:2 * d_model] * branch_b
         + gates[:, 2 * d_model:] * branch_c)
    return _linear(y, p["w_out"][layer], "w_out")


def _local_loss(x, p, target):
    depth = len(p["w_in"])
    for layer in range(depth):
        x = x + _mixer(_rms_norm(x, p["norm_mix_g"][layer]), p, layer)
        f = _linear(_rms_norm(x, p["norm_mlp_g"][layer]), p["w_ff1"][layer], "w_ff1")
        x = x + _linear(jnp.square(jax.nn.relu(f)), p["w_ff2"][layer], "w_ff2")
    err = jnp.square(x - target)
    return 0.5 * jnp.sum(jnp.mean(err, axis=-1))


ANY = pl.BlockSpec(memory_space=pl.ANY)


def _place():
    x, y, c = lax.axis_index("x"), lax.axis_index("y"), lax.axis_index("c")
    other_chips = [(1 - x, y), (x, 1 - y), (1 - x, 1 - y)]
    return x, y, c, other_chips


def _gather_chips(w):
    depth, rows, cols = w.shape
    half = rows // 2

    def body(w_ref, out_ref, send_sems, recv_sems, local_sem):
        x, y, c, other_chips = _place()
        sibling = (x, y, 1 - c)
        mine = pl.ds(c * half, half)
        theirs = pl.ds((1 - c) * half, half)

        def slab(chip, part):
            return out_ref.at[2 * chip[0] + chip[1], :, part, :]

        def copy(sem, src, dst, to):
            return pltpu.make_async_remote_copy(src_ref=src, dst_ref=dst, send_sem=send_sems.at[sem],
                                                recv_sem=recv_sems.at[sem], device_id=to, device_id_type=MESH)

        own = pltpu.make_async_copy(w_ref, out_ref.at[2 * x + y], local_sem)
        own.start()
        sent = [copy(j, w_ref.at[:, mine, :], slab((x, y), mine), (*chip, c)) for j, chip in enumerate(other_chips)]
        for cp in sent:
            cp.start()
        passed = [copy(3 + j, slab(chip, mine), slab(chip, mine), sibling) for j, chip in enumerate(other_chips)]
        for j, chip in enumerate(other_chips):
            copy(j, slab(chip, mine), slab(chip, mine), sibling).wait_recv()
            passed[j].start()
        for j, chip in enumerate(other_chips):
            copy(3 + j, slab(chip, theirs), slab(chip, theirs), sibling).wait_recv()
        for cp in sent + passed:
            cp.wait_send()
        own.wait()

    return pl.pallas_call(
        body,
        name="gather_chips",
        in_specs=[ANY],
        out_specs=ANY,
        out_shape=jax.ShapeDtypeStruct((N_CHIPS, depth, rows, cols), w.dtype),
        scratch_shapes=[pltpu.SemaphoreType.DMA((6,)), pltpu.SemaphoreType.DMA((6,)), pltpu.SemaphoreType.DMA],
    )(w)


def _pair_exchange(grad):
    chips, depth, rows, cols = grad.shape
    half = rows // 2

    def body(g_ref, out_ref, send_sem, recv_sem):
        x, y, c, _ = _place()
        cp = pltpu.make_async_remote_copy(src_ref=g_ref.at[:, :, pl.ds((1 - c) * half, half), :], dst_ref=out_ref,
                                          send_sem=send_sem, recv_sem=recv_sem, device_id=(x, y, 1 - c),
                                          device_id_type=MESH)
        cp.start()
        cp.wait()

    return pl.pallas_call(
        body,
        name="pair_exchange",
        in_specs=[ANY],
        out_specs=ANY,
        out_shape=jax.ShapeDtypeStruct((chips, depth, half, cols), grad.dtype),
        scratch_shapes=[pltpu.SemaphoreType.DMA, pltpu.SemaphoreType.DMA],
    )(grad)


def _chip_exchange(part):
    def body(p_ref, out_ref, send_sems, recv_sems, local_sem):
        x, y, c, other_chips = _place()
        me = 2 * x + y
        own = pltpu.make_async_copy(p_ref.at[me], out_ref.at[me], local_sem)
        own.start()
        copies = []
        for j, chip in enumerate(other_chips):
            there = 2 * chip[0] + chip[1]
            copies.append(pltpu.make_async_remote_copy(src_ref=p_ref.at[there], dst_ref=out_ref.at[me],
                                                       send_sem=send_sems.at[j], recv_sem=recv_sems.at[j],
                                                       device_id=(*chip, c), device_id_type=MESH))
        for cp in copies:
            cp.start()
        for cp in copies:
            cp.wait()
        own.wait()

    return pl.pallas_call(
        body,
        name="chip_exchange",
        in_specs=[ANY],
        out_specs=ANY,
        out_shape=jax.ShapeDtypeStruct(part.shape, part.dtype),
        scratch_shapes=[pltpu.SemaphoreType.DMA((3,)), pltpu.SemaphoreType.DMA((3,)), pltpu.SemaphoreType.DMA],
    )(part)


def _pair_share(half_sum):
    depth, half, cols = half_sum.shape

    def body(h_ref, out_ref, send_sem, recv_sem, local_sem):
        x, y, c, _ = _place()
        mine = out_ref.at[:, pl.ds(c * half, half), :]
        own = pltpu.make_async_copy(h_ref, mine, local_sem)
        own.start()
        cp = pltpu.make_async_remote_copy(src_ref=h_ref, dst_ref=mine, send_sem=send_sem, recv_sem=recv_sem,
                                          device_id=(x, y, 1 - c), device_id_type=MESH)
        cp.start()
        cp.wait()
        own.wait()

    return pl.pallas_call(
        body,
        name="pair_share",
        in_specs=[ANY],
        out_specs=ANY,
        out_shape=jax.ShapeDtypeStruct((depth, 2 * half, cols), half_sum.dtype),
        scratch_shapes=[pltpu.SemaphoreType.DMA, pltpu.SemaphoreType.DMA, pltpu.SemaphoreType.DMA],
    )(half_sum)


def _all_gather_rows(block):
    m_per, n = block.shape

    def body(x_ref, out_ref, send_sems, recv_sems, local_sem):
        x, y, c, other_chips = _place()
        me, sibling = (x, y, c), (x, y, 1 - c)

        def rows(px, py, pc):
            return out_ref.at[pl.ds((4 * px + 2 * py + pc) * m_per, m_per), :]

        def copy(k, blk, to, src=None):
            return pltpu.make_async_remote_copy(src_ref=rows(*blk) if src is None else src, dst_ref=rows(*blk),
                                                send_sem=send_sems.at[k], recv_sem=recv_sems.at[k],
                                                device_id=to, device_id_type=MESH)

        mine = pltpu.make_async_copy(x_ref, rows(*me), local_sem)
        mine.start()
        first = [copy(0, me, sibling, src=x_ref)]
        first += [copy(1 + j, me, (*chip, c), src=x_ref) for j, chip in enumerate(other_chips)]
        for cp in first:
            cp.start()
        passed = [copy(4 + j, (*chip, c), sibling) for j, chip in enumerate(other_chips)]
        for j, chip in enumerate(other_chips):
            copy(1 + j, (*chip, c), me).wait_recv()
            passed[j].start()
        copy(0, sibling, me).wait_recv()
        for j, chip in enumerate(other_chips):
            copy(4 + j, (*chip, 1 - c), me).wait_recv()
        for cp in first + passed:
            cp.wait_send()
        mine.wait()

    return pl.pallas_call(
        body,
        name="all_gather_rows",
        in_specs=[ANY],
        out_specs=ANY,
        out_shape=jax.ShapeDtypeStruct((N_DEV * m_per, n), block.dtype),
        scratch_shapes=[pltpu.SemaphoreType.DMA((7,)), pltpu.SemaphoreType.DMA((7,)), pltpu.SemaphoreType.DMA],
    )(block)


def _row_tile(rows, cols, multiple):
    budget = max(multiple, ELEMENTWISE_BLOCK_ELEMS // _round_up(cols, LANES))
    t = multiple
    while t * 2 <= budget and rows % (t * 2) == 0:
        t *= 2
    return t if rows % t == 0 else rows


def _pair_sum(grad, theirs):
    chips, depth, rows, cols = grad.shape
    half = rows // 2
    tr = _row_tile(half, cols, 16)
    blocks = half // tr
    core = lax.axis_index("c").astype(jnp.int32).reshape(1)

    def body(c_ref, a_ref, b_ref, o_ref):
        o_ref[...] = (a_ref[...].astype(F32) + b_ref[...].astype(F32)).astype(o_ref.dtype)

    return pl.pallas_call(
        body,
        name="pair_sum",
        grid_spec=pltpu.PrefetchScalarGridSpec(
            num_scalar_prefetch=1,
            grid=(chips, depth, blocks),
            in_specs=[pl.BlockSpec((None, None, tr, cols), lambda s, l, r, c_ref: (s, l, c_ref[0] * blocks + r, 0)),
                      pl.BlockSpec((None, None, tr, cols), lambda s, l, r, c_ref: (s, l, r, 0))],
            out_specs=pl.BlockSpec((None, None, tr, cols), lambda s, l, r, c_ref: (s, l, r, 0)),
        ),
        out_shape=jax.ShapeDtypeStruct((chips, depth, half, cols), grad.dtype),
        compiler_params=_params("parallel", "parallel", "parallel"),
    )(core, grad, theirs)


def _chip_sum(parts):
    chips, depth, half, cols = parts.shape
    tr = _row_tile(half, cols, 16)

    def body(p_ref, o_ref):
        total = p_ref[0].astype(F32)
        for s in range(1, chips):
            total = total + p_ref[s].astype(F32)
        o_ref[...] = total

    return pl.pallas_call(
        body,
        name="chip_sum",
        grid=(depth, half // tr),
        in_specs=[pl.BlockSpec((chips, None, tr, cols), lambda l, r: (0, l, r, 0))],
        out_specs=pl.BlockSpec((None, tr, cols), lambda l, r: (l, r, 0)),
        out_shape=jax.ShapeDtypeStruct((depth, half, cols), F32),
        compiler_params=_params("parallel", "parallel"),
    )(parts)


def _device_sum(blocks, m_per):
    tr = _row_tile(m_per, LANES, 8)
    per = m_per // tr

    def body(*refs):
        o_ref = refs[-1]
        total = refs[0][...]
        for ref in refs[1:-1]:
            total = total + ref[...]
        o_ref[...] = total

    return pl.pallas_call(
        body,
        name="device_sum",
        grid=(per,),
        in_specs=[pl.BlockSpec((tr, LANES), functools.partial(lambda d, r: (d * per + r, 0), d)) for d in range(N_DEV)],
        out_specs=pl.BlockSpec((tr, LANES), lambda r: (r, 0)),
        out_shape=jax.ShapeDtypeStruct((m_per, LANES), F32),
        compiler_params=_params("parallel"),
    )(*([blocks] * N_DEV))


def _adamw(grad, w, m, v):
    depth, rows, cols = w.shape
    tr = _row_tile(rows, cols, 8) if rows % 8 == 0 else rows
    spec = pl.BlockSpec((None, tr, cols), lambda l, r: (l, r, 0))

    def body(g_ref, w_ref, m_ref, v_ref, go_ref, d_ref, mo_ref, vo_ref):
        g = g_ref[...]
        m_new = ADAM_B1 * m_ref[...] + (1.0 - ADAM_B1) * g
        v_new = ADAM_B2 * v_ref[...] + (1.0 - ADAM_B2) * jnp.square(g)
        m_hat = m_new / (1.0 - ADAM_B1 ** ADAM_STEP)
        v_hat = v_new / (1.0 - ADAM_B2 ** ADAM_STEP)
        go_ref[...] = g
        d_ref[...] = -ADAM_LR * (m_hat / (jnp.sqrt(v_hat) + ADAM_EPS) + ADAM_WD * w_ref[...])
        mo_ref[...] = m_new
        vo_ref[...] = v_new

    return pl.pallas_call(
        body,
        name="adamw",
        grid=(depth, rows // tr),
        in_specs=[spec] * 4,
        out_specs=[spec] * 4,
        out_shape=[jax.ShapeDtypeStruct(w.shape, F32)] * 4,
        compiler_params=_params("parallel", "parallel"),
    )(grad, w, m, v)


def _join_cols(gathered):
    return jnp.concatenate([gathered[s] for s in range(N_CHIPS)], axis=-1)


def _join_rows(gathered):
    return jnp.concatenate([gathered[s] for s in range(N_CHIPS)], axis=-2)


def _split_cols(full):
    depth, rows, cols = full.shape
    return jnp.transpose(full.reshape(depth, rows, N_CHIPS, cols // N_CHIPS), (2, 0, 1, 3))


def _split_rows(full):
    depth, rows, cols = full.shape
    return jnp.transpose(full.reshape(depth, N_CHIPS, rows // N_CHIPS, cols), (1, 0, 2, 3))


def _reduce_scatter(grad):
    pair = _pair_sum(grad, _pair_exchange(grad))
    half = _chip_sum(_chip_exchange(pair))
    return _pair_share(half)


def _pack_rows(arrays):
    parts, counts = [], []
    for a in arrays:
        flat = a.reshape(-1).astype(F32)
        n_rows = _round_up(flat.shape[0], LANES) // LANES
        parts.append(jnp.pad(flat, (0, n_rows * LANES - flat.shape[0])).reshape(n_rows, LANES))
        counts.append(n_rows)
    return jnp.concatenate(parts, axis=0), counts


def _pad_rows(a, rows):
    return jnp.pad(a, ((0, rows - a.shape[0]), (0, 0)))


def _unpack_rows(packed, counts, shapes):
    out, row = [], 0
    for n_rows, shape in zip(counts, shapes):
        size = 1
        for d in shape:
            size *= d
        out.append(packed[row:row + n_rows].reshape(-1)[:size].reshape(shape))
        row += n_rows
    return out


def kernel(x, w_in, conv_w, a_log, dt_bias, gdn_norm_g, gmlp_ln_g, w_spatial, b_spatial, sba_q_g, sba_k_g, w_out_a, w_out_b, w_out_c, w_out, norm_mix_g, norm_mlp_g, w_ff1, w_ff2, loss_target, m_w_in, m_conv_w, m_a_log, m_dt_bias, m_gdn_norm_g, m_gmlp_ln_g, m_w_spatial, m_b_spatial, m_sba_q_g, m_sba_k_g, m_w_out_a, m_w_out_b, m_w_out_c, m_w_out, m_norm_mix_g, m_norm_mlp_g, m_w_ff1, m_w_ff2, v_w_in, v_conv_w, v_a_log, v_dt_bias, v_gdn_norm_g, v_gmlp_ln_g, v_w_spatial, v_b_spatial, v_sba_q_g, v_sba_k_g, v_w_out_a, v_w_out_b, v_w_out_c, v_w_out, v_norm_mix_g, v_norm_mlp_g, v_w_ff1, v_w_ff2):
    given = dict(locals())
    weights = {n: given[n] for n in WEIGHT_NAMES}
    depth, d_model = w_in.shape[0], w_in.shape[1]
    chip = 2 * lax.axis_index("x") + lax.axis_index("y")

    w_out_abc = jnp.concatenate([w_out_a, w_out_b, w_out_c], axis=1)
    full_in = _pack_w_in(_join_cols(_gather_chips(w_in.astype(BF16))), d_model)
    full_abc = _join_cols(_gather_chips(w_out_abc.astype(BF16)))
    full_out = _join_rows(_gather_chips(w_out.astype(BF16)))
    full_ff1 = _join_cols(_gather_chips(w_ff1.astype(BF16)))
    full_ff2 = _join_rows(_gather_chips(w_ff2.astype(BF16)))
    conv_rows = depth * CONV_WIDTH
    conv_cols = conv_w.shape[-1]
    conv_all = _all_gather_rows(jnp.pad(conv_w.reshape(conv_rows, conv_cols), ((0, _round_up(conv_rows, 8) - conv_rows), (0, 0))))
    conv_all = conv_all.reshape(N_CHIPS, 2, _round_up(conv_rows, 8), conv_cols)[:, 0, :conv_rows]
    conv_full = jnp.concatenate([conv_all[s] for s in range(N_CHIPS)], axis=-1).reshape(depth, CONV_WIDTH, N_CHIPS * conv_cols)

    mixer = w_out_a.shape[1]
    layers = range(depth)
    params = {n: weights[n] for n in REPLICATED_NAMES}
    params.update(conv_w=conv_full,
                  w_in=[full_in[l] for l in layers],
                  w_out_a=[full_abc[l, :mixer] for l in layers],
                  w_out_b=[full_abc[l, mixer:2 * mixer] for l in layers],
                  w_out_c=[full_abc[l, 2 * mixer:] for l in layers],
                  w_out=[full_out[l] for l in layers],
                  w_ff1=[full_ff1[l] for l in layers],
                  w_ff2=[full_ff2[l] for l in layers])

    loss_local, (grad_x, grads) = jax.value_and_grad(_local_loss, argnums=(0, 1))(x[0], params, loss_target[0])

    stack = lambda name: jnp.stack(grads[name])
    summed = {
        "w_in": _reduce_scatter(_split_cols(_unpack_w_in(stack("w_in"), d_model))),
        "w_out_abc": _reduce_scatter(_split_cols(jnp.concatenate([stack("w_out_a"), stack("w_out_b"), stack("w_out_c")], axis=1))),
        "w_out": _reduce_scatter(_split_rows(stack("w_out"))),
        "w_ff1": _reduce_scatter(_split_cols(stack("w_ff1"))),
        "w_ff2": _reduce_scatter(_split_rows(stack("w_ff2"))),
    }
    rep_packed, rep_counts = _pack_rows([grads[n] for n in REPLICATED_NAMES])
    rep_rows = _round_up(rep_packed.shape[0], SMALL_ROW_ALIGN)
    rest_packed, rest_counts = _pack_rows([grads["conv_w"], loss_local])
    m_per = rep_rows + _round_up(rest_packed.shape[0], SMALL_ROW_ALIGN)
    packed = jnp.concatenate([_pad_rows(rep_packed, rep_rows), _pad_rows(rest_packed, m_per - rep_rows)], axis=0)
    total = _device_sum(_all_gather_rows(packed), m_per)
    conv_sum, loss = _unpack_rows(total[rep_rows:], rest_counts, [grads["conv_w"].shape, ()])
    conv_grad = lax.dynamic_slice_in_dim(conv_sum, chip * conv_cols, conv_cols, axis=2)

    out = {}
    for name in ("w_in", "w_out", "w_ff1", "w_ff2"):
        out[name] = _adamw(summed[name], weights[name], given["m_" + name], given["v_" + name])
    abc = _adamw(summed["w_out_abc"], w_out_abc,
                 jnp.concatenate([m_w_out_a, m_w_out_b, m_w_out_c], axis=1),
                 jnp.concatenate([v_w_out_a, v_w_out_b, v_w_out_c], axis=1))
    for i, name in enumerate(("w_out_a", "w_out_b", "w_out_c")):
        out[name] = tuple(t[:, i * mixer:(i + 1) * mixer] for t in abc)
    out["conv_w"] = _adamw(conv_grad, conv_w, m_conv_w, v_conv_w)
    rep_shapes = [weights[n].shape for n in REPLICATED_NAMES]
    pack3 = lambda prefix: _pad_rows(_pack_rows([given[prefix + n] for n in REPLICATED_NAMES])[0], rep_rows)[None]
    rep = _adamw(total[:rep_rows][None], pack3(""), pack3("m_"), pack3("v_"))
    rep = [_unpack_rows(t[0], rep_counts, rep_shapes) for t in rep]
    for i, name in enumerate(REPLICATED_NAMES):
        out[name] = tuple(t[i] for t in rep)

    results = [loss, grad_x[None]]
    for kind in range(4):
        results += [out[n][kind] for n in WEIGHT_NAMES]
    return tuple(results)
```

```python
import functools

import jax
import jax.numpy as jnp
from jax import lax
from jax.experimental import pallas as pl
from jax.experimental.pallas import tpu as pltpu

F32, BF16 = jnp.float32, jnp.bfloat16
MESH = pl.DeviceIdType.MESH
N_CHIPS = 4
N_DEV = 8

EPS = 1e-6
CHUNK = 64
HEAD_DIM = 128
CONV_WIDTH = 4
GMLP_GROUPS = 8
GMLP_BLOCK = 128
N_BRANCHES = 3
ADAM_LR, ADAM_B1, ADAM_B2, ADAM_EPS, ADAM_WD, ADAM_STEP = 0.001, 0.9, 0.999, 1e-08, 0.01, 10

V7X_VMEM_BYTES = 64 * 1024 * 1024
VMEM_LIMIT = V7X_VMEM_BYTES - 8 * 1024 * 1024
LANES = 128
MATMUL_TILE = 1024
MATMUL_K_TILE = 2048
ATTN_TILE = 256
ELEMENTWISE_BLOCK_ELEMS = 512 * 1024
SMALL_ROW_ALIGN = 256

WEIGHT_NAMES = ("w_in", "conv_w", "a_log", "dt_bias", "gdn_norm_g", "gmlp_ln_g", "w_spatial", "b_spatial",
                "sba_q_g", "sba_k_g", "w_out_a", "w_out_b", "w_out_c", "w_out", "norm_mix_g", "norm_mlp_g",
                "w_ff1", "w_ff2")
REPLICATED_NAMES = ("a_log", "dt_bias", "gdn_norm_g", "gmlp_ln_g", "w_spatial", "b_spatial", "sba_q_g",
                    "sba_k_g", "norm_mix_g", "norm_mlp_g")


def _round_up(n, m):
    return (n + m - 1) // m * m


def _tile(dim, pref):
    if dim <= pref:
        return dim
    t = pref // LANES * LANES
    while t >= LANES:
        if dim % t == 0:
            return t
        t -= LANES
    return dim


def _params(*semantics):
    return pltpu.CompilerParams(dimension_semantics=semantics, vmem_limit_bytes=VMEM_LIMIT)


_DOT_DIMS = {"nn": ((1,), (0,)), "nt": ((1,), (1,)), "tn": ((0,), (0,))}


def _logical_shape(shape, kind):
    if kind is None:
        return shape
    chips, rows, cols = shape
    return (rows, chips * cols) if kind == "col" else (chips * rows, cols)


def _weight_tiles(shape, kind, row_pref, col_pref):
    rows, cols = _logical_shape(shape, kind)
    tr = _tile(shape[1] if kind == "row" else rows, row_pref)
    tc = _tile(shape[2] if kind == "col" else cols, col_pref)
    return tr, tc


def _weight_spec(shape, kind, tr, tc, pick):
    if kind is None:
        return pl.BlockSpec((tr, tc), pick)
    if kind == "col":
        per = shape[2] // tc
        return pl.BlockSpec((None, tr, tc), lambda i, j, k: (pick(i, j, k)[1] // per, pick(i, j, k)[0], pick(i, j, k)[1] % per))
    per = shape[1] // tr
    return pl.BlockSpec((None, tr, tc), lambda i, j, k: (pick(i, j, k)[0] // per, pick(i, j, k)[0] % per, pick(i, j, k)[1]))


def _matmul(a, b, mode, out_dtype, name, kind=None, out_shape=None):
    if mode == "tn":
        k_dim, m_dim = a.shape
        n_dim = b.shape[1]
        out_shape = (m_dim, n_dim) if kind is None else out_shape
        tm, tn = _weight_tiles(out_shape, kind, MATMUL_TILE, MATMUL_TILE)
        tk = _tile(k_dim, MATMUL_K_TILE)
        a_spec = pl.BlockSpec((tk, tm), lambda i, j, k: (k, i))
        b_spec = pl.BlockSpec((tk, tn), lambda i, j, k: (k, j))
        out_spec = _weight_spec(out_shape, kind, tm, tn, lambda i, j, k: (i, j))
    else:
        m_dim, k_dim = a.shape
        tm = _tile(m_dim, MATMUL_TILE)
        if mode == "nn":
            _, n_dim = _logical_shape(b.shape, kind)
            tk, tn = _weight_tiles(b.shape, kind, MATMUL_K_TILE, MATMUL_TILE)
            b_spec = _weight_spec(b.shape, kind, tk, tn, lambda i, j, k: (k, j))
        else:
            n_dim, _ = _logical_shape(b.shape, kind)
            tn, tk = _weight_tiles(b.shape, kind, MATMUL_TILE, MATMUL_K_TILE)
            b_spec = _weight_spec(b.shape, kind, tn, tk, lambda i, j, k: (j, k))
        a_spec = pl.BlockSpec((tm, tk), lambda i, j, k: (i, k))
        out_shape = (m_dim, n_dim)
        out_spec = pl.BlockSpec((tm, tn), lambda i, j, k: (i, j))
    nk = k_dim // tk
    dims = (_DOT_DIMS[mode], ((), ()))

    def body(a_ref, b_ref, o_ref, *scratch):
        part = lax.dot_general(a_ref[...], b_ref[...], dims, preferred_element_type=F32)
        if nk == 1:
            o_ref[...] = part.astype(out_dtype)
            return
        acc_ref, = scratch
        k = pl.program_id(2)

        @pl.when(k == 0)
        def _():
            acc_ref[...] = part

        @pl.when(k > 0)
        def _():
            acc_ref[...] += part

        @pl.when(k == nk - 1)
        def _():
            o_ref[...] = acc_ref[...].astype(out_dtype)

    return pl.pallas_call(
        body,
        name=name,
        grid=(m_dim // tm, n_dim // tn, nk),
        in_specs=[a_spec, b_spec],
        out_specs=out_spec,
        out_shape=jax.ShapeDtypeStruct(out_shape, out_dtype),
        scratch_shapes=[] if nk == 1 else [pltpu.VMEM((tm, tn), F32)],
        compiler_params=_params("parallel", "parallel", "arbitrary"),
    )(a, b)


@functools.partial(jax.custom_vjp, nondiff_argnums=(2, 3))
def _linear(a, w, name, kind=None):
    return _matmul(a.astype(BF16), w, "nn", F32, name + "_fwd", kind)


def _linear_fwd(a, w, name, kind):
    a16 = a.astype(BF16)
    return _matmul(a16, w, "nn", F32, name + "_fwd", kind), (a16, w)


def _linear_bwd(name, kind, res, g):
    a16, w = res
    g16 = g.astype(BF16)
    da = _matmul(g16, w, "nt", F32, name + "_dgrad", kind)
    dw = _matmul(a16, g16, "tn", BF16, name + "_wgrad", kind, w.shape)
    return da, dw


_linear.defvjp(_linear_fwd, _linear_bwd)


def _split3_dot(x, ones_mat):
    hi = x.astype(BF16)
    rest = x - hi.astype(F32)
    mid = rest.astype(BF16)
    lo = (rest - mid.astype(F32)).astype(BF16)
    dot = lambda p: jnp.dot(p, ones_mat, preferred_element_type=F32)
    return dot(hi) + dot(mid) + dot(lo)


def _dot_nt(a, b):
    return lax.dot_general(a, b, (((1,), (1,)), ((), ())), preferred_element_type=F32)


def _dot_tn(a, b):
    return lax.dot_general(a, b, (((0,), (0,)), ((), ())), preferred_element_type=F32)


def _sba_scores(q16, k16, q_block, k_block, tile, scale):
    z = _dot_nt(q16, k16) * scale
    row = lax.broadcasted_iota(jnp.int32, (tile, tile), 0) + q_block * tile
    col = lax.broadcasted_iota(jnp.int32, (tile, tile), 1) + k_block * tile
    mask = col < row
    log_sig = jnp.minimum(z, 0.0) - jnp.log(1.0 + jnp.exp(-jnp.abs(z)))
    log_keep = jnp.where(mask, log_sig - z, 0.0)
    return mask, log_sig, log_keep


def _sba_forward(q, k, v):
    t_len, width = q.shape
    heads = width // HEAD_DIM
    tile = min(ATTN_TILE, t_len)
    scale = HEAD_DIM ** -0.5

    def body(q_ref, k_ref, v_ref, o_ref, r_ref):
        i = pl.program_id(1)
        q16 = q_ref[...].astype(BF16)
        ri = lax.broadcasted_iota(jnp.int32, (tile, tile), 0)
        ci = lax.broadcasted_iota(jnp.int32, (tile, tile), 1)
        later = (ri > ci).astype(BF16)

        def step(jj, carry):
            run, acc = carry
            j = i - jj
            off = pl.multiple_of(j * tile, tile)
            k16 = k_ref[pl.ds(off, tile), :].astype(BF16)
            v16 = v_ref[pl.ds(off, tile), :].astype(BF16)
            mask, log_sig, log_keep = _sba_scores(q16, k16, i, j, tile, scale)
            suffix = _split3_dot(log_keep, later) + run
            att = jnp.where(mask, jnp.exp(log_sig + suffix), 0.0)
            acc = acc + jnp.dot(att.astype(BF16), v16, preferred_element_type=F32)
            run = run + jnp.sum(log_keep, axis=1, keepdims=True)
            return run, acc

        run, acc = lax.fori_loop(0, i + 1, step, (jnp.zeros((tile, 1), F32), jnp.zeros((tile, HEAD_DIM), F32)))
        o_ref[...] = acc
        r_ref[...] = run

    return pl.pallas_call(
        body,
        name="sba_fwd",
        grid=(heads, t_len // tile),
        in_specs=[pl.BlockSpec((tile, HEAD_DIM), lambda h, i: (i, h)),
                  pl.BlockSpec((t_len, HEAD_DIM), lambda h, i: (0, h)),
                  pl.BlockSpec((t_len, HEAD_DIM), lambda h, i: (0, h))],
        out_specs=[pl.BlockSpec((tile, HEAD_DIM), lambda h, i: (i, h)),
                   pl.BlockSpec((None, tile, 1), lambda h, i: (h, i, 0))],
        out_shape=[jax.ShapeDtypeStruct((t_len, width), F32), jax.ShapeDtypeStruct((heads, t_len, 1), F32)],
        compiler_params=_params("parallel", "arbitrary"),
    )(q, k, v)


def _sba_backward(q, k, v, total, do):
    t_len, width = q.shape
    heads = width // HEAD_DIM
    tile = min(ATTN_TILE, t_len)
    scale = HEAD_DIM ** -0.5

    def body(q_ref, k_ref, v_ref, r_ref, do_ref, dq_ref, dk_ref, dv_ref):
        i = pl.program_id(1)

        @pl.when(i == 0)
        def _():
            dk_ref[...] = jnp.zeros_like(dk_ref)
            dv_ref[...] = jnp.zeros_like(dv_ref)

        q16 = q_ref[...].astype(BF16)
        do16 = do_ref[...].astype(BF16)
        tot = r_ref[...]
        ri = lax.broadcasted_iota(jnp.int32, (tile, tile), 0)
        ci = lax.broadcasted_iota(jnp.int32, (tile, tile), 1)
        upto = (ri <= ci).astype(BF16)
        before = (ri < ci).astype(BF16)

        def step(j, carry):
            keep_left, w_left, dq = carry
            off = pl.multiple_of(j * tile, tile)
            k16 = k_ref[pl.ds(off, tile), :].astype(BF16)
            v16 = v_ref[pl.ds(off, tile), :].astype(BF16)
            mask, log_sig, log_keep = _sba_scores(q16, k16, i, j, tile, scale)
            suffix = tot - keep_left - _split3_dot(log_keep, upto)
            att = jnp.where(mask, jnp.exp(log_sig + suffix), 0.0)
            w = att * _dot_nt(do16, v16)
            d_keep = w_left + _split3_dot(w, before)
            sig = jnp.exp(log_sig)
            dz = jnp.where(mask, w * (1.0 - sig) - sig * d_keep, 0.0) * scale
            dz16 = dz.astype(BF16)
            dq = dq + jnp.dot(dz16, k16, preferred_element_type=F32)
            dk_ref[pl.ds(off, tile), :] += _dot_tn(dz16, q16)
            dv_ref[pl.ds(off, tile), :] += _dot_tn(att.astype(BF16), do16)
            keep_left = keep_left + jnp.sum(log_keep, axis=1, keepdims=True)
            w_left = w_left + jnp.sum(w, axis=1, keepdims=True)
            return keep_left, w_left, dq

        zero = jnp.zeros((tile, 1), F32)
        _, _, dq = lax.fori_loop(0, i + 1, step, (zero, zero, jnp.zeros((tile, HEAD_DIM), F32)))
        dq_ref[...] = dq

    q_spec = pl.BlockSpec((tile, HEAD_DIM), lambda h, i: (i, h))
    kv_spec = pl.BlockSpec((t_len, HEAD_DIM), lambda h, i: (0, h))
    return pl.pallas_call(
        body,
        name="sba_bwd",
        grid=(heads, t_len // tile),
        in_specs=[q_spec, kv_spec, kv_spec, pl.BlockSpec((None, tile, 1), lambda h, i: (h, i, 0)), q_spec],
        out_specs=[q_spec, kv_spec, kv_spec],
        out_shape=[jax.ShapeDtypeStruct((t_len, width), F32)] * 3,
        compiler_params=_params("parallel", "arbitrary"),
    )(q, k, v, total, do)


@jax.custom_vjp
def _stick_breaking(q, k, v):
    return _sba_forward(q, k, v)[0]


def _stick_breaking_fwd(q, k, v):
    o, total = _sba_forward(q, k, v)
    return o, (q, k, v, total)


def _stick_breaking_bwd(res, do):
    q, k, v, total = res
    return tuple(_sba_backward(q, k, v, total, do))


_stick_breaking.defvjp(_stick_breaking_fwd, _stick_breaking_bwd)


def _contract(a16, b16, ca, cb):
    return lax.dot_general(a16, b16, (((ca,), (cb,)), ((), ())), preferred_element_type=F32)


def _pdot_raw(a, b, ca, cb):
    a_hi = a.astype(BF16)
    a_lo = (a - a_hi.astype(F32)).astype(BF16)
    b_hi = b.astype(BF16)
    b_lo = (b - b_hi.astype(F32)).astype(BF16)
    return _contract(a_hi, b_hi, ca, cb) + _contract(a_hi, b_lo, ca, cb) + _contract(a_lo, b_hi, ca, cb)


def _bdot_raw(a, b, ca, cb):
    return _contract(a.astype(BF16), b.astype(BF16), ca, cb)


def _make_dot(raw):
    @functools.partial(jax.custom_vjp, nondiff_argnums=(2, 3))
    def dot(a, b, ca, cb):
        return raw(a, b, ca, cb)

    def fwd(a, b, ca, cb):
        return raw(a, b, ca, cb), (a, b)

    def bwd(ca, cb, res, g):
        a, b = res
        da = raw(g, b, 1, 1 - cb) if ca == 1 else raw(b, g, 1 - cb, 1)
        db = raw(a, g, 1 - ca, 0) if cb == 0 else raw(g, a, 0, 1 - ca)
        return da, db

    dot.defvjp(fwd, bwd)
    return dot


_pdot = _make_dot(_pdot_raw)
_bdot = _make_dot(_bdot_raw)


@jax.custom_vjp
def _unit_lower_inverse(low):
    size = low.shape[0]
    eye = (lax.broadcasted_iota(jnp.int32, (size, size), 0) == lax.broadcasted_iota(jnp.int32, (size, size), 1)).astype(F32)
    inv = eye - low
    power = low
    span = 2
    while span < size:
        power = _pdot_raw(power, power, 1, 0)
        inv = inv + _pdot_raw(inv, power, 1, 0)
        span *= 2
    return inv


def _unit_lower_inverse_fwd(low):
    inv = _unit_lower_inverse(low)
    return inv, inv


def _unit_lower_inverse_bwd(inv, g):
    return (-_pdot_raw(_pdot_raw(inv, g, 0, 0), inv, 1, 1),)


_unit_lower_inverse.defvjp(_unit_lower_inverse_fwd, _unit_lower_inverse_bwd)


def _gdn_chunk(state, q, k, v, g_row, b_row):
    size = q.shape[0]
    ri = lax.broadcasted_iota(jnp.int32, (size, size), 0)
    ci = lax.broadcasted_iota(jnp.int32, (size, size), 1)
    eye, incl, strict = ri == ci, ci <= ri, ci < ri
    g_rowb = jnp.broadcast_to(g_row, (size, size))
    g_col = jnp.sum(jnp.where(eye, g_rowb, 0.0), axis=1, keepdims=True)
    b_col = jnp.sum(jnp.where(eye, jnp.broadcast_to(b_row, (size, size)), 0.0), axis=1, keepdims=True)
    gc_col = jnp.sum(jnp.where(incl, g_rowb, 0.0), axis=1, keepdims=True)
    gc_row = jnp.sum(jnp.where(ri <= ci, jnp.broadcast_to(g_col, (size, size)), 0.0), axis=0, keepdims=True)
    g_last = jnp.sum(g_row, axis=1, keepdims=True)
    decay = jnp.where(incl, jnp.exp(jnp.where(incl, gc_col - gc_row, 0.0)), 0.0)
    kb = k * b_col
    low = jnp.where(strict, _bdot(kb, k, 1, 1) * decay, 0.0)
    inv = _unit_lower_inverse(low)
    grow = jnp.exp(gc_col)
    u = _pdot(inv, v * b_col, 1, 0)
    w = _pdot(inv, kb * grow, 1, 0)
    intra = _bdot(q, k, 1, 1) * decay
    v_new = u - _bdot(w, state, 1, 0)
    o = _bdot(q * grow, state, 1, 0) + _bdot(intra, v_new, 1, 0)
    k_dec = k * jnp.exp(g_last - gc_col)
    new_state = state * jnp.exp(g_last) + _bdot(k_dec, v_new, 0, 0)
    return new_state, o


def _gdn_specs(t_len, n_chunks):
    seq = pl.BlockSpec((t_len, HEAD_DIM), lambda h: (0, h))
    gate = pl.BlockSpec((None, n_chunks, CHUNK), lambda h: (h, 0, 0))
    states = pl.BlockSpec((None, n_chunks, HEAD_DIM, HEAD_DIM), lambda h: (h, 0, 0, 0))
    return seq, gate, states


def _gdn_forward(q, k, v, g, beta):
    t_len, width = q.shape
    heads = width // HEAD_DIM
    n_chunks = t_len // CHUNK
    seq, gate, states = _gdn_specs(t_len, n_chunks)

    def body(q_ref, k_ref, v_ref, g_ref, b_ref, o_ref, s_ref):
        def step(n, state):
            rows = pl.ds(pl.multiple_of(n * CHUNK, CHUNK), CHUNK)
            s_ref[n] = state
            state, o = _gdn_chunk(state, q_ref[rows, :], k_ref[rows, :], v_ref[rows, :],
                                  g_ref[pl.ds(n, 1), :], b_ref[pl.ds(n, 1), :])
            o_ref[rows, :] = o
            return state

        lax.fori_loop(0, n_chunks, step, jnp.zeros((HEAD_DIM, HEAD_DIM), F32))

    return pl.pallas_call(
        body,
        name="gdn_fwd",
        grid=(heads,),
        in_specs=[seq, seq, seq, gate, gate],
        out_specs=[seq, states],
        out_shape=[jax.ShapeDtypeStruct((t_len, width), F32),
                   jax.ShapeDtypeStruct((heads, n_chunks, HEAD_DIM, HEAD_DIM), F32)],
        compiler_params=_params("parallel"),
    )(q, k, v, g, beta)


def _gdn_backward(q, k, v, g, beta, starts, do):
    t_len, width = q.shape
    heads = width // HEAD_DIM
    n_chunks = t_len // CHUNK
    seq, gate, states = _gdn_specs(t_len, n_chunks)

    def body(q_ref, k_ref, v_ref, g_ref, b_ref, s_ref, do_ref, dq_ref, dk_ref, dv_ref, dg_ref, db_ref):
        def step(m, d_state):
            n = n_chunks - 1 - m
            rows = pl.ds(pl.multiple_of(n * CHUNK, CHUNK), CHUNK)
            _, pull = jax.vjp(_gdn_chunk, s_ref[n], q_ref[rows, :], k_ref[rows, :], v_ref[rows, :],
                              g_ref[pl.ds(n, 1), :], b_ref[pl.ds(n, 1), :])
            d_state, dq, dk, dv, dg, db = pull((d_state, do_ref[rows, :]))
            dq_ref[rows, :] = dq
            dk_ref[rows, :] = dk
            dv_ref[rows, :] = dv
            dg_ref[pl.ds(n, 1), :] = dg
            db_ref[pl.ds(n, 1), :] = db
            return d_state

        lax.fori_loop(0, n_chunks, step, jnp.zeros((HEAD_DIM, HEAD_DIM), F32))

    return pl.pallas_call(
        body,
        name="gdn_bwd",
        grid=(heads,),
        in_specs=[seq, seq, seq, gate, gate, states, seq],
        out_specs=[seq, seq, seq, gate, gate],
        out_shape=[jax.ShapeDtypeStruct((t_len, width), F32)] * 3
        + [jax.ShapeDtypeStruct((heads, n_chunks, CHUNK), F32)] * 2,
        compiler_params=_params("parallel"),
    )(q, k, v, g, beta, starts, do)


@jax.custom_vjp
def _gated_delta_rule(q, k, v, g, beta):
    return _gdn_forward(q, k, v, g, beta)[0]


def _gated_delta_rule_fwd(q, k, v, g, beta):
    o, starts = _gdn_forward(q, k, v, g, beta)
    return o, (q, k, v, g, beta, starts)


def _gated_delta_rule_bwd(res, do):
    return tuple(_gdn_backward(*res, do))


_gated_delta_rule.defvjp(_gated_delta_rule_fwd, _gated_delta_rule_bwd)


def _rms_norm(x, gain):
    return x * lax.rsqrt(jnp.mean(x * x, axis=-1, keepdims=True) + EPS) * gain


def _layer_norm(x, gain):
    xc = x - jnp.mean(x, axis=-1, keepdims=True)
    return xc * lax.rsqrt(jnp.mean(xc * xc, axis=-1, keepdims=True) + EPS) * gain


def _l2_norm(x):
    return x * lax.rsqrt(jnp.sum(x * x, axis=-1, keepdims=True) + EPS)


def _causal_conv(x, w):
    t_len = x.shape[0]
    xp = jnp.pad(x, ((CONV_WIDTH - 1, 0), (0, 0)))
    return sum(xp[i:i + t_len] * w[i] for i in range(CONV_WIDTH))


def _packed_sections(d_model):
    heads = d_model // 256
    mixer = heads * HEAD_DIM
    gmlp = d_model // 2
    widths = (("gdn_qkv", 3 * mixer), ("gdn_gate", mixer), ("gmlp_uv", 2 * gmlp), ("sba_qkv", 3 * mixer),
              ("gates", N_BRANCHES * d_model), ("gdn_ab", 2 * heads))
    sections, off = {}, 0
    for name, width in widths:
        sections[name] = (off, width)
        off += width
    return sections, _round_up(off, 512)


def _pack_w_in(w, d_model):
    sections, total = _packed_sections(d_model)
    heads = d_model // 256
    mixer = heads * HEAD_DIM
    ref_off = {"gdn_qkv": 0, "gdn_ab": 3 * mixer, "gdn_gate": 3 * mixer + 2 * heads}
    ref_off["gmlp_uv"] = ref_off["gdn_gate"] + mixer
    ref_off["sba_qkv"] = ref_off["gmlp_uv"] + d_model
    ref_off["gates"] = ref_off["sba_qkv"] + 3 * mixer
    parts = [w[..., ref_off[name]:ref_off[name] + width] for name, (_, width) in sections.items()]
    used = sum(width for _, width in sections.values())
    parts.append(jnp.zeros(w.shape[:-1] + (total - used,), w.dtype))
    return jnp.concatenate(parts, axis=-1)


def _unpack_w_in(wp, d_model):
    sections, _ = _packed_sections(d_model)
    order = ("gdn_qkv", "gdn_ab", "gdn_gate", "gmlp_uv", "sba_qkv", "gates")
    return jnp.concatenate([wp[..., sections[n][0]:sections[n][0] + sections[n][1]] for n in order], axis=-1)


def _mixer(h, p, layer):
    t_len, d_model = h.shape
    heads = d_model // 256
    mixer = heads * HEAD_DIM
    gmlp = d_model // 2
    sections, _ = _packed_sections(d_model)
    z = _linear(h, p["w_in"][layer], "w_in")
    cut = lambda name: z[:, sections[name][0]:sections[name][0] + sections[name][1]]
    per_head = lambda t: t.reshape(t_len, heads, HEAD_DIM)

    qkv = jax.nn.silu(_causal_conv(cut("gdn_qkv"), p["conv_w"][layer]))
    qa = _l2_norm(per_head(qkv[:, :mixer])) * HEAD_DIM ** -0.5
    ka = _l2_norm(per_head(qkv[:, mixer:2 * mixer]))
    va = qkv[:, 2 * mixer:]
    ab = cut("gdn_ab")
    to_chunks = lambda t: jnp.transpose(t).reshape(heads, t_len // CHUNK, CHUNK)
    beta = to_chunks(jax.nn.sigmoid(ab[:, heads:]))
    g = to_chunks(-jnp.exp(p["a_log"][layer]) * jax.nn.softplus(ab[:, :heads] + p["dt_bias"][layer]))
    oa = _gated_delta_rule(qa.reshape(t_len, mixer), ka.reshape(t_len, mixer), va, g, beta)
    oa = _rms_norm(per_head(oa), p["gdn_norm_g"][layer]) * jax.nn.silu(per_head(cut("gdn_gate")))
    branch_a = _linear(oa.reshape(t_len, mixer), p["w_out_a"][layer], "w_out_branch", "col")

    uv = jax.nn.gelu(cut("gmlp_uv"), approximate=False)
    u, vb = uv[:, :gmlp], uv[:, gmlp:]
    group_dim = gmlp // GMLP_GROUPS
    vb = _layer_norm(vb, p["gmlp_ln_g"][layer]).reshape(t_len // GMLP_BLOCK, GMLP_BLOCK, GMLP_GROUPS, group_dim)
    pos = jnp.arange(GMLP_BLOCK) // CHUNK
    ws = jnp.where((pos[None, :] <= pos[:, None])[None], p["w_spatial"][layer], 0.0)
    s = jnp.einsum("gts,nsgc->ntgc", ws.astype(BF16), vb.astype(BF16), preferred_element_type=F32)
    s = s + jnp.transpose(p["b_spatial"][layer])[None, :, :, None]
    branch_b = _linear(u * s.reshape(t_len, gmlp), p["w_out_b"][layer], "w_out_branch", "col")

    sba = cut("sba_qkv")
    qc = _rms_norm(per_head(sba[:, :mixer]), p["sba_q_g"][layer]).reshape(t_len, mixer)
    kc = _rms_norm(per_head(sba[:, mixer:2 * mixer]), p["sba_k_g"][layer]).reshape(t_len, mixer)
    oc = _stick_breaking(qc, kc, sba[:, 2 * mixer:])
    branch_c = _linear(oc, p["w_out_c"][layer], "w_out_branch", "col")

    gates = jax.nn.sigmoid(cut("gates"))
    y = (gates[:, :d_model] * branch_a + gates[:, d_model:2 * d_model] * branch_b
         + gates[:, 2 * d_model:] * branch_c)
    return _linear(y, p["w_out"][layer], "w_out", "row")


def _local_loss(x, p, target):
    depth = len(p["w_in"])
    for layer in range(depth):
        x = x + _mixer(_rms_norm(x, p["norm_mix_g"][layer]), p, layer)
        f = _linear(_rms_norm(x, p["norm_mlp_g"][layer]), p["w_ff1"][layer], "w_ff1", "col")
        x = x + _linear(jnp.square(jax.nn.relu(f)), p["w_ff2"][layer], "w_ff2", "row")
    err = jnp.square(x - target)
    return 0.5 * jnp.sum(jnp.mean(err, axis=-1))


ANY = pl.BlockSpec(memory_space=pl.ANY)


def _place():
    x, y, c = lax.axis_index("x"), lax.axis_index("y"), lax.axis_index("c")
    other_chips = [(1 - x, y), (x, 1 - y), (1 - x, 1 - y)]
    return x, y, c, other_chips


def _my_chip():
    return 2 * lax.axis_index("x") + lax.axis_index("y")


def _cast_to_slot(w, layer):
    _, rows, cols = w.shape
    tr = _row_tile(rows, cols, 16)

    def body(w_ref, o_ref):
        o_ref[...] = w_ref[...].astype(BF16)

    return pl.pallas_call(
        body,
        name="cast_to_slot",
        grid=(rows // tr,),
        in_specs=[pl.BlockSpec((None, tr, cols), lambda r: (layer, r, 0))],
        out_specs=pl.BlockSpec((None, tr, cols), lambda r: (_my_chip(), r, 0)),
        out_shape=jax.ShapeDtypeStruct((N_CHIPS, rows, cols), BF16),
        compiler_params=_params("parallel"),
    )(w)


def _remote(src, dst, sems, a, k, to):
    send_sems, recv_sems = sems
    return pltpu.make_async_remote_copy(src_ref=src, dst_ref=dst, send_sem=send_sems.at[a, k],
                                        recv_sem=recv_sems.at[a, k], device_id=to, device_id_type=MESH)


def _sem_pairs(n_arrays, n_copies):
    return [pltpu.SemaphoreType.DMA((n_arrays, n_copies)), pltpu.SemaphoreType.DMA((n_arrays, n_copies))]


def _gather_chips(bufs):
    n = len(bufs)

    def body(*refs):
        outs, sems = refs[n:2 * n], refs[2 * n:]
        x, y, c, other_chips = _place()
        sibling = (x, y, 1 - c)

        def half(a, core):
            rows = bufs[a].shape[1] // 2
            return pl.ds(core * rows, rows)

        def copy(a, k, chip, core, to):
            ref = outs[a].at[2 * chip[0] + chip[1], half(a, core), :]
            return _remote(ref, ref, sems, a, k, to)

        sent = [copy(a, j, (x, y), c, (*chip, c)) for a in range(n) for j, chip in enumerate(other_chips)]
        for cp in sent:
            cp.start()
        passed = []
        for a in range(n):
            for j, chip in enumerate(other_chips):
                copy(a, j, chip, c, sibling).wait_recv()
                passed.append(copy(a, 3 + j, chip, c, sibling))
                passed[-1].start()
        for a in range(n):
            for j, chip in enumerate(other_chips):
                copy(a, 3 + j, chip, 1 - c, sibling).wait_recv()
        for cp in sent + passed:
            cp.wait_send()

    return pl.pallas_call(
        body,
        name="gather_chips",
        in_specs=[ANY] * n,
        out_specs=[ANY] * n,
        out_shape=[jax.ShapeDtypeStruct(b.shape, b.dtype) for b in bufs],
        input_output_aliases={a: a for a in range(n)},
        scratch_shapes=_sem_pairs(n, 6),
    )(*bufs)


def _pair_exchange(grads):
    n = len(grads)

    def body(*refs):
        ins, outs, sems = refs[:n], refs[n:2 * n], refs[2 * n:]
        x, y, c, _ = _place()
        copies = []
        for a in range(n):
            rows = grads[a].shape[1] // 2
            copies.append(_remote(ins[a].at[:, pl.ds((1 - c) * rows, rows), :], outs[a], sems, a, 0, (x, y, 1 - c)))
        for cp in copies:
            cp.start()
        for cp in copies:
            cp.wait()

    return pl.pallas_call(
        body,
        name="pair_exchange",
        in_specs=[ANY] * n,
        out_specs=[ANY] * n,
        out_shape=[jax.ShapeDtypeStruct((g.shape[0], g.shape[1] // 2, g.shape[2]), g.dtype) for g in grads],
        scratch_shapes=_sem_pairs(n, 1),
    )(*grads)


def _chip_exchange(parts):
    n = len(parts)

    def body(*refs):
        ins, outs, sems = refs[:n], refs[n:2 * n], refs[2 * n:]
        x, y, c, other_chips = _place()
        me = 2 * x + y
        copies = [_remote(ins[a].at[2 * chip[0] + chip[1]], outs[a].at[me], sems, a, j, (*chip, c))
                  for a in range(n) for j, chip in enumerate(other_chips)]
        for cp in copies:
            cp.start()
        for cp in copies:
            cp.wait()

    return pl.pallas_call(
        body,
        name="chip_exchange",
        in_specs=[ANY] * n,
        out_specs=[ANY] * n,
        out_shape=[jax.ShapeDtypeStruct(p.shape, p.dtype) for p in parts],
        scratch_shapes=_sem_pairs(n, 3),
    )(*parts)


def _pair_share(halves):
    n = len(halves)

    def body(*refs):
        outs, sems = refs[n:2 * n], refs[2 * n:]
        x, y, c, _ = _place()
        copies = []
        for a in range(n):
            rows = halves[a].shape[0] // 2
            mine = outs[a].at[pl.ds(c * rows, rows), :]
            copies.append(_remote(mine, mine, sems, a, 0, (x, y, 1 - c)))
        for cp in copies:
            cp.start()
        for cp in copies:
            cp.wait()

    return pl.pallas_call(
        body,
        name="pair_share",
        in_specs=[ANY] * n,
        out_specs=[ANY] * n,
        out_shape=[jax.ShapeDtypeStruct(h.shape, h.dtype) for h in halves],
        input_output_aliases={a: a for a in range(n)},
        scratch_shapes=_sem_pairs(n, 1),
    )(*halves)


def _all_gather_rows(block):
    m_per, n = block.shape

    def body(x_ref, out_ref, send_sems, recv_sems, local_sem):
        x, y, c, other_chips = _place()
        me, sibling = (x, y, c), (x, y, 1 - c)

        def rows(px, py, pc):
            return out_ref.at[pl.ds((4 * px + 2 * py + pc) * m_per, m_per), :]

        def copy(k, blk, to, src=None):
            return pltpu.make_async_remote_copy(src_ref=rows(*blk) if src is None else src, dst_ref=rows(*blk),
                                                send_sem=send_sems.at[k], recv_sem=recv_sems.at[k],
                                                device_id=to, device_id_type=MESH)

        mine = pltpu.make_async_copy(x_ref, rows(*me), local_sem)
        mine.start()
        first = [copy(0, me, sibling, src=x_ref)]
        first += [copy(1 + j, me, (*chip, c), src=x_ref) for j, chip in enumerate(other_chips)]
        for cp in first:
            cp.start()
        passed = [copy(4 + j, (*chip, c), sibling) for j, chip in enumerate(other_chips)]
        for j, chip in enumerate(other_chips):
            copy(1 + j, (*chip, c), me).wait_recv()
            passed[j].start()
        copy(0, sibling, me).wait_recv()
        for j, chip in enumerate(other_chips):
            copy(4 + j, (*chip, 1 - c), me).wait_recv()
        for cp in first + passed:
            cp.wait_send()
        mine.wait()

    return pl.pallas_call(
        body,
        name="all_gather_rows",
        in_specs=[ANY],
        out_specs=ANY,
        out_shape=jax.ShapeDtypeStruct((N_DEV * m_per, n), block.dtype),
        scratch_shapes=[pltpu.SemaphoreType.DMA((7,)), pltpu.SemaphoreType.DMA((7,)), pltpu.SemaphoreType.DMA],
    )(block)


def _row_tile(rows, cols, multiple):
    budget = max(multiple, ELEMENTWISE_BLOCK_ELEMS // _round_up(cols, LANES))
    t = multiple
    while t * 2 <= budget and rows % (t * 2) == 0:
        t *= 2
    return t if rows % t == 0 else rows


def _pair_sum(grad, theirs):
    chips, rows, cols = grad.shape
    half = rows // 2
    tr = _row_tile(half, cols, 16)
    blocks = half // tr

    def body(a_ref, b_ref, o_ref):
        o_ref[...] = (a_ref[...].astype(F32) + b_ref[...].astype(F32)).astype(o_ref.dtype)

    return pl.pallas_call(
        body,
        name="pair_sum",
        grid=(chips, blocks),
        in_specs=[pl.BlockSpec((None, tr, cols), lambda s, r: (s, lax.axis_index("c") * blocks + r, 0)),
                  pl.BlockSpec((None, tr, cols), lambda s, r: (s, r, 0))],
        out_specs=pl.BlockSpec((None, tr, cols), lambda s, r: (s, r, 0)),
        out_shape=jax.ShapeDtypeStruct((chips, half, cols), grad.dtype),
        compiler_params=_params("parallel", "parallel"),
    )(grad, theirs)


def _chip_sum(mine, others):
    chips, half, cols = mine.shape
    tr = _row_tile(half, cols, 16)
    blocks = half // tr

    def body(own_ref, *refs):
        o_ref = refs[-1]
        total = own_ref[...].astype(F32)
        for ref in refs[:-1]:
            total = total + ref[...].astype(F32)
        o_ref[...] = total

    slot = lambda q: pl.BlockSpec((None, tr, cols), lambda r: ((_my_chip() + q) % chips, r, 0))
    return pl.pallas_call(
        body,
        name="chip_sum",
        grid=(blocks,),
        in_specs=[slot(q) for q in range(chips)],
        out_specs=pl.BlockSpec((tr, cols), lambda r: (lax.axis_index("c") * blocks + r, 0)),
        out_shape=jax.ShapeDtypeStruct((2 * half, cols), F32),
        compiler_params=_params("parallel"),
    )(mine, *([others] * (chips - 1)))


def _device_sum(blocks, m_per):
    tr = _row_tile(m_per, LANES, 8)
    per = m_per // tr

    def body(*refs):
        o_ref = refs[-1]
        total = refs[0][...]
        for ref in refs[1:-1]:
            total = total + ref[...]
        o_ref[...] = total

    return pl.pallas_call(
        body,
        name="device_sum",
        grid=(per,),
        in_specs=[pl.BlockSpec((tr, LANES), functools.partial(lambda d, r: (d * per + r, 0), d)) for d in range(N_DEV)],
        out_specs=pl.BlockSpec((tr, LANES), lambda r: (r, 0)),
        out_shape=jax.ShapeDtypeStruct((m_per, LANES), F32),
        compiler_params=_params("parallel"),
    )(*([blocks] * N_DEV))


def _adamw(grads, w, m, v):
    depth, rows, cols = w.shape
    tr = _row_tile(rows, cols, 8) if rows % 8 == 0 else rows
    spec = pl.BlockSpec((None, tr, cols), lambda l, r: (l, r, 0))
    grad_spec = lambda q: pl.BlockSpec((tr, cols), lambda l, r: (jnp.where(l == q, r, 0), 0))

    def body(*refs):
        g_refs, (w_ref, m_ref, v_ref, go_ref, d_ref, mo_ref, vo_ref) = refs[:depth], refs[depth:]
        layer = pl.program_id(0)
        g = g_refs[0][...]
        for q in range(1, depth):
            g = jnp.where(layer == q, g_refs[q][...], g)
        m_new = ADAM_B1 * m_ref[...] + (1.0 - ADAM_B1) * g
        v_new = ADAM_B2 * v_ref[...] + (1.0 - ADAM_B2) * jnp.square(g)
        m_hat = m_new / (1.0 - ADAM_B1 ** ADAM_STEP)
        v_hat = v_new / (1.0 - ADAM_B2 ** ADAM_STEP)
        go_ref[...] = g
        d_ref[...] = -ADAM_LR * (m_hat / (jnp.sqrt(v_hat) + ADAM_EPS) + ADAM_WD * w_ref[...])
        mo_ref[...] = m_new
        vo_ref[...] = v_new

    return pl.pallas_call(
        body,
        name="adamw",
        grid=(depth, rows // tr),
        in_specs=[grad_spec(q) for q in range(depth)] + [spec] * 3,
        out_specs=[spec] * 4,
        out_shape=[jax.ShapeDtypeStruct(w.shape, F32)] * 4,
        compiler_params=_params("parallel", "parallel"),
    )(*grads, w, m, v)


def _join_cols(gathered):
    return jnp.concatenate([gathered[s] for s in range(N_CHIPS)], axis=-1)


def _split_cols(full):
    rows, cols = full.shape
    return jnp.transpose(full.reshape(rows, N_CHIPS, cols // N_CHIPS), (1, 0, 2))


def _reduce_scatter(grads):
    theirs = _pair_exchange(grads)
    pairs = [_pair_sum(g, t) for g, t in zip(grads, theirs)]
    others = _chip_exchange(pairs)
    return _pair_share([_chip_sum(p, o) for p, o in zip(pairs, others)])


def _pack_rows(arrays):
    parts, counts = [], []
    for a in arrays:
        flat = a.reshape(-1).astype(F32)
        n_rows = _round_up(flat.shape[0], 8 * LANES) // LANES
        parts.append(jnp.pad(flat, (0, n_rows * LANES - flat.shape[0])).reshape(n_rows, LANES))
        counts.append(n_rows)
    return jnp.concatenate(parts, axis=0), counts


def _pad_rows(a, rows):
    return jnp.pad(a, ((0, rows - a.shape[0]), (0, 0)))


def _unpack_rows(packed, counts, shapes):
    out, row = [], 0
    for n_rows, shape in zip(counts, shapes):
        size = 1
        for d in shape:
            size *= d
        out.append(packed[row:row + n_rows].reshape(-1)[:size].reshape(shape))
        row += n_rows
    return out


SHARDED = (("w_in", "col"), ("w_out_a", "col"), ("w_out_b", "col"), ("w_out_c", "col"), ("w_out", "row"),
           ("w_ff1", "col"), ("w_ff2", "row"))


def kernel(x, w_in, conv_w, a_log, dt_bias, gdn_norm_g, gmlp_ln_g, w_spatial, b_spatial, sba_q_g, sba_k_g, w_out_a, w_out_b, w_out_c, w_out, norm_mix_g, norm_mlp_g, w_ff1, w_ff2, loss_target, m_w_in, m_conv_w, m_a_log, m_dt_bias, m_gdn_norm_g, m_gmlp_ln_g, m_w_spatial, m_b_spatial, m_sba_q_g, m_sba_k_g, m_w_out_a, m_w_out_b, m_w_out_c, m_w_out, m_norm_mix_g, m_norm_mlp_g, m_w_ff1, m_w_ff2, v_w_in, v_conv_w, v_a_log, v_dt_bias, v_gdn_norm_g, v_gmlp_ln_g, v_w_spatial, v_b_spatial, v_sba_q_g, v_sba_k_g, v_w_out_a, v_w_out_b, v_w_out_c, v_w_out, v_norm_mix_g, v_norm_mlp_g, v_w_ff1, v_w_ff2):
    given = dict(locals())
    weights = {n: given[n] for n in WEIGHT_NAMES}
    depth, d_model = w_in.shape[0], w_in.shape[1]
    layers = range(depth)
    chip = _my_chip()

    gathered = [_gather_chips([_cast_to_slot(weights[name], l) for name, _ in SHARDED]) for l in layers]
    conv_rows = depth * CONV_WIDTH
    conv_cols = conv_w.shape[-1]
    conv_all = _all_gather_rows(_pad_rows(conv_w.reshape(conv_rows, conv_cols), _round_up(conv_rows, 8)))
    conv_all = conv_all.reshape(N_CHIPS, 2, _round_up(conv_rows, 8), conv_cols)[:, 0, :conv_rows]
    conv_full = jnp.concatenate([conv_all[s] for s in range(N_CHIPS)], axis=-1).reshape(depth, CONV_WIDTH, N_CHIPS * conv_cols)

    params = {n: weights[n] for n in REPLICATED_NAMES}
    params["conv_w"] = conv_full
    for i, (name, _) in enumerate(SHARDED):
        params[name] = [gathered[l][i] for l in layers]
    params["w_in"] = [_pack_w_in(_join_cols(gathered[l][0]), d_model) for l in layers]

    loss_local, (grad_x, grads) = jax.value_and_grad(_local_loss, argnums=(0, 1))(x[0], params, loss_target[0])

    summed = []
    for l in layers:
        local = [grads[name][l] for name, _ in SHARDED]
        local[0] = _split_cols(_unpack_w_in(local[0], d_model))
        summed.append(_reduce_scatter(local))
    rep_packed, rep_counts = _pack_rows([grads[n] for n in REPLICATED_NAMES])
    rep_rows = _round_up(rep_packed.shape[0], SMALL_ROW_ALIGN)
    rest_packed, rest_counts = _pack_rows([grads["conv_w"], loss_local])
    m_per = rep_rows + _round_up(rest_packed.shape[0], SMALL_ROW_ALIGN)
    packed = jnp.concatenate([_pad_rows(rep_packed, rep_rows), _pad_rows(rest_packed, m_per - rep_rows)], axis=0)
    total = _device_sum(_all_gather_rows(packed), m_per)
    conv_sum, loss = _unpack_rows(total[rep_rows:], rest_counts, [grads["conv_w"].shape, ()])
    conv_grad = lax.dynamic_slice_in_dim(conv_sum, chip * conv_cols, conv_cols, axis=2)

    out = {}
    for i, (name, _) in enumerate(SHARDED):
        out[name] = _adamw([summed[l][i] for l in layers], weights[name], given["m_" + name], given["v_" + name])
    out["conv_w"] = _adamw([conv_grad[l] for l in layers], conv_w, m_conv_w, v_conv_w)
    rep_shapes = [weights[n].shape for n in REPLICATED_NAMES]
    pack3 = lambda prefix: _pad_rows(_pack_rows([given[prefix + n] for n in REPLICATED_NAMES])[0], rep_rows)[None]
    rep = _adamw([total[:rep_rows]], pack3(""), pack3("m_"), pack3("v_"))
    rep = [_unpack_rows(t[0], rep_counts, rep_shapes) for t in rep]
    for i, name in enumerate(REPLICATED_NAMES):
        out[name] = tuple(t[i] for t in rep)

    results = [loss, grad_x[None]]
    for kind in range(4):
        results += [out[n][kind] for n in WEIGHT_NAMES]
    return tuple(results)
```

```python
import functools

import jax
import jax.numpy as jnp
from jax import lax
from jax.experimental import pallas as pl
from jax.experimental.pallas import tpu as pltpu

F32, BF16 = jnp.float32, jnp.bfloat16
MESH = pl.DeviceIdType.MESH
N_CHIPS = 4
N_DEV = 8

EPS = 1e-6
CHUNK = 64
HEAD_DIM = 128
CONV_WIDTH = 4
GMLP_GROUPS = 8
GMLP_BLOCK = 128
N_BRANCHES = 3
ADAM_LR, ADAM_B1, ADAM_B2, ADAM_EPS, ADAM_WD, ADAM_STEP = 0.001, 0.9, 0.999, 1e-08, 0.01, 10

V7X_VMEM_BYTES = 64 * 1024 * 1024
VMEM_LIMIT = V7X_VMEM_BYTES - 8 * 1024 * 1024
LANES = 128
MATMUL_TILE = 1024
MATMUL_K_TILE = 2048
ATTN_TILE = 256
GDN_HEADS_PER_STEP = 4
GDN_SEGMENT = 512
ELEMENTWISE_BLOCK_ELEMS = 512 * 1024
SMALL_ROW_ALIGN = 256

WEIGHT_NAMES = ("w_in", "conv_w", "a_log", "dt_bias", "gdn_norm_g", "gmlp_ln_g", "w_spatial", "b_spatial",
                "sba_q_g", "sba_k_g", "w_out_a", "w_out_b", "w_out_c", "w_out", "norm_mix_g", "norm_mlp_g",
                "w_ff1", "w_ff2")
REPLICATED_NAMES = ("a_log", "dt_bias", "gdn_norm_g", "gmlp_ln_g", "w_spatial", "b_spatial", "sba_q_g",
                    "sba_k_g", "norm_mix_g", "norm_mlp_g")


def _round_up(n, m):
    return (n + m - 1) // m * m


def _tile(dim, pref):
    if dim <= pref:
        return dim
    t = pref // LANES * LANES
    while t >= LANES:
        if dim % t == 0:
            return t
        t -= LANES
    return dim


def _params(*semantics):
    return pltpu.CompilerParams(dimension_semantics=semantics, vmem_limit_bytes=VMEM_LIMIT)


_DOT_DIMS = {"nn": ((1,), (0,)), "nt": ((1,), (1,)), "tn": ((0,), (0,))}


def _logical_shape(shape, kind):
    if kind is None:
        return shape
    chips, rows, cols = shape
    return (rows, chips * cols) if kind == "col" else (chips * rows, cols)


def _weight_tiles(shape, kind, row_pref, col_pref):
    rows, cols = _logical_shape(shape, kind)
    tr = _tile(shape[1] if kind == "row" else rows, row_pref)
    tc = _tile(shape[2] if kind == "col" else cols, col_pref)
    return tr, tc


def _weight_spec(shape, kind, tr, tc, pick):
    if kind is None:
        return pl.BlockSpec((tr, tc), pick)
    if kind == "col":
        per = shape[2] // tc
        return pl.BlockSpec((None, tr, tc), lambda i, j, k: (pick(i, j, k)[1] // per, pick(i, j, k)[0], pick(i, j, k)[1] % per))
    per = shape[1] // tr
    return pl.BlockSpec((None, tr, tc), lambda i, j, k: (pick(i, j, k)[0] // per, pick(i, j, k)[0] % per, pick(i, j, k)[1]))


def _matmul(a, b, mode, out_dtype, name, kind=None, out_shape=None):
    if mode == "tn":
        k_dim, m_dim = a.shape
        n_dim = b.shape[1]
        out_shape = (m_dim, n_dim) if kind is None else out_shape
        tm, tn = _weight_tiles(out_shape, kind, MATMUL_TILE, MATMUL_TILE)
        tk = _tile(k_dim, MATMUL_K_TILE)
        a_spec = pl.BlockSpec((tk, tm), lambda i, j, k: (k, i))
        b_spec = pl.BlockSpec((tk, tn), lambda i, j, k: (k, j))
        out_spec = _weight_spec(out_shape, kind, tm, tn, lambda i, j, k: (i, j))
    else:
        m_dim, k_dim = a.shape
        tm = _tile(m_dim, MATMUL_TILE)
        if mode == "nn":
            _, n_dim = _logical_shape(b.shape, kind)
            tk, tn = _weight_tiles(b.shape, kind, MATMUL_K_TILE, MATMUL_TILE)
            b_spec = _weight_spec(b.shape, kind, tk, tn, lambda i, j, k: (k, j))
        else:
            n_dim, _ = _logical_shape(b.shape, kind)
            tn, tk = _weight_tiles(b.shape, kind, MATMUL_TILE, MATMUL_K_TILE)
            b_spec = _weight_spec(b.shape, kind, tn, tk, lambda i, j, k: (j, k))
        a_spec = pl.BlockSpec((tm, tk), lambda i, j, k: (i, k))
        out_shape = (m_dim, n_dim)
        out_spec = pl.BlockSpec((tm, tn), lambda i, j, k: (i, j))
    nk = k_dim // tk
    dims = (_DOT_DIMS[mode], ((), ()))

    def body(a_ref, b_ref, o_ref, *scratch):
        part = lax.dot_general(a_ref[...], b_ref[...], dims, preferred_element_type=F32)
        if nk == 1:
            o_ref[...] = part.astype(out_dtype)
            return
        acc_ref, = scratch
        k = pl.program_id(2)

        @pl.when(k == 0)
        def _():
            acc_ref[...] = part

        @pl.when(k > 0)
        def _():
            acc_ref[...] += part

        @pl.when(k == nk - 1)
        def _():
            o_ref[...] = acc_ref[...].astype(out_dtype)

    return pl.pallas_call(
        body,
        name=name,
        grid=(m_dim // tm, n_dim // tn, nk),
        in_specs=[a_spec, b_spec],
        out_specs=out_spec,
        out_shape=jax.ShapeDtypeStruct(out_shape, out_dtype),
        scratch_shapes=[] if nk == 1 else [pltpu.VMEM((tm, tn), F32)],
        compiler_params=_params("parallel", "parallel", "arbitrary"),
    )(a, b)


@functools.partial(jax.custom_vjp, nondiff_argnums=(2, 3))
def _linear(a, w, name, kind=None):
    return _matmul(a.astype(BF16), w, "nn", F32, name + "_fwd", kind)


def _linear_fwd(a, w, name, kind):
    a16 = a.astype(BF16)
    return _matmul(a16, w, "nn", F32, name + "_fwd", kind), (a16, w)


def _linear_bwd(name, kind, res, g):
    a16, w = res
    g16 = g.astype(BF16)
    da = _matmul(g16, w, "nt", F32, name + "_dgrad", kind)
    dw = _matmul(a16, g16, "tn", BF16, name + "_wgrad", kind, w.shape)
    return da, dw


_linear.defvjp(_linear_fwd, _linear_bwd)


def _split3_dot(x, ones_mat):
    hi = x.astype(BF16)
    rest = x - hi.astype(F32)
    mid = rest.astype(BF16)
    lo = (rest - mid.astype(F32)).astype(BF16)
    dot = lambda p: jnp.dot(p, ones_mat, preferred_element_type=F32)
    return dot(hi) + dot(mid) + dot(lo)


def _dot_nt(a, b):
    return lax.dot_general(a, b, (((1,), (1,)), ((), ())), preferred_element_type=F32)


def _dot_tn(a, b):
    return lax.dot_general(a, b, (((0,), (0,)), ((), ())), preferred_element_type=F32)


def _sba_scores(q16, k16, tile, scale, diagonal):
    z = _dot_nt(q16, k16) * scale
    log_sig = jnp.minimum(z, 0.0) - jnp.log(1.0 + jnp.exp(-jnp.abs(z)))
    if not diagonal:
        return None, log_sig, log_sig - z
    mask = lax.broadcasted_iota(jnp.int32, (tile, tile), 1) < lax.broadcasted_iota(jnp.int32, (tile, tile), 0)
    return mask, log_sig, jnp.where(mask, log_sig - z, 0.0)


def _masked(mask, value):
    return value if mask is None else jnp.where(mask, value, 0.0)


def _sba_forward(q, k, v):
    t_len, width = q.shape
    heads = width // HEAD_DIM
    tile = min(ATTN_TILE, t_len)
    scale = HEAD_DIM ** -0.5

    def body(q_ref, k_ref, v_ref, o_ref, r_ref):
        i = pl.program_id(1)
        q16 = q_ref[...].astype(BF16)
        ri = lax.broadcasted_iota(jnp.int32, (tile, tile), 0)
        ci = lax.broadcasted_iota(jnp.int32, (tile, tile), 1)
        later = (ri > ci).astype(BF16)

        def tile_step(j, carry, diagonal):
            run, acc = carry
            off = pl.multiple_of(j * tile, tile)
            k16 = k_ref[pl.ds(off, tile), :].astype(BF16)
            v16 = v_ref[pl.ds(off, tile), :].astype(BF16)
            mask, log_sig, log_keep = _sba_scores(q16, k16, tile, scale, diagonal)
            suffix = _split3_dot(log_keep, later) + run
            att = _masked(mask, jnp.exp(log_sig + suffix))
            acc = acc + jnp.dot(att.astype(BF16), v16, preferred_element_type=F32)
            run = run + jnp.sum(log_keep, axis=1, keepdims=True)
            return run, acc

        carry = tile_step(i, (jnp.zeros((tile, 1), F32), jnp.zeros((tile, HEAD_DIM), F32)), True)
        run, acc = lax.fori_loop(0, i, lambda jj, carry: tile_step(i - 1 - jj, carry, False), carry)
        o_ref[...] = acc
        r_ref[...] = run

    return pl.pallas_call(
        body,
        name="sba_fwd",
        grid=(heads, t_len // tile),
        in_specs=[pl.BlockSpec((tile, HEAD_DIM), lambda h, i: (i, h)),
                  pl.BlockSpec((t_len, HEAD_DIM), lambda h, i: (0, h)),
                  pl.BlockSpec((t_len, HEAD_DIM), lambda h, i: (0, h))],
        out_specs=[pl.BlockSpec((tile, HEAD_DIM), lambda h, i: (i, h)),
                   pl.BlockSpec((None, tile, 1), lambda h, i: (h, i, 0))],
        out_shape=[jax.ShapeDtypeStruct((t_len, width), F32), jax.ShapeDtypeStruct((heads, t_len, 1), F32)],
        compiler_params=_params("parallel", "arbitrary"),
    )(q, k, v)


def _sba_backward(q, k, v, total, do):
    t_len, width = q.shape
    heads = width // HEAD_DIM
    tile = min(ATTN_TILE, t_len)
    scale = HEAD_DIM ** -0.5

    def body(q_ref, k_ref, v_ref, r_ref, do_ref, dq_ref, dk_ref, dv_ref):
        i = pl.program_id(1)

        @pl.when(i == 0)
        def _():
            dk_ref[...] = jnp.zeros_like(dk_ref)
            dv_ref[...] = jnp.zeros_like(dv_ref)

        q16 = q_ref[...].astype(BF16)
        do16 = do_ref[...].astype(BF16)
        tot = r_ref[...]
        ri = lax.broadcasted_iota(jnp.int32, (tile, tile), 0)
        ci = lax.broadcasted_iota(jnp.int32, (tile, tile), 1)
        upto = (ri <= ci).astype(BF16)
        before = (ri < ci).astype(BF16)

        def tile_step(j, carry, diagonal):
            keep_left, w_left, dq = carry
            off = pl.multiple_of(j * tile, tile)
            k16 = k_ref[pl.ds(off, tile), :].astype(BF16)
            v16 = v_ref[pl.ds(off, tile), :].astype(BF16)
            mask, log_sig, log_keep = _sba_scores(q16, k16, tile, scale, diagonal)
            suffix = tot - keep_left - _split3_dot(log_keep, upto)
            att = _masked(mask, jnp.exp(log_sig + suffix))
            w = att * _dot_nt(do16, v16)
            d_keep = w_left + _split3_dot(w, before)
            sig = jnp.exp(log_sig)
            dz = _masked(mask, w * (1.0 - sig) - sig * d_keep) * scale
            dz16 = dz.astype(BF16)
            dq = dq + jnp.dot(dz16, k16, preferred_element_type=F32)
            dk_ref[pl.ds(off, tile), :] += _dot_tn(dz16, q16)
            dv_ref[pl.ds(off, tile), :] += _dot_tn(att.astype(BF16), do16)
            keep_left = keep_left + jnp.sum(log_keep, axis=1, keepdims=True)
            w_left = w_left + jnp.sum(w, axis=1, keepdims=True)
            return keep_left, w_left, dq

        zero = jnp.zeros((tile, 1), F32)
        carry = lax.fori_loop(0, i, lambda j, carry: tile_step(j, carry, False), (zero, zero, jnp.zeros((tile, HEAD_DIM), F32)))
        _, _, dq = tile_step(i, carry, True)
        dq_ref[...] = dq

    q_spec = pl.BlockSpec((tile, HEAD_DIM), lambda h, i: (i, h))
    kv_spec = pl.BlockSpec((t_len, HEAD_DIM), lambda h, i: (0, h))
    return pl.pallas_call(
        body,
        name="sba_bwd",
        grid=(heads, t_len // tile),
        in_specs=[q_spec, kv_spec, kv_spec, pl.BlockSpec((None, tile, 1), lambda h, i: (h, i, 0)), q_spec],
        out_specs=[q_spec, kv_spec, kv_spec],
        out_shape=[jax.ShapeDtypeStruct((t_len, width), F32)] * 3,
        compiler_params=_params("parallel", "arbitrary"),
    )(q, k, v, total, do)


@jax.custom_vjp
def _stick_breaking(q, k, v):
    return _sba_forward(q, k, v)[0]


def _stick_breaking_fwd(q, k, v):
    o, total = _sba_forward(q, k, v)
    return o, (q, k, v, total)


def _stick_breaking_bwd(res, do):
    q, k, v, total = res
    return tuple(_sba_backward(q, k, v, total, do))


_stick_breaking.defvjp(_stick_breaking_fwd, _stick_breaking_bwd)


def _contract(a16, b16, ca, cb):
    return lax.dot_general(a16, b16, (((ca,), (cb,)), ((), ())), preferred_element_type=F32)


def _pdot_raw(a, b, ca, cb):
    a_hi = a.astype(BF16)
    a_lo = (a - a_hi.astype(F32)).astype(BF16)
    b_hi = b.astype(BF16)
    b_lo = (b - b_hi.astype(F32)).astype(BF16)
    return _contract(a_hi, b_hi, ca, cb) + _contract(a_hi, b_lo, ca, cb) + _contract(a_lo, b_hi, ca, cb)


def _bdot_raw(a, b, ca, cb):
    return _contract(a.astype(BF16), b.astype(BF16), ca, cb)


def _make_dot(raw):
    @functools.partial(jax.custom_vjp, nondiff_argnums=(2, 3))
    def dot(a, b, ca, cb):
        return raw(a, b, ca, cb)

    def fwd(a, b, ca, cb):
        return raw(a, b, ca, cb), (a, b)

    def bwd(ca, cb, res, g):
        a, b = res
        da = raw(g, b, 1, 1 - cb) if ca == 1 else raw(b, g, 1 - cb, 1)
        db = raw(a, g, 1 - ca, 0) if cb == 0 else raw(g, a, 0, 1 - ca)
        return da, db

    dot.defvjp(fwd, bwd)
    return dot


_pdot = _make_dot(_pdot_raw)
_bdot = _make_dot(_bdot_raw)


@jax.custom_vjp
def _unit_lower_inverse(low):
    size = low.shape[0]
    eye = (lax.broadcasted_iota(jnp.int32, (size, size), 0) == lax.broadcasted_iota(jnp.int32, (size, size), 1)).astype(F32)
    inv = eye - low
    power = low
    span = 2
    while span < size:
        power = _pdot_raw(power, power, 1, 0)
        inv = inv + _pdot_raw(inv, power, 1, 0)
        span *= 2
    return inv


def _unit_lower_inverse_fwd(low):
    inv = _unit_lower_inverse(low)
    return inv, inv


def _unit_lower_inverse_bwd(inv, g):
    return (-_pdot_raw(_pdot_raw(inv, g, 0, 0), inv, 1, 1),)


_unit_lower_inverse.defvjp(_unit_lower_inverse_fwd, _unit_lower_inverse_bwd)


def _gdn_chunk(state, q, k, v, g_row, b_row):
    size = q.shape[0]
    ri = lax.broadcasted_iota(jnp.int32, (size, size), 0)
    ci = lax.broadcasted_iota(jnp.int32, (size, size), 1)
    eye, incl, strict = ri == ci, ci <= ri, ci < ri
    g_rowb = jnp.broadcast_to(g_row, (size, size))
    g_col = jnp.sum(jnp.where(eye, g_rowb, 0.0), axis=1, keepdims=True)
    b_col = jnp.sum(jnp.where(eye, jnp.broadcast_to(b_row, (size, size)), 0.0), axis=1, keepdims=True)
    gc_col = jnp.sum(jnp.where(incl, g_rowb, 0.0), axis=1, keepdims=True)
    gc_row = jnp.sum(jnp.where(ri <= ci, jnp.broadcast_to(g_col, (size, size)), 0.0), axis=0, keepdims=True)
    g_last = jnp.sum(g_row, axis=1, keepdims=True)
    decay = jnp.where(incl, jnp.exp(jnp.where(incl, gc_col - gc_row, 0.0)), 0.0)
    kb = k * b_col
    low = jnp.where(strict, _bdot(kb, k, 1, 1) * decay, 0.0)
    inv = _unit_lower_inverse(low)
    grow = jnp.exp(gc_col)
    u = _pdot(inv, v * b_col, 1, 0)
    w = _pdot(inv, kb * grow, 1, 0)
    intra = _bdot(q, k, 1, 1) * decay
    v_new = u - _bdot(w, state, 1, 0)
    o = _bdot(q * grow, state, 1, 0) + _bdot(intra, v_new, 1, 0)
    k_dec = k * jnp.exp(g_last - gc_col)
    new_state = state * jnp.exp(g_last) + _bdot(k_dec, v_new, 0, 0)
    return new_state, o


def _gdn_layout(q):
    t_len, width = q.shape
    heads = width // HEAD_DIM
    group = min(GDN_HEADS_PER_STEP, heads)
    seg = min(GDN_SEGMENT, t_len)
    return heads, group, seg, (heads // group, t_len // seg)


def _gdn_specs(group, seg, order):
    chunks = seg // CHUNK
    seq = pl.BlockSpec((seg, group * HEAD_DIM), lambda h, t: (order(t), h))
    gate = pl.BlockSpec((group, chunks, CHUNK), lambda h, t: (h, order(t), 0))
    states = pl.BlockSpec((group, chunks, HEAD_DIM, HEAD_DIM), lambda h, t: (h, order(t), 0, 0))
    return seq, gate, states


def _gdn_forward(q, k, v, g, beta):
    t_len, width = q.shape
    heads, group, seg, grid = _gdn_layout(q)
    seq, gate, states = _gdn_specs(group, seg, lambda t: t)

    def body(q_ref, k_ref, v_ref, g_ref, b_ref, o_ref, s_ref, state_ref):
        @pl.when(pl.program_id(1) == 0)
        def _():
            state_ref[...] = jnp.zeros_like(state_ref)

        def step(n, carry):
            rows = pl.ds(pl.multiple_of(n * CHUNK, CHUNK), CHUNK)
            for j in range(group):
                cols = slice(j * HEAD_DIM, (j + 1) * HEAD_DIM)
                state = state_ref[j]
                s_ref[j, n] = state
                state, o = _gdn_chunk(state, q_ref[rows, cols], k_ref[rows, cols], v_ref[rows, cols],
                                      g_ref[j, pl.ds(n, 1), :], b_ref[j, pl.ds(n, 1), :])
                o_ref[rows, cols] = o
                state_ref[j] = state
            return carry

        lax.fori_loop(0, seg // CHUNK, step, 0)

    return pl.pallas_call(
        body,
        name="gdn_fwd",
        grid=grid,
        in_specs=[seq, seq, seq, gate, gate],
        out_specs=[seq, states],
        out_shape=[jax.ShapeDtypeStruct((t_len, width), F32),
                   jax.ShapeDtypeStruct((heads, t_len // CHUNK, HEAD_DIM, HEAD_DIM), F32)],
        scratch_shapes=[pltpu.VMEM((group, HEAD_DIM, HEAD_DIM), F32)],
        compiler_params=_params("parallel", "arbitrary"),
    )(q, k, v, g, beta)


def _gdn_backward(q, k, v, g, beta, starts, do):
    t_len, width = q.shape
    heads, group, seg, grid = _gdn_layout(q)
    last = grid[1] - 1
    seq, gate, states = _gdn_specs(group, seg, lambda t: last - t)
    chunks = seg // CHUNK

    def body(q_ref, k_ref, v_ref, g_ref, b_ref, s_ref, do_ref, dq_ref, dk_ref, dv_ref, dg_ref, db_ref, d_state_ref):
        @pl.when(pl.program_id(1) == 0)
        def _():
            d_state_ref[...] = jnp.zeros_like(d_state_ref)

        def step(m, carry):
            n = chunks - 1 - m
            rows = pl.ds(pl.multiple_of(n * CHUNK, CHUNK), CHUNK)
            for j in range(group):
                cols = slice(j * HEAD_DIM, (j + 1) * HEAD_DIM)
                _, pull = jax.vjp(_gdn_chunk, s_ref[j, n], q_ref[rows, cols], k_ref[rows, cols], v_ref[rows, cols],
                                  g_ref[j, pl.ds(n, 1), :], b_ref[j, pl.ds(n, 1), :])
                d_state, dq, dk, dv, dg, db = pull((d_state_ref[j], do_ref[rows, cols]))
                dq_ref[rows, cols] = dq
                dk_ref[rows, cols] = dk
                dv_ref[rows, cols] = dv
                dg_ref[j, pl.ds(n, 1), :] = dg
                db_ref[j, pl.ds(n, 1), :] = db
                d_state_ref[j] = d_state
            return carry

        lax.fori_loop(0, chunks, step, 0)

    return pl.pallas_call(
        body,
        name="gdn_bwd",
        grid=grid,
        in_specs=[seq, seq, seq, gate, gate, states, seq],
        out_specs=[seq, seq, seq, gate, gate],
        out_shape=[jax.ShapeDtypeStruct((t_len, width), F32)] * 3
        + [jax.ShapeDtypeStruct((heads, t_len // CHUNK, CHUNK), F32)] * 2,
        scratch_shapes=[pltpu.VMEM((group, HEAD_DIM, HEAD_DIM), F32)],
        compiler_params=_params("parallel", "arbitrary"),
    )(q, k, v, g, beta, starts, do)


@jax.custom_vjp
def _gated_delta_rule(q, k, v, g, beta):
    return _gdn_forward(q, k, v, g, beta)[0]


def _gated_delta_rule_fwd(q, k, v, g, beta):
    o, starts = _gdn_forward(q, k, v, g, beta)
    return o, (q, k, v, g, beta, starts)


def _gated_delta_rule_bwd(res, do):
    return tuple(_gdn_backward(*res, do))


_gated_delta_rule.defvjp(_gated_delta_rule_fwd, _gated_delta_rule_bwd)


def _rms_norm(x, gain):
    return x * lax.rsqrt(jnp.mean(x * x, axis=-1, keepdims=True) + EPS) * gain


def _layer_norm(x, gain):
    xc = x - jnp.mean(x, axis=-1, keepdims=True)
    return xc * lax.rsqrt(jnp.mean(xc * xc, axis=-1, keepdims=True) + EPS) * gain


def _l2_norm(x):
    return x * lax.rsqrt(jnp.sum(x * x, axis=-1, keepdims=True) + EPS)


def _causal_conv(x, w):
    t_len = x.shape[0]
    xp = jnp.pad(x, ((CONV_WIDTH - 1, 0), (0, 0)))
    return sum(xp[i:i + t_len] * w[i] for i in range(CONV_WIDTH))


def _packed_sections(d_model):
    heads = d_model // 256
    mixer = heads * HEAD_DIM
    gmlp = d_model // 2
    widths = (("gdn_qkv", 3 * mixer), ("gdn_gate", mixer), ("gmlp_uv", 2 * gmlp), ("sba_qkv", 3 * mixer),
              ("gates", N_BRANCHES * d_model), ("gdn_ab", 2 * heads))
    sections, off = {}, 0
    for name, width in widths:
        sections[name] = (off, width)
        off += width
    return sections, _round_up(off, MATMUL_K_TILE)


def _pack_w_in(w, d_model):
    sections, total = _packed_sections(d_model)
    heads = d_model // 256
    mixer = heads * HEAD_DIM
    ref_off = {"gdn_qkv": 0, "gdn_ab": 3 * mixer, "gdn_gate": 3 * mixer + 2 * heads}
    ref_off["gmlp_uv"] = ref_off["gdn_gate"] + mixer
    ref_off["sba_qkv"] = ref_off["gmlp_uv"] + d_model
    ref_off["gates"] = ref_off["sba_qkv"] + 3 * mixer
    parts = [w[..., ref_off[name]:ref_off[name] + width] for name, (_, width) in sections.items()]
    used = sum(width for _, width in sections.values())
    parts.append(jnp.zeros(w.shape[:-1] + (total - used,), w.dtype))
    return jnp.concatenate(parts, axis=-1)


def _unpack_w_in(wp, d_model):
    sections, _ = _packed_sections(d_model)
    order = ("gdn_qkv", "gdn_ab", "gdn_gate", "gmlp_uv", "sba_qkv", "gates")
    return jnp.concatenate([wp[..., sections[n][0]:sections[n][0] + sections[n][1]] for n in order], axis=-1)


def _mixer(h, p, layer):
    t_len, d_model = h.shape
    heads = d_model // 256
    mixer = heads * HEAD_DIM
    gmlp = d_model // 2
    sections, _ = _packed_sections(d_model)
    z = _linear(h, p["w_in"][layer], "w_in")
    cut = lambda name: z[:, sections[name][0]:sections[name][0] + sections[name][1]]
    per_head = lambda t: t.reshape(t_len, heads, HEAD_DIM)

    qkv = jax.nn.silu(_causal_conv(cut("gdn_qkv"), p["conv_w"][layer]))
    qa = _l2_norm(per_head(qkv[:, :mixer])) * HEAD_DIM ** -0.5
    ka = _l2_norm(per_head(qkv[:, mixer:2 * mixer]))
    va = qkv[:, 2 * mixer:]
    ab = cut("gdn_ab")
    to_chunks = lambda t: jnp.transpose(t).reshape(heads, t_len // CHUNK, CHUNK)
    beta = to_chunks(jax.nn.sigmoid(ab[:, heads:]))
    g = to_chunks(-jnp.exp(p["a_log"][layer]) * jax.nn.softplus(ab[:, :heads] + p["dt_bias"][layer]))
    oa = _gated_delta_rule(qa.reshape(t_len, mixer), ka.reshape(t_len, mixer), va, g, beta)
    oa = _rms_norm(per_head(oa), p["gdn_norm_g"][layer]) * jax.nn.silu(per_head(cut("gdn_gate")))
    branch_a = _linear(oa.reshape(t_len, mixer), p["w_out_a"][layer], "w_out_branch", "col")

    uv = jax.nn.gelu(cut("gmlp_uv"), approximate=False)
    u, vb = uv[:, :gmlp], uv[:, gmlp:]
    group_dim = gmlp // GMLP_GROUPS
    vb = _layer_norm(vb, p["gmlp_ln_g"][layer]).reshape(t_len // GMLP_BLOCK, GMLP_BLOCK, GMLP_GROUPS, group_dim)
    pos = jnp.arange(GMLP_BLOCK) // CHUNK
    ws = jnp.where((pos[None, :] <= pos[:, None])[None], p["w_spatial"][layer], 0.0)
    s = jnp.einsum("gts,nsgc->ntgc", ws.astype(BF16), vb.astype(BF16), preferred_element_type=F32)
    s = s + jnp.transpose(p["b_spatial"][layer])[None, :, :, None]
    branch_b = _linear(u * s.reshape(t_len, gmlp), p["w_out_b"][layer], "w_out_branch", "col")

    sba = cut("sba_qkv")
    qc = _rms_norm(per_head(sba[:, :mixer]), p["sba_q_g"][layer]).reshape(t_len, mixer)
    kc = _rms_norm(per_head(sba[:, mixer:2 * mixer]), p["sba_k_g"][layer]).reshape(t_len, mixer)
    oc = _stick_breaking(qc, kc, sba[:, 2 * mixer:])
    branch_c = _linear(oc, p["w_out_c"][layer], "w_out_branch", "col")

    gates = jax.nn.sigmoid(cut("gates"))
    y = (gates[:, :d_model] * branch_a + gates[:, d_model:2 * d_model] * branch_b
         + gates[:, 2 * d_model:] * branch_c)
    return _linear(y, p["w_out"][layer], "w_out", "row")


def _local_loss(x, p, target):
    depth = len(p["w_in"])
    for layer in range(depth):
        x = x + _mixer(_rms_norm(x, p["norm_mix_g"][layer]), p, layer)
        f = _linear(_rms_norm(x, p["norm_mlp_g"][layer]), p["w_ff1"][layer], "w_ff1", "col")
        x = x + _linear(jnp.square(jax.nn.relu(f)), p["w_ff2"][layer], "w_ff2", "row")
    err = jnp.square(x - target)
    return 0.5 * jnp.sum(jnp.mean(err, axis=-1))


ANY = pl.BlockSpec(memory_space=pl.ANY)


def _place():
    x, y, c = lax.axis_index("x"), lax.axis_index("y"), lax.axis_index("c")
    other_chips = [(1 - x, y), (x, 1 - y), (1 - x, 1 - y)]
    return x, y, c, other_chips


def _my_chip():
    return 2 * lax.axis_index("x") + lax.axis_index("y")


def _cast_to_slot(w, layer):
    _, rows, cols = w.shape
    tr = _row_tile(rows, cols, 16)

    def body(w_ref, o_ref):
        o_ref[...] = w_ref[...].astype(BF16)

    return pl.pallas_call(
        body,
        name="cast_to_slot",
        grid=(rows // tr,),
        in_specs=[pl.BlockSpec((None, tr, cols), lambda r: (layer, r, 0))],
        out_specs=pl.BlockSpec((None, tr, cols), lambda r: (_my_chip(), r, 0)),
        out_shape=jax.ShapeDtypeStruct((N_CHIPS, rows, cols), BF16),
        compiler_params=_params("parallel"),
    )(w)


def _remote(src, dst, sems, a, k, to):
    send_sems, recv_sems = sems
    return pltpu.make_async_remote_copy(src_ref=src, dst_ref=dst, send_sem=send_sems.at[a, k],
                                        recv_sem=recv_sems.at[a, k], device_id=to, device_id_type=MESH)


def _sem_pairs(n_arrays, n_copies):
    return [pltpu.SemaphoreType.DMA((n_arrays, n_copies)), pltpu.SemaphoreType.DMA((n_arrays, n_copies))]


def _gather_chips(bufs):
    n = len(bufs)

    def body(*refs):
        outs, sems = refs[n:2 * n], refs[2 * n:]
        x, y, c, _ = _place()
        x_nbr, y_nbr, diag, sibling = (1 - x, y), (x, 1 - y), (1 - x, 1 - y), (x, y, 1 - c)

        def piece(a, chip, core, quarter=None):
            half = bufs[a].shape[1] // 2
            rows = pl.ds(core * half, half) if quarter is None else pl.ds(core * half + quarter * (half // 2), half // 2)
            return outs[a].at[2 * chip[0] + chip[1], rows, :]

        def copy(a, k, ref, to):
            return _remote(ref, ref, sems, a, k, to)

        def sends(a):
            return [(0, piece(a, (x, y), c), (*x_nbr, c)),
                    (1, piece(a, (x, y), c), (*y_nbr, c)),
                    (2, piece(a, x_nbr, c, 0), (*y_nbr, c)),
                    (3, piece(a, y_nbr, c, 1), (*x_nbr, c)),
                    (4, piece(a, x_nbr, c), sibling),
                    (5, piece(a, y_nbr, c), sibling),
                    (6, piece(a, diag, c, 0), sibling),
                    (7, piece(a, diag, c, 1), sibling)]

        def lands(a):
            return [piece(a, x_nbr, c), piece(a, y_nbr, c), piece(a, diag, c, 0), piece(a, diag, c, 1),
                    piece(a, x_nbr, 1 - c), piece(a, y_nbr, 1 - c), piece(a, diag, 1 - c, 0), piece(a, diag, 1 - c, 1)]

        started = []

        def start(a, k):
            _, ref, to = sends(a)[k]
            started.append(copy(a, k, ref, to))
            started[-1].start()

        def arrived(a, k):
            copy(a, k, lands(a)[k], sibling).wait_recv()

        for a in range(n):
            start(a, 0)
            start(a, 1)
        for k, onward in ((0, (2, 4)), (1, (3, 5)), (2, (6,)), (3, (7,))):
            for a in range(n):
                arrived(a, k)
                for nxt in onward:
                    start(a, nxt)
        for a in range(n):
            for k in (4, 5, 6, 7):
                arrived(a, k)
        for cp in started:
            cp.wait_send()

    return pl.pallas_call(
        body,
        name="gather_chips",
        in_specs=[ANY] * n,
        out_specs=[ANY] * n,
        out_shape=[jax.ShapeDtypeStruct(b.shape, b.dtype) for b in bufs],
        input_output_aliases={a: a for a in range(n)},
        scratch_shapes=_sem_pairs(n, 8),
    )(*bufs)


def _pair_exchange(grads):
    n = len(grads)

    def body(*refs):
        ins, outs, sems = refs[:n], refs[n:2 * n], refs[2 * n:]
        x, y, c, _ = _place()
        copies = []
        for a in range(n):
            rows = grads[a].shape[1] // 2
            copies.append(_remote(ins[a].at[:, pl.ds((1 - c) * rows, rows), :], outs[a], sems, a, 0, (x, y, 1 - c)))
        for cp in copies:
            cp.start()
        for cp in copies:
            cp.wait()

    return pl.pallas_call(
        body,
        name="pair_exchange",
        in_specs=[ANY] * n,
        out_specs=[ANY] * n,
        out_shape=[jax.ShapeDtypeStruct((g.shape[0], g.shape[1] // 2, g.shape[2]), g.dtype) for g in grads],
        scratch_shapes=_sem_pairs(n, 1),
    )(*grads)


def _chip_exchange(parts):
    n = len(parts)

    def body(*refs):
        ins, outs, sems = refs[:n], refs[n:2 * n], refs[2 * n:]
        x, y, c, other_chips = _place()
        me = 2 * x + y
        copies = [_remote(ins[a].at[2 * chip[0] + chip[1]], outs[a].at[me], sems, a, j, (*chip, c))
                  for a in range(n) for j, chip in enumerate(other_chips)]
        for cp in copies:
            cp.start()
        for cp in copies:
            cp.wait()

    return pl.pallas_call(
        body,
        name="chip_exchange",
        in_specs=[ANY] * n,
        out_specs=[ANY] * n,
        out_shape=[jax.ShapeDtypeStruct(p.shape, p.dtype) for p in parts],
        scratch_shapes=_sem_pairs(n, 3),
    )(*parts)


def _pair_share(halves):
    n = len(halves)

    def body(*refs):
        outs, sems = refs[n:2 * n], refs[2 * n:]
        x, y, c, _ = _place()
        copies = []
        for a in range(n):
            rows = halves[a].shape[0] // 2
            mine = outs[a].at[pl.ds(c * rows, rows), :]
            copies.append(_remote(mine, mine, sems, a, 0, (x, y, 1 - c)))
        for cp in copies:
            cp.start()
        for cp in copies:
            cp.wait()

    return pl.pallas_call(
        body,
        name="pair_share",
        in_specs=[ANY] * n,
        out_specs=[ANY] * n,
        out_shape=[jax.ShapeDtypeStruct(h.shape, h.dtype) for h in halves],
        input_output_aliases={a: a for a in range(n)},
        scratch_shapes=_sem_pairs(n, 1),
    )(*halves)


def _all_gather_rows(block):
    m_per, n = block.shape

    def body(x_ref, out_ref, send_sems, recv_sems, local_sem):
        x, y, c, other_chips = _place()
        me, sibling = (x, y, c), (x, y, 1 - c)

        def rows(px, py, pc):
            return out_ref.at[pl.ds((4 * px + 2 * py + pc) * m_per, m_per), :]

        def copy(k, blk, to, src=None):
            return pltpu.make_async_remote_copy(src_ref=rows(*blk) if src is None else src, dst_ref=rows(*blk),
                                                send_sem=send_sems.at[k], recv_sem=recv_sems.at[k],
                                                device_id=to, device_id_type=MESH)

        mine = pltpu.make_async_copy(x_ref, rows(*me), local_sem)
        mine.start()
        first = [copy(0, me, sibling, src=x_ref)]
        first += [copy(1 + j, me, (*chip, c), src=x_ref) for j, chip in enumerate(other_chips)]
        for cp in first:
            cp.start()
        passed = [copy(4 + j, (*chip, c), sibling) for j, chip in enumerate(other_chips)]
        for j, chip in enumerate(other_chips):
            copy(1 + j, (*chip, c), me).wait_recv()
            passed[j].start()
        copy(0, sibling, me).wait_recv()
        for j, chip in enumerate(other_chips):
            copy(4 + j, (*chip, 1 - c), me).wait_recv()
        for cp in first + passed:
            cp.wait_send()
        mine.wait()

    return pl.pallas_call(
        body,
        name="all_gather_rows",
        in_specs=[ANY],
        out_specs=ANY,
        out_shape=jax.ShapeDtypeStruct((N_DEV * m_per, n), block.dtype),
        scratch_shapes=[pltpu.SemaphoreType.DMA((7,)), pltpu.SemaphoreType.DMA((7,)), pltpu.SemaphoreType.DMA],
    )(block)


def _row_tile(rows, cols, multiple):
    budget = max(multiple, ELEMENTWISE_BLOCK_ELEMS // _round_up(cols, LANES))
    t = multiple
    while t * 2 <= budget and rows % (t * 2) == 0:
        t *= 2
    return t if rows % t == 0 else rows


def _pair_sum(grad, theirs):
    chips, rows, cols = grad.shape
    half = rows // 2
    tr = _row_tile(half, cols, 16)
    blocks = half // tr

    def body(a_ref, b_ref, o_ref):
        o_ref[...] = (a_ref[...].astype(F32) + b_ref[...].astype(F32)).astype(o_ref.dtype)

    return pl.pallas_call(
        body,
        name="pair_sum",
        grid=(chips, blocks),
        in_specs=[pl.BlockSpec((None, tr, cols), lambda s, r: (s, lax.axis_index("c") * blocks + r, 0)),
                  pl.BlockSpec((None, tr, cols), lambda s, r: (s, r, 0))],
        out_specs=pl.BlockSpec((None, tr, cols), lambda s, r: (s, r, 0)),
        out_shape=jax.ShapeDtypeStruct((chips, half, cols), grad.dtype),
        compiler_params=_params("parallel", "parallel"),
    )(grad, theirs)


def _chip_sum(mine, others):
    chips, half, cols = mine.shape
    tr = _row_tile(half, cols, 16)
    blocks = half // tr

    def body(own_ref, *refs):
        o_ref = refs[-1]
        total = own_ref[...].astype(F32)
        for ref in refs[:-1]:
            total = total + ref[...].astype(F32)
        o_ref[...] = total

    slot = lambda q: pl.BlockSpec((None, tr, cols), lambda r: ((_my_chip() + q) % chips, r, 0))
    return pl.pallas_call(
        body,
        name="chip_sum",
        grid=(blocks,),
        in_specs=[slot(q) for q in range(chips)],
        out_specs=pl.BlockSpec((tr, cols), lambda r: (lax.axis_index("c") * blocks + r, 0)),
        out_shape=jax.ShapeDtypeStruct((2 * half, cols), F32),
        compiler_params=_params("parallel"),
    )(mine, *([others] * (chips - 1)))


def _device_sum(blocks, m_per):
    tr = _row_tile(m_per, LANES, 8)
    per = m_per // tr

    def body(*refs):
        o_ref = refs[-1]
        total = refs[0][...]
        for ref in refs[1:-1]:
            total = total + ref[...]
        o_ref[...] = total

    return pl.pallas_call(
        body,
        name="device_sum",
        grid=(per,),
        in_specs=[pl.BlockSpec((tr, LANES), functools.partial(lambda d, r: (d * per + r, 0), d)) for d in range(N_DEV)],
        out_specs=pl.BlockSpec((tr, LANES), lambda r: (r, 0)),
        out_shape=jax.ShapeDtypeStruct((m_per, LANES), F32),
        compiler_params=_params("parallel"),
    )(*([blocks] * N_DEV))


def _adamw(grads, w, m, v):
    depth, rows, cols = w.shape
    tr = _row_tile(rows, cols, 8) if rows % 8 == 0 else rows
    spec = pl.BlockSpec((None, tr, cols), lambda l, r: (l, r, 0))
    grad_spec = lambda q: pl.BlockSpec((tr, cols), lambda l, r: (jnp.where(l == q, r, 0), 0))

    def body(*refs):
        g_refs, (w_ref, m_ref, v_ref, go_ref, d_ref, mo_ref, vo_ref) = refs[:depth], refs[depth:]
        layer = pl.program_id(0)
        g = g_refs[0][...]
        for q in range(1, depth):
            g = jnp.where(layer == q, g_refs[q][...], g)
        m_new = ADAM_B1 * m_ref[...] + (1.0 - ADAM_B1) * g
        v_new = ADAM_B2 * v_ref[...] + (1.0 - ADAM_B2) * jnp.square(g)
        m_hat = m_new / (1.0 - ADAM_B1 ** ADAM_STEP)
        v_hat = v_new / (1.0 - ADAM_B2 ** ADAM_STEP)
        go_ref[...] = g
        d_ref[...] = -ADAM_LR * (m_hat / (jnp.sqrt(v_hat) + ADAM_EPS) + ADAM_WD * w_ref[...])
        mo_ref[...] = m_new
        vo_ref[...] = v_new

    return pl.pallas_call(
        body,
        name="adamw",
        grid=(depth, rows // tr),
        in_specs=[grad_spec(q) for q in range(depth)] + [spec] * 3,
        out_specs=[spec] * 4,
        out_shape=[jax.ShapeDtypeStruct(w.shape, F32)] * 4,
        compiler_params=_params("parallel", "parallel"),
    )(*grads, w, m, v)


def _join_cols(gathered):
    return jnp.concatenate([gathered[s] for s in range(N_CHIPS)], axis=-1)


def _split_cols(full):
    rows, cols = full.shape
    return jnp.transpose(full.reshape(rows, N_CHIPS, cols // N_CHIPS), (1, 0, 2))


def _reduce_scatter(grads):
    theirs = _pair_exchange(grads)
    pairs = [_pair_sum(g, t) for g, t in zip(grads, theirs)]
    others = _chip_exchange(pairs)
    return _pair_share([_chip_sum(p, o) for p, o in zip(pairs, others)])


def _pack_rows(arrays):
    parts, counts = [], []
    for a in arrays:
        flat = a.reshape(-1).astype(F32)
        n_rows = _round_up(flat.shape[0], 8 * LANES) // LANES
        parts.append(jnp.pad(flat, (0, n_rows * LANES - flat.shape[0])).reshape(n_rows, LANES))
        counts.append(n_rows)
    return jnp.concatenate(parts, axis=0), counts


def _pad_rows(a, rows):
    return jnp.pad(a, ((0, rows - a.shape[0]), (0, 0)))


def _unpack_rows(packed, counts, shapes):
    out, row = [], 0
    for n_rows, shape in zip(counts, shapes):
        size = 1
        for d in shape:
            size *= d
        out.append(packed[row:row + n_rows].reshape(-1)[:size].reshape(shape))
        row += n_rows
    return out


SHARDED = (("w_in", "col"), ("w_out_a", "col"), ("w_out_b", "col"), ("w_out_c", "col"), ("w_out", "row"),
           ("w_ff1", "col"), ("w_ff2", "row"))


def kernel(x, w_in, conv_w, a_log, dt_bias, gdn_norm_g, gmlp_ln_g, w_spatial, b_spatial, sba_q_g, sba_k_g, w_out_a, w_out_b, w_out_c, w_out, norm_mix_g, norm_mlp_g, w_ff1, w_ff2, loss_target, m_w_in, m_conv_w, m_a_log, m_dt_bias, m_gdn_norm_g, m_gmlp_ln_g, m_w_spatial, m_b_spatial, m_sba_q_g, m_sba_k_g, m_w_out_a, m_w_out_b, m_w_out_c, m_w_out, m_norm_mix_g, m_norm_mlp_g, m_w_ff1, m_w_ff2, v_w_in, v_conv_w, v_a_log, v_dt_bias, v_gdn_norm_g, v_gmlp_ln_g, v_w_spatial, v_b_spatial, v_sba_q_g, v_sba_k_g, v_w_out_a, v_w_out_b, v_w_out_c, v_w_out, v_norm_mix_g, v_norm_mlp_g, v_w_ff1, v_w_ff2):
    given = dict(locals())
    weights = {n: given[n] for n in WEIGHT_NAMES}
    depth, d_model = w_in.shape[0], w_in.shape[1]
    layers = range(depth)
    chip = _my_chip()

    gathered = [_gather_chips([_cast_to_slot(weights[name], l) for name, _ in SHARDED]) for l in layers]
    conv_rows = depth * CONV_WIDTH
    conv_cols = conv_w.shape[-1]
    conv_all = _all_gather_rows(_pad_rows(conv_w.reshape(conv_rows, conv_cols), _round_up(conv_rows, 8)))
    conv_all = conv_all.reshape(N_CHIPS, 2, _round_up(conv_rows, 8), conv_cols)[:, 0, :conv_rows]
    conv_full = jnp.concatenate([conv_all[s] for s in range(N_CHIPS)], axis=-1).reshape(depth, CONV_WIDTH, N_CHIPS * conv_cols)

    params = {n: weights[n] for n in REPLICATED_NAMES}
    params["conv_w"] = conv_full
    for i, (name, _) in enumerate(SHARDED):
        params[name] = [gathered[l][i] for l in layers]
    params["w_in"] = [_pack_w_in(_join_cols(gathered[l][0]), d_model) for l in layers]

    loss_local, (grad_x, grads) = jax.value_and_grad(_local_loss, argnums=(0, 1))(x[0], params, loss_target[0])

    summed = []
    for l in layers:
        local = [grads[name][l] for name, _ in SHARDED]
        local[0] = _split_cols(_unpack_w_in(local[0], d_model))
        summed.append(_reduce_scatter(local))
    rep_packed, rep_counts = _pack_rows([grads[n] for n in REPLICATED_NAMES])
    rep_rows = _round_up(rep_packed.shape[0], SMALL_ROW_ALIGN)
    rest_packed, rest_counts = _pack_rows([grads["conv_w"], loss_local])
    m_per = rep_rows + _round_up(rest_packed.shape[0], SMALL_ROW_ALIGN)
    packed = jnp.concatenate([_pad_rows(rep_packed, rep_rows), _pad_rows(rest_packed, m_per - rep_rows)], axis=0)
    total = _device_sum(_all_gather_rows(packed), m_per)
    conv_sum, loss = _unpack_rows(total[rep_rows:], rest_counts, [grads["conv_w"].shape, ()])
    conv_grad = lax.dynamic_slice_in_dim(conv_sum, chip * conv_cols, conv_cols, axis=2)

    out = {}
    for i, (name, _) in enumerate(SHARDED):
        out[name] = _adamw([summed[l][i] for l in layers], weights[name], given["m_" + name], given["v_" + name])
    out["conv_w"] = _adamw([conv_grad[l] for l in layers], conv_w, m_conv_w, v_conv_w)
    rep_shapes = [weights[n].shape for n in REPLICATED_NAMES]
    pack3 = lambda prefix: _pad_rows(_pack_rows([given[prefix + n] for n in REPLICATED_NAMES])[0], rep_rows)[None]
    rep = _adamw([total[:rep_rows]], pack3(""), pack3("m_"), pack3("v_"))
    rep = [_unpack_rows(t[0], rep_counts, rep_shapes) for t in rep]
    for i, name in enumerate(REPLICATED_NAMES):
        out[name] = tuple(t[i] for t in rep)

    results = [loss, grad_x[None]]
    for kind in range(4):
        results += [out[n][kind] for n in WEIGHT_NAMES]
    return tuple(results)
```

```python
import functools

import jax
import jax.numpy as jnp
from jax import lax
from jax.experimental import pallas as pl
from jax.experimental.pallas import tpu as pltpu

F32, BF16 = jnp.float32, jnp.bfloat16
MESH = pl.DeviceIdType.MESH
N_CHIPS = 4
N_DEV = 8

EPS = 1e-6
CHUNK = 64
HEAD_DIM = 128
CONV_WIDTH = 4
GMLP_GROUPS = 8
GMLP_BLOCK = 128
N_BRANCHES = 3
ADAM_LR, ADAM_B1, ADAM_B2, ADAM_EPS, ADAM_WD, ADAM_STEP = 0.001, 0.9, 0.999, 1e-08, 0.01, 10

V7X_VMEM_BYTES = 64 * 1024 * 1024
VMEM_LIMIT = V7X_VMEM_BYTES - 8 * 1024 * 1024
LANES = 128
MATMUL_TILE = 1024
MATMUL_K_TILE = 2048
ATTN_TILE = 256
GDN_HEADS_PER_STEP = 4
GDN_SEGMENT = 512
ELEMENTWISE_BLOCK_ELEMS = 512 * 1024
SMALL_ROW_ALIGN = 256

WEIGHT_NAMES = ("w_in", "conv_w", "a_log", "dt_bias", "gdn_norm_g", "gmlp_ln_g", "w_spatial", "b_spatial",
                "sba_q_g", "sba_k_g", "w_out_a", "w_out_b", "w_out_c", "w_out", "norm_mix_g", "norm_mlp_g",
                "w_ff1", "w_ff2")
REPLICATED_NAMES = ("a_log", "dt_bias", "gdn_norm_g", "gmlp_ln_g", "w_spatial", "b_spatial", "sba_q_g",
                    "sba_k_g", "norm_mix_g", "norm_mlp_g")


def _round_up(n, m):
    return (n + m - 1) // m * m


def _tile(dim, pref):
    if dim <= pref:
        return dim
    t = pref // LANES * LANES
    while t >= LANES:
        if dim % t == 0:
            return t
        t -= LANES
    return dim


def _params(*semantics):
    return pltpu.CompilerParams(dimension_semantics=semantics, vmem_limit_bytes=VMEM_LIMIT)


_DOT_DIMS = {"nn": ((1,), (0,)), "nt": ((1,), (1,)), "tn": ((0,), (0,))}


def _logical_shape(shape, kind):
    if kind is None:
        return shape
    chips, rows, cols = shape
    return (rows, chips * cols) if kind == "col" else (chips * rows, cols)


def _weight_tiles(shape, kind, row_pref, col_pref):
    rows, cols = _logical_shape(shape, kind)
    tr = _tile(shape[1] if kind == "row" else rows, row_pref)
    tc = _tile(shape[2] if kind == "col" else cols, col_pref)
    return tr, tc


def _weight_spec(shape, kind, tr, tc, pick):
    if kind is None:
        return pl.BlockSpec((tr, tc), pick)
    if kind == "col":
        per = shape[2] // tc
        return pl.BlockSpec((None, tr, tc), lambda i, j, k: (pick(i, j, k)[1] // per, pick(i, j, k)[0], pick(i, j, k)[1] % per))
    per = shape[1] // tr
    return pl.BlockSpec((None, tr, tc), lambda i, j, k: (pick(i, j, k)[0] // per, pick(i, j, k)[0] % per, pick(i, j, k)[1]))


def _matmul(a, b, mode, out_dtype, name, kind=None, out_shape=None):
    if mode == "tn":
        k_dim, m_dim = a.shape
        n_dim = b.shape[1]
        out_shape = (m_dim, n_dim) if kind is None else out_shape
        tm, tn = _weight_tiles(out_shape, kind, MATMUL_TILE, MATMUL_TILE)
        tk = _tile(k_dim, MATMUL_K_TILE)
        a_spec = pl.BlockSpec((tk, tm), lambda i, j, k: (k, i))
        b_spec = pl.BlockSpec((tk, tn), lambda i, j, k: (k, j))
        out_spec = _weight_spec(out_shape, kind, tm, tn, lambda i, j, k: (i, j))
    else:
        m_dim, k_dim = a.shape
        tm = _tile(m_dim, MATMUL_TILE)
        if mode == "nn":
            _, n_dim = _logical_shape(b.shape, kind)
            tk, tn = _weight_tiles(b.shape, kind, MATMUL_K_TILE, MATMUL_TILE)
            b_spec = _weight_spec(b.shape, kind, tk, tn, lambda i, j, k: (k, j))
        else:
            n_dim, _ = _logical_shape(b.shape, kind)
            tn, tk = _weight_tiles(b.shape, kind, MATMUL_TILE, MATMUL_K_TILE)
            b_spec = _weight_spec(b.shape, kind, tn, tk, lambda i, j, k: (j, k))
        a_spec = pl.BlockSpec((tm, tk), lambda i, j, k: (i, k))
        out_shape = (m_dim, n_dim)
        out_spec = pl.BlockSpec((tm, tn), lambda i, j, k: (i, j))
    nk = k_dim // tk
    dims = (_DOT_DIMS[mode], ((), ()))

    def body(a_ref, b_ref, o_ref, *scratch):
        part = lax.dot_general(a_ref[...], b_ref[...], dims, preferred_element_type=F32)
        if nk == 1:
            o_ref[...] = part.astype(out_dtype)
            return
        acc_ref, = scratch
        k = pl.program_id(2)

        @pl.when(k == 0)
        def _():
            acc_ref[...] = part

        @pl.when(k > 0)
        def _():
            acc_ref[...] += part

        @pl.when(k == nk - 1)
        def _():
            o_ref[...] = acc_ref[...].astype(out_dtype)

    return pl.pallas_call(
        body,
        name=name,
        grid=(m_dim // tm, n_dim // tn, nk),
        in_specs=[a_spec, b_spec],
        out_specs=out_spec,
        out_shape=jax.ShapeDtypeStruct(out_shape, out_dtype),
        scratch_shapes=[] if nk == 1 else [pltpu.VMEM((tm, tn), F32)],
        compiler_params=_params("parallel", "parallel", "arbitrary"),
    )(a, b)


@functools.partial(jax.custom_vjp, nondiff_argnums=(2, 3))
def _linear(a, w, name, kind=None):
    return _matmul(a.astype(BF16), w, "nn", F32, name + "_fwd", kind)


def _linear_fwd(a, w, name, kind):
    a16 = a.astype(BF16)
    return _matmul(a16, w, "nn", F32, name + "_fwd", kind), (a16, w)


def _linear_bwd(name, kind, res, g):
    a16, w = res
    g16 = g.astype(BF16)
    da = _matmul(g16, w, "nt", F32, name + "_dgrad", kind)
    dw = _matmul(a16, g16, "tn", BF16, name + "_wgrad", kind, w.shape)
    return da, dw


_linear.defvjp(_linear_fwd, _linear_bwd)


def _split3_dot(x, ones_mat):
    hi = x.astype(BF16)
    rest = x - hi.astype(F32)
    mid = rest.astype(BF16)
    lo = (rest - mid.astype(F32)).astype(BF16)
    dot = lambda p: jnp.dot(p, ones_mat, preferred_element_type=F32)
    return dot(hi) + dot(mid) + dot(lo)


def _dot_nt(a, b):
    return lax.dot_general(a, b, (((1,), (1,)), ((), ())), preferred_element_type=F32)


def _dot_tn(a, b):
    return lax.dot_general(a, b, (((0,), (0,)), ((), ())), preferred_element_type=F32)


def _sba_scores(q16, k16, tile, scale, diagonal):
    z = _dot_nt(q16, k16) * scale
    log_sig = jnp.minimum(z, 0.0) - jnp.log(1.0 + jnp.exp(-jnp.abs(z)))
    if not diagonal:
        return None, log_sig, log_sig - z
    mask = lax.broadcasted_iota(jnp.int32, (tile, tile), 1) < lax.broadcasted_iota(jnp.int32, (tile, tile), 0)
    return mask, log_sig, jnp.where(mask, log_sig - z, 0.0)


def _masked(mask, value):
    return value if mask is None else jnp.where(mask, value, 0.0)


def _sba_forward(q, k, v):
    t_len, width = q.shape
    heads = width // HEAD_DIM
    tile = min(ATTN_TILE, t_len)
    scale = HEAD_DIM ** -0.5

    def body(q_ref, k_ref, v_ref, o_ref, r_ref):
        i = pl.program_id(1)
        q16 = q_ref[...].astype(BF16)
        ri = lax.broadcasted_iota(jnp.int32, (tile, tile), 0)
        ci = lax.broadcasted_iota(jnp.int32, (tile, tile), 1)
        later = (ri > ci).astype(BF16)

        def tile_step(j, carry, diagonal):
            run, acc = carry
            off = pl.multiple_of(j * tile, tile)
            k16 = k_ref[pl.ds(off, tile), :].astype(BF16)
            v16 = v_ref[pl.ds(off, tile), :].astype(BF16)
            mask, log_sig, log_keep = _sba_scores(q16, k16, tile, scale, diagonal)
            suffix = _split3_dot(log_keep, later) + run
            att = _masked(mask, jnp.exp(log_sig + suffix))
            acc = acc + jnp.dot(att.astype(BF16), v16, preferred_element_type=F32)
            run = run + jnp.sum(log_keep, axis=1, keepdims=True)
            return run, acc

        carry = tile_step(i, (jnp.zeros((tile, 1), F32), jnp.zeros((tile, HEAD_DIM), F32)), True)
        run, acc = lax.fori_loop(0, i, lambda jj, carry: tile_step(i - 1 - jj, carry, False), carry)
        o_ref[...] = acc
        r_ref[...] = run

    return pl.pallas_call(
        body,
        name="sba_fwd",
        grid=(heads, t_len // tile),
        in_specs=[pl.BlockSpec((tile, HEAD_DIM), lambda h, i: (i, h)),
                  pl.BlockSpec((t_len, HEAD_DIM), lambda h, i: (0, h)),
                  pl.BlockSpec((t_len, HEAD_DIM), lambda h, i: (0, h))],
        out_specs=[pl.BlockSpec((tile, HEAD_DIM), lambda h, i: (i, h)),
                   pl.BlockSpec((None, tile, 1), lambda h, i: (h, i, 0))],
        out_shape=[jax.ShapeDtypeStruct((t_len, width), F32), jax.ShapeDtypeStruct((heads, t_len, 1), F32)],
        compiler_params=_params("parallel", "arbitrary"),
    )(q, k, v)


def _sba_backward(q, k, v, total, do):
    t_len, width = q.shape
    heads = width // HEAD_DIM
    tile = min(ATTN_TILE, t_len)
    scale = HEAD_DIM ** -0.5

    def body(q_ref, k_ref, v_ref, r_ref, do_ref, dq_ref, dk_ref, dv_ref):
        i = pl.program_id(1)

        @pl.when(i == 0)
        def _():
            dk_ref[...] = jnp.zeros_like(dk_ref)
            dv_ref[...] = jnp.zeros_like(dv_ref)

        q16 = q_ref[...].astype(BF16)
        do16 = do_ref[...].astype(BF16)
        tot = r_ref[...]
        ri = lax.broadcasted_iota(jnp.int32, (tile, tile), 0)
        ci = lax.broadcasted_iota(jnp.int32, (tile, tile), 1)
        upto = (ri <= ci).astype(BF16)
        before = (ri < ci).astype(BF16)

        def tile_step(j, carry, diagonal):
            keep_left, w_left, dq = carry
            off = pl.multiple_of(j * tile, tile)
            k16 = k_ref[pl.ds(off, tile), :].astype(BF16)
            v16 = v_ref[pl.ds(off, tile), :].astype(BF16)
            mask, log_sig, log_keep = _sba_scores(q16, k16, tile, scale, diagonal)
            suffix = tot - keep_left - _split3_dot(log_keep, upto)
            att = _masked(mask, jnp.exp(log_sig + suffix))
            w = att * _dot_nt(do16, v16)
            d_keep = w_left + _split3_dot(w, before)
            sig = jnp.exp(log_sig)
            dz = _masked(mask, w * (1.0 - sig) - sig * d_keep) * scale
            dz16 = dz.astype(BF16)
            dq = dq + jnp.dot(dz16, k16, preferred_element_type=F32)
            dk_ref[pl.ds(off, tile), :] += _dot_tn(dz16, q16)
            dv_ref[pl.ds(off, tile), :] += _dot_tn(att.astype(BF16), do16)
            keep_left = keep_left + jnp.sum(log_keep, axis=1, keepdims=True)
            w_left = w_left + jnp.sum(w, axis=1, keepdims=True)
            return keep_left, w_left, dq

        zero = jnp.zeros((tile, 1), F32)
        carry = lax.fori_loop(0, i, lambda j, carry: tile_step(j, carry, False), (zero, zero, jnp.zeros((tile, HEAD_DIM), F32)))
        _, _, dq = tile_step(i, carry, True)
        dq_ref[...] = dq

    q_spec = pl.BlockSpec((tile, HEAD_DIM), lambda h, i: (i, h))
    kv_spec = pl.BlockSpec((t_len, HEAD_DIM), lambda h, i: (0, h))
    return pl.pallas_call(
        body,
        name="sba_bwd",
        grid=(heads, t_len // tile),
        in_specs=[q_spec, kv_spec, kv_spec, pl.BlockSpec((None, tile, 1), lambda h, i: (h, i, 0)), q_spec],
        out_specs=[q_spec, kv_spec, kv_spec],
        out_shape=[jax.ShapeDtypeStruct((t_len, width), F32)] * 3,
        compiler_params=_params("parallel", "arbitrary"),
    )(q, k, v, total, do)


@jax.custom_vjp
def _stick_breaking(q, k, v):
    return _sba_forward(q, k, v)[0]


def _stick_breaking_fwd(q, k, v):
    o, total = _sba_forward(q, k, v)
    return o, (q, k, v, total)


def _stick_breaking_bwd(res, do):
    q, k, v, total = res
    return tuple(_sba_backward(q, k, v, total, do))


_stick_breaking.defvjp(_stick_breaking_fwd, _stick_breaking_bwd)


def _contract(a16, b16, ca, cb):
    return lax.dot_general(a16, b16, (((ca,), (cb,)), ((), ())), preferred_element_type=F32)


def _pdot_raw(a, b, ca, cb):
    a_hi = a.astype(BF16)
    a_lo = (a - a_hi.astype(F32)).astype(BF16)
    b_hi = b.astype(BF16)
    b_lo = (b - b_hi.astype(F32)).astype(BF16)
    return _contract(a_hi, b_hi, ca, cb) + _contract(a_hi, b_lo, ca, cb) + _contract(a_lo, b_hi, ca, cb)


def _bdot_raw(a, b, ca, cb):
    return _contract(a.astype(BF16), b.astype(BF16), ca, cb)


def _make_dot(raw):
    @functools.partial(jax.custom_vjp, nondiff_argnums=(2, 3))
    def dot(a, b, ca, cb):
        return raw(a, b, ca, cb)

    def fwd(a, b, ca, cb):
        return raw(a, b, ca, cb), (a, b)

    def bwd(ca, cb, res, g):
        a, b = res
        da = raw(g, b, 1, 1 - cb) if ca == 1 else raw(b, g, 1 - cb, 1)
        db = raw(a, g, 1 - ca, 0) if cb == 0 else raw(g, a, 0, 1 - ca)
        return da, db

    dot.defvjp(fwd, bwd)
    return dot


_bdot = _make_dot(_bdot_raw)


@jax.custom_vjp
def _unit_lower_inverse(low):
    size = low.shape[0]
    eye = (lax.broadcasted_iota(jnp.int32, (size, size), 0) == lax.broadcasted_iota(jnp.int32, (size, size), 1)).astype(F32)
    inv = eye - low
    power = low
    span = 2
    while span < size:
        power = _bdot_raw(power, power, 1, 0)
        inv = inv + _bdot_raw(inv, power, 1, 0)
        span *= 2
    resid = eye - inv - _pdot_raw(low, inv, 1, 0)
    return inv + _bdot_raw(inv, resid, 1, 0)


def _unit_lower_inverse_fwd(low):
    inv = _unit_lower_inverse(low)
    return inv, inv


def _unit_lower_inverse_bwd(inv, g):
    return (-_bdot_raw(_bdot_raw(inv, g, 0, 0), inv, 1, 1),)


_unit_lower_inverse.defvjp(_unit_lower_inverse_fwd, _unit_lower_inverse_bwd)


def _gdn_chunk(state, q, k, v, g_row, b_row):
    size = q.shape[0]
    ri = lax.broadcasted_iota(jnp.int32, (size, size), 0)
    ci = lax.broadcasted_iota(jnp.int32, (size, size), 1)
    eye, incl, strict = ri == ci, ci <= ri, ci < ri
    g_rowb = jnp.broadcast_to(g_row, (size, size))
    g_col = jnp.sum(jnp.where(eye, g_rowb, 0.0), axis=1, keepdims=True)
    b_col = jnp.sum(jnp.where(eye, jnp.broadcast_to(b_row, (size, size)), 0.0), axis=1, keepdims=True)
    gc_col = jnp.sum(jnp.where(incl, g_rowb, 0.0), axis=1, keepdims=True)
    gc_row = jnp.sum(jnp.where(ri <= ci, jnp.broadcast_to(g_col, (size, size)), 0.0), axis=0, keepdims=True)
    g_last = jnp.sum(g_row, axis=1, keepdims=True)
    decay = jnp.where(incl, jnp.exp(jnp.where(incl, gc_col - gc_row, 0.0)), 0.0)
    kb = k * b_col
    low = jnp.where(strict, _bdot(kb, k, 1, 1) * decay, 0.0)
    inv = _unit_lower_inverse(low)
    grow = jnp.exp(gc_col)
    u = _bdot(inv, v * b_col, 1, 0)
    w = _bdot(inv, kb * grow, 1, 0)
    intra = _bdot(q, k, 1, 1) * decay
    v_new = u - _bdot(w, state, 1, 0)
    o = _bdot(q * grow, state, 1, 0) + _bdot(intra, v_new, 1, 0)
    k_dec = k * jnp.exp(g_last - gc_col)
    new_state = state * jnp.exp(g_last) + _bdot(k_dec, v_new, 0, 0)
    return new_state, o


def _gdn_layout(q):
    t_len, width = q.shape
    heads = width // HEAD_DIM
    group = min(GDN_HEADS_PER_STEP, heads)
    seg = min(GDN_SEGMENT, t_len)
    return heads, group, seg, (heads // group, t_len // seg)


def _gdn_specs(group, seg, order):
    chunks = seg // CHUNK
    seq = pl.BlockSpec((seg, group * HEAD_DIM), lambda h, t: (order(t), h))
    gate = pl.BlockSpec((group, chunks, CHUNK), lambda h, t: (h, order(t), 0))
    states = pl.BlockSpec((group, chunks, HEAD_DIM, HEAD_DIM), lambda h, t: (h, order(t), 0, 0))
    return seq, gate, states


def _gdn_forward(q, k, v, g, beta):
    t_len, width = q.shape
    heads, group, seg, grid = _gdn_layout(q)
    seq, gate, states = _gdn_specs(group, seg, lambda t: t)

    def body(q_ref, k_ref, v_ref, g_ref, b_ref, o_ref, s_ref, state_ref):
        @pl.when(pl.program_id(1) == 0)
        def _():
            state_ref[...] = jnp.zeros_like(state_ref)

        def step(n, carry):
            rows = pl.ds(pl.multiple_of(n * CHUNK, CHUNK), CHUNK)
            for j in range(group):
                cols = slice(j * HEAD_DIM, (j + 1) * HEAD_DIM)
                state = state_ref[j]
                s_ref[j, n] = state
                state, o = _gdn_chunk(state, q_ref[rows, cols], k_ref[rows, cols], v_ref[rows, cols],
                                      g_ref[j, pl.ds(n, 1), :], b_ref[j, pl.ds(n, 1), :])
                o_ref[rows, cols] = o
                state_ref[j] = state
            return carry

        lax.fori_loop(0, seg // CHUNK, step, 0)

    return pl.pallas_call(
        body,
        name="gdn_fwd",
        grid=grid,
        in_specs=[seq, seq, seq, gate, gate],
        out_specs=[seq, states],
        out_shape=[jax.ShapeDtypeStruct((t_len, width), F32),
                   jax.ShapeDtypeStruct((heads, t_len // CHUNK, HEAD_DIM, HEAD_DIM), F32)],
        scratch_shapes=[pltpu.VMEM((group, HEAD_DIM, HEAD_DIM), F32)],
        compiler_params=_params("parallel", "arbitrary"),
    )(q, k, v, g, beta)


def _gdn_backward(q, k, v, g, beta, starts, do):
    t_len, width = q.shape
    heads, group, seg, grid = _gdn_layout(q)
    last = grid[1] - 1
    seq, gate, states = _gdn_specs(group, seg, lambda t: last - t)
    chunks = seg // CHUNK

    def body(q_ref, k_ref, v_ref, g_ref, b_ref, s_ref, do_ref, dq_ref, dk_ref, dv_ref, dg_ref, db_ref, d_state_ref):
        @pl.when(pl.program_id(1) == 0)
        def _():
            d_state_ref[...] = jnp.zeros_like(d_state_ref)

        def step(m, carry):
            n = chunks - 1 - m
            rows = pl.ds(pl.multiple_of(n * CHUNK, CHUNK), CHUNK)
            for j in range(group):
                cols = slice(j * HEAD_DIM, (j + 1) * HEAD_DIM)
                _, pull = jax.vjp(_gdn_chunk, s_ref[j, n], q_ref[rows, cols], k_ref[rows, cols], v_ref[rows, cols],
                                  g_ref[j, pl.ds(n, 1), :], b_ref[j, pl.ds(n, 1), :])
                d_state, dq, dk, dv, dg, db = pull((d_state_ref[j], do_ref[rows, cols]))
                dq_ref[rows, cols] = dq
                dk_ref[rows, cols] = dk
                dv_ref[rows, cols] = dv
                dg_ref[j, pl.ds(n, 1), :] = dg
                db_ref[j, pl.ds(n, 1), :] = db
                d_state_ref[j] = d_state
            return carry

        lax.fori_loop(0, chunks, step, 0)

    return pl.pallas_call(
        body,
        name="gdn_bwd",
        grid=grid,
        in_specs=[seq, seq, seq, gate, gate, states, seq],
        out_specs=[seq, seq, seq, gate, gate],
        out_shape=[jax.ShapeDtypeStruct((t_len, width), F32)] * 3
        + [jax.ShapeDtypeStruct((heads, t_len // CHUNK, CHUNK), F32)] * 2,
        scratch_shapes=[pltpu.VMEM((group, HEAD_DIM, HEAD_DIM), F32)],
        compiler_params=_params("parallel", "arbitrary"),
    )(q, k, v, g, beta, starts, do)


@jax.custom_vjp
def _gated_delta_rule(q, k, v, g, beta):
    return _gdn_forward(q, k, v, g, beta)[0]


def _gated_delta_rule_fwd(q, k, v, g, beta):
    o, starts = _gdn_forward(q, k, v, g, beta)
    return o, (q, k, v, g, beta, starts)


def _gated_delta_rule_bwd(res, do):
    return tuple(_gdn_backward(*res, do))


_gated_delta_rule.defvjp(_gated_delta_rule_fwd, _gated_delta_rule_bwd)


def _rms_norm(x, gain):
    return x * lax.rsqrt(jnp.mean(x * x, axis=-1, keepdims=True) + EPS) * gain


def _layer_norm(x, gain):
    xc = x - jnp.mean(x, axis=-1, keepdims=True)
    return xc * lax.rsqrt(jnp.mean(xc * xc, axis=-1, keepdims=True) + EPS) * gain


def _l2_norm(x):
    return x * lax.rsqrt(jnp.sum(x * x, axis=-1, keepdims=True) + EPS)


def _causal_conv(x, w):
    t_len = x.shape[0]
    xp = jnp.pad(x, ((CONV_WIDTH - 1, 0), (0, 0)))
    return sum(xp[i:i + t_len] * w[i] for i in range(CONV_WIDTH))


def _packed_sections(d_model):
    heads = d_model // 256
    mixer = heads * HEAD_DIM
    gmlp = d_model // 2
    widths = (("gdn_qkv", 3 * mixer), ("gdn_gate", mixer), ("gmlp_uv", 2 * gmlp), ("sba_qkv", 3 * mixer),
              ("gates", N_BRANCHES * d_model), ("gdn_ab", 2 * heads))
    sections, off = {}, 0
    for name, width in widths:
        sections[name] = (off, width)
        off += width
    return sections, _round_up(off, MATMUL_K_TILE)


def _pack_w_in(w, d_model):
    sections, total = _packed_sections(d_model)
    heads = d_model // 256
    mixer = heads * HEAD_DIM
    ref_off = {"gdn_qkv": 0, "gdn_ab": 3 * mixer, "gdn_gate": 3 * mixer + 2 * heads}
    ref_off["gmlp_uv"] = ref_off["gdn_gate"] + mixer
    ref_off["sba_qkv"] = ref_off["gmlp_uv"] + d_model
    ref_off["gates"] = ref_off["sba_qkv"] + 3 * mixer
    parts = [w[..., ref_off[name]:ref_off[name] + width] for name, (_, width) in sections.items()]
    used = sum(width for _, width in sections.values())
    parts.append(jnp.zeros(w.shape[:-1] + (total - used,), w.dtype))
    return jnp.concatenate(parts, axis=-1)


def _unpack_w_in(wp, d_model):
    sections, _ = _packed_sections(d_model)
    order = ("gdn_qkv", "gdn_ab", "gdn_gate", "gmlp_uv", "sba_qkv", "gates")
    return jnp.concatenate([wp[..., sections[n][0]:sections[n][0] + sections[n][1]] for n in order], axis=-1)


@functools.partial(jax.custom_vjp, nondiff_argnums=(1,))
def _split_columns(z, widths):
    out, off = [], 0
    for width in widths:
        out.append(z[:, off:off + width])
        off += width
    return tuple(out)


def _split_columns_fwd(z, widths):
    return _split_columns(z, widths), jnp.zeros((z.shape[0], z.shape[1] - sum(widths)), z.dtype)


def _split_columns_bwd(widths, rest, gs):
    return (jnp.concatenate(list(gs) + ([rest] if rest.shape[1] else []), axis=1),)


_split_columns.defvjp(_split_columns_fwd, _split_columns_bwd)


def _mixer(h, p, layer):
    t_len, d_model = h.shape
    heads = d_model // 256
    mixer = heads * HEAD_DIM
    gmlp = d_model // 2
    sections, _ = _packed_sections(d_model)
    z = _linear(h, p["w_in"][layer], "w_in")
    gdn_qkv, gdn_gate, gmlp_uv, sba_qkv, gate_logits, gdn_ab = _split_columns(z, tuple(w for _, w in sections.values()))
    per_head = lambda t: t.reshape(t_len, heads, HEAD_DIM)

    qa, ka, va = _split_columns(jax.nn.silu(_causal_conv(gdn_qkv, p["conv_w"][layer])), (mixer,) * 3)
    qa = _l2_norm(per_head(qa)) * HEAD_DIM ** -0.5
    ka = _l2_norm(per_head(ka))
    gdn_a, gdn_b = _split_columns(gdn_ab, (heads, heads))
    to_chunks = lambda t: jnp.transpose(t).reshape(heads, t_len // CHUNK, CHUNK)
    beta = to_chunks(jax.nn.sigmoid(gdn_b))
    g = to_chunks(-jnp.exp(p["a_log"][layer]) * jax.nn.softplus(gdn_a + p["dt_bias"][layer]))
    oa = _gated_delta_rule(qa.reshape(t_len, mixer), ka.reshape(t_len, mixer), va, g, beta)
    oa = _rms_norm(per_head(oa), p["gdn_norm_g"][layer]) * jax.nn.silu(per_head(gdn_gate))
    branch_a = _linear(oa.reshape(t_len, mixer), p["w_out_a"][layer], "w_out_branch", "col")

    u, vb = _split_columns(jax.nn.gelu(gmlp_uv, approximate=False), (gmlp, gmlp))
    group_dim = gmlp // GMLP_GROUPS
    vb = _layer_norm(vb, p["gmlp_ln_g"][layer]).reshape(t_len // GMLP_BLOCK, GMLP_BLOCK, GMLP_GROUPS, group_dim)
    pos = jnp.arange(GMLP_BLOCK) // CHUNK
    ws = jnp.where((pos[None, :] <= pos[:, None])[None], p["w_spatial"][layer], 0.0)
    s = jnp.einsum("gts,nsgc->ntgc", ws.astype(BF16), vb.astype(BF16), preferred_element_type=F32)
    s = s + jnp.transpose(p["b_spatial"][layer])[None, :, :, None]
    branch_b = _linear(u * s.reshape(t_len, gmlp), p["w_out_b"][layer], "w_out_branch", "col")

    qc, kc, vc = _split_columns(sba_qkv, (mixer,) * 3)
    qc = _rms_norm(per_head(qc), p["sba_q_g"][layer]).reshape(t_len, mixer)
    kc = _rms_norm(per_head(kc), p["sba_k_g"][layer]).reshape(t_len, mixer)
    branch_c = _linear(_stick_breaking(qc, kc, vc), p["w_out_c"][layer], "w_out_branch", "col")

    gate_a, gate_b, gate_c = _split_columns(jax.nn.sigmoid(gate_logits), (d_model,) * 3)
    return _linear(gate_a * branch_a + gate_b * branch_b + gate_c * branch_c, p["w_out"][layer], "w_out", "row")


def _local_loss(x, p, target):
    depth = len(p["w_in"])
    for layer in range(depth):
        x = x + _mixer(_rms_norm(x, p["norm_mix_g"][layer]), p, layer)
        f = _linear(_rms_norm(x, p["norm_mlp_g"][layer]), p["w_ff1"][layer], "w_ff1", "col")
        x = x + _linear(jnp.square(jax.nn.relu(f)), p["w_ff2"][layer], "w_ff2", "row")
    err = jnp.square(x - target)
    return 0.5 * jnp.sum(jnp.mean(err, axis=-1))


ANY = pl.BlockSpec(memory_space=pl.ANY)


def _place():
    x, y, c = lax.axis_index("x"), lax.axis_index("y"), lax.axis_index("c")
    other_chips = [(1 - x, y), (x, 1 - y), (1 - x, 1 - y)]
    return x, y, c, other_chips


def _my_chip():
    return 2 * lax.axis_index("x") + lax.axis_index("y")


def _cast_to_slot(w, layer):
    _, rows, cols = w.shape
    tr = _row_tile(rows, cols, 16)

    def body(w_ref, o_ref):
        o_ref[...] = w_ref[...].astype(BF16)

    return pl.pallas_call(
        body,
        name="cast_to_slot",
        grid=(rows // tr,),
        in_specs=[pl.BlockSpec((None, tr, cols), lambda r: (layer, r, 0))],
        out_specs=pl.BlockSpec((None, tr, cols), lambda r: (_my_chip(), r, 0)),
        out_shape=jax.ShapeDtypeStruct((N_CHIPS, rows, cols), BF16),
        compiler_params=_params("parallel"),
    )(w)


def _remote(src, dst, sems, a, k, to):
    send_sems, recv_sems = sems
    return pltpu.make_async_remote_copy(src_ref=src, dst_ref=dst, send_sem=send_sems.at[a, k],
                                        recv_sem=recv_sems.at[a, k], device_id=to, device_id_type=MESH)


def _sem_pairs(n_arrays, n_copies):
    return [pltpu.SemaphoreType.DMA((n_arrays, n_copies)), pltpu.SemaphoreType.DMA((n_arrays, n_copies))]


def _gather_chips(bufs):
    n = len(bufs)

    def body(*refs):
        outs, sems = refs[n:2 * n], refs[2 * n:]
        x, y, c, _ = _place()
        x_nbr, y_nbr, diag, sibling = (1 - x, y), (x, 1 - y), (1 - x, 1 - y), (x, y, 1 - c)

        def piece(a, chip, core, quarter=None):
            half = bufs[a].shape[1] // 2
            rows = pl.ds(core * half, half) if quarter is None else pl.ds(core * half + quarter * (half // 2), half // 2)
            return outs[a].at[2 * chip[0] + chip[1], rows, :]

        def copy(a, k, ref, to):
            return _remote(ref, ref, sems, a, k, to)

        def sends(a):
            return [(0, piece(a, (x, y), c), (*x_nbr, c)),
                    (1, piece(a, (x, y), c), (*y_nbr, c)),
                    (2, piece(a, x_nbr, c, 0), (*y_nbr, c)),
                    (3, piece(a, y_nbr, c, 1), (*x_nbr, c)),
                    (4, piece(a, x_nbr, c), sibling),
                    (5, piece(a, y_nbr, c), sibling),
                    (6, piece(a, diag, c, 0), sibling),
                    (7, piece(a, diag, c, 1), sibling)]

        def lands(a):
            return [piece(a, x_nbr, c), piece(a, y_nbr, c), piece(a, diag, c, 0), piece(a, diag, c, 1),
                    piece(a, x_nbr, 1 - c), piece(a, y_nbr, 1 - c), piece(a, diag, 1 - c, 0), piece(a, diag, 1 - c, 1)]

        started = []

        def start(a, k):
            _, ref, to = sends(a)[k]
            started.append(copy(a, k, ref, to))
            started[-1].start()

        def arrived(a, k):
            copy(a, k, lands(a)[k], sibling).wait_recv()

        for a in range(n):
            start(a, 0)
            start(a, 1)
        for k, onward in ((0, (2, 4)), (1, (3, 5)), (2, (6,)), (3, (7,))):
            for a in range(n):
                arrived(a, k)
                for nxt in onward:
                    start(a, nxt)
        for a in range(n):
            for k in (4, 5, 6, 7):
                arrived(a, k)
        for cp in started:
            cp.wait_send()

    return pl.pallas_call(
        body,
        name="gather_chips",
        in_specs=[ANY] * n,
        out_specs=[ANY] * n,
        out_shape=[jax.ShapeDtypeStruct(b.shape, b.dtype) for b in bufs],
        input_output_aliases={a: a for a in range(n)},
        scratch_shapes=_sem_pairs(n, 8),
    )(*bufs)


def _pair_exchange(grads):
    n = len(grads)

    def body(*refs):
        ins, outs, sems = refs[:n], refs[n:2 * n], refs[2 * n:]
        x, y, c, _ = _place()
        copies = []
        for a in range(n):
            rows = grads[a].shape[1] // 2
            copies.append(_remote(ins[a].at[:, pl.ds((1 - c) * rows, rows), :], outs[a], sems, a, 0, (x, y, 1 - c)))
        for cp in copies:
            cp.start()
        for cp in copies:
            cp.wait()

    return pl.pallas_call(
        body,
        name="pair_exchange",
        in_specs=[ANY] * n,
        out_specs=[ANY] * n,
        out_shape=[jax.ShapeDtypeStruct((g.shape[0], g.shape[1] // 2, g.shape[2]), g.dtype) for g in grads],
        scratch_shapes=_sem_pairs(n, 1),
    )(*grads)


def _chip_exchange(parts):
    n = len(parts)

    def body(*refs):
        ins, outs, sems = refs[:n], refs[n:2 * n], refs[2 * n:]
        x, y, c, other_chips = _place()
        me = 2 * x + y
        copies = [_remote(ins[a].at[2 * chip[0] + chip[1]], outs[a].at[me], sems, a, j, (*chip, c))
                  for a in range(n) for j, chip in enumerate(other_chips)]
        for cp in copies:
            cp.start()
        for cp in copies:
            cp.wait()

    return pl.pallas_call(
        body,
        name="chip_exchange",
        in_specs=[ANY] * n,
        out_specs=[ANY] * n,
        out_shape=[jax.ShapeDtypeStruct(p.shape, p.dtype) for p in parts],
        scratch_shapes=_sem_pairs(n, 3),
    )(*parts)


def _pair_share(halves):
    n = len(halves)

    def body(*refs):
        outs, sems = refs[n:2 * n], refs[2 * n:]
        x, y, c, _ = _place()
        copies = []
        for a in range(n):
            rows = halves[a].shape[0] // 2
            mine = outs[a].at[pl.ds(c * rows, rows), :]
            copies.append(_remote(mine, mine, sems, a, 0, (x, y, 1 - c)))
        for cp in copies:
            cp.start()
        for cp in copies:
            cp.wait()

    return pl.pallas_call(
        body,
        name="pair_share",
        in_specs=[ANY] * n,
        out_specs=[ANY] * n,
        out_shape=[jax.ShapeDtypeStruct(h.shape, h.dtype) for h in halves],
        input_output_aliases={a: a for a in range(n)},
        scratch_shapes=_sem_pairs(n, 1),
    )(*halves)


def _all_gather_rows(block):
    m_per, n = block.shape

    def body(x_ref, out_ref, send_sems, recv_sems, local_sem):
        x, y, c, other_chips = _place()
        me, sibling = (x, y, c), (x, y, 1 - c)

        def rows(px, py, pc):
            return out_ref.at[pl.ds((4 * px + 2 * py + pc) * m_per, m_per), :]

        def copy(k, blk, to, src=None):
            return pltpu.make_async_remote_copy(src_ref=rows(*blk) if src is None else src, dst_ref=rows(*blk),
                                                send_sem=send_sems.at[k], recv_sem=recv_sems.at[k],
                                                device_id=to, device_id_type=MESH)

        mine = pltpu.make_async_copy(x_ref, rows(*me), local_sem)
        mine.start()
        first = [copy(0, me, sibling, src=x_ref)]
        first += [copy(1 + j, me, (*chip, c), src=x_ref) for j, chip in enumerate(other_chips)]
        for cp in first:
            cp.start()
        passed = [copy(4 + j, (*chip, c), sibling) for j, chip in enumerate(other_chips)]
        for j, chip in enumerate(other_chips):
            copy(1 + j, (*chip, c), me).wait_recv()
            passed[j].start()
        copy(0, sibling, me).wait_recv()
        for j, chip in enumerate(other_chips):
            copy(4 + j, (*chip, 1 - c), me).wait_recv()
        for cp in first + passed:
            cp.wait_send()
        mine.wait()

    return pl.pallas_call(
        body,
        name="all_gather_rows",
        in_specs=[ANY],
        out_specs=ANY,
        out_shape=jax.ShapeDtypeStruct((N_DEV * m_per, n), block.dtype),
        scratch_shapes=[pltpu.SemaphoreType.DMA((7,)), pltpu.SemaphoreType.DMA((7,)), pltpu.SemaphoreType.DMA],
    )(block)


def _row_tile(rows, cols, multiple):
    budget = max(multiple, ELEMENTWISE_BLOCK_ELEMS // _round_up(cols, LANES))
    t = multiple
    while t * 2 <= budget and rows % (t * 2) == 0:
        t *= 2
    return t if rows % t == 0 else rows


def _pair_sum(grad, theirs):
    chips, rows, cols = grad.shape
    half = rows // 2
    tr = _row_tile(half, cols, 16)
    blocks = half // tr

    def body(a_ref, b_ref, o_ref):
        o_ref[...] = (a_ref[...].astype(F32) + b_ref[...].astype(F32)).astype(o_ref.dtype)

    return pl.pallas_call(
        body,
        name="pair_sum",
        grid=(chips, blocks),
        in_specs=[pl.BlockSpec((None, tr, cols), lambda s, r: (s, lax.axis_index("c") * blocks + r, 0)),
                  pl.BlockSpec((None, tr, cols), lambda s, r: (s, r, 0))],
        out_specs=pl.BlockSpec((None, tr, cols), lambda s, r: (s, r, 0)),
        out_shape=jax.ShapeDtypeStruct((chips, half, cols), grad.dtype),
        compiler_params=_params("parallel", "parallel"),
    )(grad, theirs)


def _chip_sum(mine, others):
    chips, half, cols = mine.shape
    tr = _row_tile(half, cols, 16)
    blocks = half // tr

    def body(own_ref, *refs):
        o_ref = refs[-1]
        total = own_ref[...].astype(F32)
        for ref in refs[:-1]:
            total = total + ref[...].astype(F32)
        o_ref[...] = total

    slot = lambda q: pl.BlockSpec((None, tr, cols), lambda r: ((_my_chip() + q) % chips, r, 0))
    return pl.pallas_call(
        body,
        name="chip_sum",
        grid=(blocks,),
        in_specs=[slot(q) for q in range(chips)],
        out_specs=pl.BlockSpec((tr, cols), lambda r: (lax.axis_index("c") * blocks + r, 0)),
        out_shape=jax.ShapeDtypeStruct((2 * half, cols), F32),
        compiler_params=_params("parallel"),
    )(mine, *([others] * (chips - 1)))


def _device_sum(blocks, m_per):
    tr = _row_tile(m_per, LANES, 8)
    per = m_per // tr

    def body(*refs):
        o_ref = refs[-1]
        total = refs[0][...]
        for ref in refs[1:-1]:
            total = total + ref[...]
        o_ref[...] = total

    return pl.pallas_call(
        body,
        name="device_sum",
        grid=(per,),
        in_specs=[pl.BlockSpec((tr, LANES), functools.partial(lambda d, r: (d * per + r, 0), d)) for d in range(N_DEV)],
        out_specs=pl.BlockSpec((tr, LANES), lambda r: (r, 0)),
        out_shape=jax.ShapeDtypeStruct((m_per, LANES), F32),
        compiler_params=_params("parallel"),
    )(*([blocks] * N_DEV))


def _adamw(grads, w, m, v):
    depth, rows, cols = w.shape
    tr = _row_tile(rows, cols, 8) if rows % 8 == 0 else rows
    spec = pl.BlockSpec((None, tr, cols), lambda l, r: (l, r, 0))
    grad_spec = lambda q: pl.BlockSpec((tr, cols), lambda l, r: (jnp.where(l == q, r, 0), 0))

    def body(*refs):
        g_refs, (w_ref, m_ref, v_ref, go_ref, d_ref, mo_ref, vo_ref) = refs[:depth], refs[depth:]
        layer = pl.program_id(0)
        g = g_refs[0][...]
        for q in range(1, depth):
            g = jnp.where(layer == q, g_refs[q][...], g)
        m_new = ADAM_B1 * m_ref[...] + (1.0 - ADAM_B1) * g
        v_new = ADAM_B2 * v_ref[...] + (1.0 - ADAM_B2) * jnp.square(g)
        m_hat = m_new / (1.0 - ADAM_B1 ** ADAM_STEP)
        v_hat = v_new / (1.0 - ADAM_B2 ** ADAM_STEP)
        go_ref[...] = g
        d_ref[...] = -ADAM_LR * (m_hat / (jnp.sqrt(v_hat) + ADAM_EPS) + ADAM_WD * w_ref[...])
        mo_ref[...] = m_new
        vo_ref[...] = v_new

    return pl.pallas_call(
        body,
        name="adamw",
        grid=(depth, rows // tr),
        in_specs=[grad_spec(q) for q in range(depth)] + [spec] * 3,
        out_specs=[spec] * 4,
        out_shape=[jax.ShapeDtypeStruct(w.shape, F32)] * 4,
        compiler_params=_params("parallel", "parallel"),
    )(*grads, w, m, v)


def _join_cols(gathered):
    return jnp.concatenate([gathered[s] for s in range(N_CHIPS)], axis=-1)


def _split_cols(full):
    rows, cols = full.shape
    return jnp.transpose(full.reshape(rows, N_CHIPS, cols // N_CHIPS), (1, 0, 2))


def _reduce_scatter(grads):
    theirs = _pair_exchange(grads)
    pairs = [_pair_sum(g, t) for g, t in zip(grads, theirs)]
    others = _chip_exchange(pairs)
    return _pair_share([_chip_sum(p, o) for p, o in zip(pairs, others)])


def _pack_rows(arrays):
    parts, counts = [], []
    for a in arrays:
        flat = a.reshape(-1).astype(F32)
        n_rows = _round_up(flat.shape[0], 8 * LANES) // LANES
        parts.append(jnp.pad(flat, (0, n_rows * LANES - flat.shape[0])).reshape(n_rows, LANES))
        counts.append(n_rows)
    return jnp.concatenate(parts, axis=0), counts


def _pad_rows(a, rows):
    return jnp.pad(a, ((0, rows - a.shape[0]), (0, 0)))


def _unpack_rows(packed, counts, shapes):
    out, row = [], 0
    for n_rows, shape in zip(counts, shapes):
        size = 1
        for d in shape:
            size *= d
        out.append(packed[row:row + n_rows].reshape(-1)[:size].reshape(shape))
        row += n_rows
    return out


SHARDED = (("w_in", "col"), ("w_out_a", "col"), ("w_out_b", "col"), ("w_out_c", "col"), ("w_out", "row"),
           ("w_ff1", "col"), ("w_ff2", "row"))


def kernel(x, w_in, conv_w, a_log, dt_bias, gdn_norm_g, gmlp_ln_g, w_spatial, b_spatial, sba_q_g, sba_k_g, w_out_a, w_out_b, w_out_c, w_out, norm_mix_g, norm_mlp_g, w_ff1, w_ff2, loss_target, m_w_in, m_conv_w, m_a_log, m_dt_bias, m_gdn_norm_g, m_gmlp_ln_g, m_w_spatial, m_b_spatial, m_sba_q_g, m_sba_k_g, m_w_out_a, m_w_out_b, m_w_out_c, m_w_out, m_norm_mix_g, m_norm_mlp_g, m_w_ff1, m_w_ff2, v_w_in, v_conv_w, v_a_log, v_dt_bias, v_gdn_norm_g, v_gmlp_ln_g, v_w_spatial, v_b_spatial, v_sba_q_g, v_sba_k_g, v_w_out_a, v_w_out_b, v_w_out_c, v_w_out, v_norm_mix_g, v_norm_mlp_g, v_w_ff1, v_w_ff2):
    given = dict(locals())
    weights = {n: given[n] for n in WEIGHT_NAMES}
    depth, d_model = w_in.shape[0], w_in.shape[1]
    layers = range(depth)
    chip = _my_chip()

    gathered = [_gather_chips([_cast_to_slot(weights[name], l) for name, _ in SHARDED]) for l in layers]
    conv_rows = depth * CONV_WIDTH
    conv_cols = conv_w.shape[-1]
    conv_all = _all_gather_rows(_pad_rows(conv_w.reshape(conv_rows, conv_cols), _round_up(conv_rows, 8)))
    conv_all = conv_all.reshape(N_CHIPS, 2, _round_up(conv_rows, 8), conv_cols)[:, 0, :conv_rows]
    conv_full = jnp.concatenate([conv_all[s] for s in range(N_CHIPS)], axis=-1).reshape(depth, CONV_WIDTH, N_CHIPS * conv_cols)

    params = {n: weights[n] for n in REPLICATED_NAMES}
    params["conv_w"] = conv_full
    for i, (name, _) in enumerate(SHARDED):
        params[name] = [gathered[l][i] for l in layers]
    params["w_in"] = [_pack_w_in(_join_cols(gathered[l][0]), d_model) for l in layers]

    loss_local, (grad_x, grads) = jax.value_and_grad(_local_loss, argnums=(0, 1))(x[0], params, loss_target[0])

    summed = []
    for l in layers:
        local = [grads[name][l] for name, _ in SHARDED]
        local[0] = _split_cols(_unpack_w_in(local[0], d_model))
        summed.append(_reduce_scatter(local))
    rep_packed, rep_counts = _pack_rows([grads[n] for n in REPLICATED_NAMES])
    rep_rows = _round_up(rep_packed.shape[0], SMALL_ROW_ALIGN)
    rest_packed, rest_counts = _pack_rows([grads["conv_w"], loss_local])
    m_per = rep_rows + _round_up(rest_packed.shape[0], SMALL_ROW_ALIGN)
    packed = jnp.concatenate([_pad_rows(rep_packed, rep_rows), _pad_rows(rest_packed, m_per - rep_rows)], axis=0)
    total = _device_sum(_all_gather_rows(packed), m_per)
    conv_sum, loss = _unpack_rows(total[rep_rows:], rest_counts, [grads["conv_w"].shape, ()])
    conv_grad = lax.dynamic_slice_in_dim(conv_sum, chip * conv_cols, conv_cols, axis=2)

    out = {}
    for i, (name, _) in enumerate(SHARDED):
        out[name] = _adamw([summed[l][i] for l in layers], weights[name], given["m_" + name], given["v_" + name])
    out["conv_w"] = _adamw([conv_grad[l] for l in layers], conv_w, m_conv_w, v_conv_w)
    rep_shapes = [weights[n].shape for n in REPLICATED_NAMES]
    pack3 = lambda prefix: _pad_rows(_pack_rows([given[prefix + n] for n in REPLICATED_NAMES])[0], rep_rows)[None]
    rep = _adamw([total[:rep_rows]], pack3(""), pack3("m_"), pack3("v_"))
    rep = [_unpack_rows(t[0], rep_counts, rep_shapes) for t in rep]
    for i, name in enumerate(REPLICATED_NAMES):
        out[name] = tuple(t[i] for t in rep)

    results = [loss, grad_x[None]]
    for kind in range(4):
        results += [out[n][kind] for n in WEIGHT_NAMES]
    return tuple(results)
```

```python
import functools

import jax
import jax.numpy as jnp
from jax import lax
from jax.experimental import pallas as pl
from jax.experimental.pallas import tpu as pltpu

F32, BF16 = jnp.float32, jnp.bfloat16
MESH = pl.DeviceIdType.MESH
N_CHIPS = 4
N_DEV = 8

EPS = 1e-6
CHUNK = 64
HEAD_DIM = 128
CONV_WIDTH = 4
GMLP_GROUPS = 8
GMLP_BLOCK = 128
N_BRANCHES = 3
ADAM_LR, ADAM_B1, ADAM_B2, ADAM_EPS, ADAM_WD, ADAM_STEP = 0.001, 0.9, 0.999, 1e-08, 0.01, 10

V7X_VMEM_BYTES = 64 * 1024 * 1024
VMEM_LIMIT = V7X_VMEM_BYTES - 8 * 1024 * 1024
LANES = 128
MATMUL_TILE = 1024
MATMUL_K_TILE = 2048
ATTN_TILE = 256
GDN_HEADS_PER_STEP = 4
GDN_SEGMENT = 512
ELEMENTWISE_BLOCK_ELEMS = 512 * 1024
SMALL_ROW_ALIGN = 256

WEIGHT_NAMES = ("w_in", "conv_w", "a_log", "dt_bias", "gdn_norm_g", "gmlp_ln_g", "w_spatial", "b_spatial",
                "sba_q_g", "sba_k_g", "w_out_a", "w_out_b", "w_out_c", "w_out", "norm_mix_g", "norm_mlp_g",
                "w_ff1", "w_ff2")
REPLICATED_NAMES = ("a_log", "dt_bias", "gdn_norm_g", "gmlp_ln_g", "w_spatial", "b_spatial", "sba_q_g",
                    "sba_k_g", "norm_mix_g", "norm_mlp_g")


def _round_up(n, m):
    return (n + m - 1) // m * m


def _tile(dim, pref):
    if dim <= pref:
        return dim
    t = pref // LANES * LANES
    while t >= LANES:
        if dim % t == 0:
            return t
        t -= LANES
    return dim


def _params(*semantics):
    return pltpu.CompilerParams(dimension_semantics=semantics, vmem_limit_bytes=VMEM_LIMIT)


_DOT_DIMS = {"nn": ((1,), (0,)), "nt": ((1,), (1,)), "tn": ((0,), (0,))}


def _logical_shape(shape, kind):
    if kind is None:
        return shape
    chips, rows, cols = shape
    return (rows, chips * cols) if kind == "col" else (chips * rows, cols)


def _weight_tiles(shape, kind, row_pref, col_pref):
    rows, cols = _logical_shape(shape, kind)
    tr = _tile(shape[1] if kind == "row" else rows, row_pref)
    tc = _tile(shape[2] if kind == "col" else cols, col_pref)
    return tr, tc


def _weight_spec(shape, kind, tr, tc, pick):
    if kind is None:
        return pl.BlockSpec((tr, tc), pick)
    if kind == "col":
        per = shape[2] // tc
        return pl.BlockSpec((None, tr, tc), lambda i, j, k: (pick(i, j, k)[1] // per, pick(i, j, k)[0], pick(i, j, k)[1] % per))
    per = shape[1] // tr
    return pl.BlockSpec((None, tr, tc), lambda i, j, k: (pick(i, j, k)[0] // per, pick(i, j, k)[0] % per, pick(i, j, k)[1]))


def _matmul(a, b, mode, out_dtype, name, kind=None, out_shape=None):
    if mode == "tn":
        k_dim, m_dim = a.shape
        n_dim = b.shape[1]
        out_shape = (m_dim, n_dim) if kind is None else out_shape
        tm, tn = _weight_tiles(out_shape, kind, MATMUL_TILE, MATMUL_TILE)
        tk = _tile(k_dim, MATMUL_K_TILE)
        a_spec = pl.BlockSpec((tk, tm), lambda i, j, k: (k, i))
        b_spec = pl.BlockSpec((tk, tn), lambda i, j, k: (k, j))
        out_spec = _weight_spec(out_shape, kind, tm, tn, lambda i, j, k: (i, j))
    else:
        m_dim, k_dim = a.shape
        tm = _tile(m_dim, MATMUL_TILE)
        if mode == "nn":
            _, n_dim = _logical_shape(b.shape, kind)
            tk, tn = _weight_tiles(b.shape, kind, MATMUL_K_TILE, MATMUL_TILE)
            b_spec = _weight_spec(b.shape, kind, tk, tn, lambda i, j, k: (k, j))
        else:
            n_dim, _ = _logical_shape(b.shape, kind)
            tn, tk = _weight_tiles(b.shape, kind, MATMUL_TILE, MATMUL_K_TILE)
            b_spec = _weight_spec(b.shape, kind, tn, tk, lambda i, j, k: (j, k))
        a_spec = pl.BlockSpec((tm, tk), lambda i, j, k: (i, k))
        out_shape = (m_dim, n_dim)
        out_spec = pl.BlockSpec((tm, tn), lambda i, j, k: (i, j))
    nk = k_dim // tk
    dims = (_DOT_DIMS[mode], ((), ()))

    def body(a_ref, b_ref, o_ref, *scratch):
        part = lax.dot_general(a_ref[...], b_ref[...], dims, preferred_element_type=F32)
        if nk == 1:
            o_ref[...] = part.astype(out_dtype)
            return
        acc_ref, = scratch
        k = pl.program_id(2)

        @pl.when(k == 0)
        def _():
            acc_ref[...] = part

        @pl.when(k > 0)
        def _():
            acc_ref[...] += part

        @pl.when(k == nk - 1)
        def _():
            o_ref[...] = acc_ref[...].astype(out_dtype)

    return pl.pallas_call(
        body,
        name=name,
        grid=(m_dim // tm, n_dim // tn, nk),
        in_specs=[a_spec, b_spec],
        out_specs=out_spec,
        out_shape=jax.ShapeDtypeStruct(out_shape, out_dtype),
        scratch_shapes=[] if nk == 1 else [pltpu.VMEM((tm, tn), F32)],
        compiler_params=_params("parallel", "parallel", "arbitrary"),
    )(a, b)


@functools.partial(jax.custom_vjp, nondiff_argnums=(2, 3))
def _linear(a, w, name, kind=None):
    return _matmul(a.astype(BF16), w, "nn", F32, name + "_fwd", kind)


def _linear_fwd(a, w, name, kind):
    a16 = a.astype(BF16)
    return _matmul(a16, w, "nn", F32, name + "_fwd", kind), (a16, w)


def _linear_bwd(name, kind, res, g):
    a16, w = res
    g16 = g.astype(BF16)
    da = _matmul(g16, w, "nt", F32, name + "_dgrad", kind)
    dw = _matmul(a16, g16, "tn", BF16, name + "_wgrad", kind, w.shape)
    return da, dw


_linear.defvjp(_linear_fwd, _linear_bwd)


def _split3_dot(x, ones_mat):
    hi = x.astype(BF16)
    rest = x - hi.astype(F32)
    mid = rest.astype(BF16)
    lo = (rest - mid.astype(F32)).astype(BF16)
    dot = lambda p: jnp.dot(p, ones_mat, preferred_element_type=F32)
    return dot(hi) + dot(mid) + dot(lo)


def _dot_nt(a, b):
    return lax.dot_general(a, b, (((1,), (1,)), ((), ())), preferred_element_type=F32)


def _dot_tn(a, b):
    return lax.dot_general(a, b, (((0,), (0,)), ((), ())), preferred_element_type=F32)


def _sba_scores(q16, k16, tile, scale, diagonal):
    z = _dot_nt(q16, k16) * scale
    log_sig = jnp.minimum(z, 0.0) - jnp.log(1.0 + jnp.exp(-jnp.abs(z)))
    if not diagonal:
        return None, log_sig, log_sig - z
    mask = lax.broadcasted_iota(jnp.int32, (tile, tile), 1) < lax.broadcasted_iota(jnp.int32, (tile, tile), 0)
    return mask, log_sig, jnp.where(mask, log_sig - z, 0.0)


def _masked(mask, value):
    return value if mask is None else jnp.where(mask, value, 0.0)


def _sba_forward(q, k, v):
    t_len, width = q.shape
    heads = width // HEAD_DIM
    tile = min(ATTN_TILE, t_len)
    scale = HEAD_DIM ** -0.5

    def body(q_ref, k_ref, v_ref, o_ref, r_ref):
        i = pl.program_id(1)
        q16 = q_ref[...].astype(BF16)
        ri = lax.broadcasted_iota(jnp.int32, (tile, tile), 0)
        ci = lax.broadcasted_iota(jnp.int32, (tile, tile), 1)
        later = (ri > ci).astype(BF16)

        def tile_step(j, carry, diagonal):
            run, acc = carry
            off = pl.multiple_of(j * tile, tile)
            k16 = k_ref[pl.ds(off, tile), :].astype(BF16)
            v16 = v_ref[pl.ds(off, tile), :].astype(BF16)
            mask, log_sig, log_keep = _sba_scores(q16, k16, tile, scale, diagonal)
            suffix = _split3_dot(log_keep, later) + run
            att = _masked(mask, jnp.exp(log_sig + suffix))
            acc = acc + jnp.dot(att.astype(BF16), v16, preferred_element_type=F32)
            run = run + jnp.sum(log_keep, axis=1, keepdims=True)
            return run, acc

        carry = tile_step(i, (jnp.zeros((tile, 1), F32), jnp.zeros((tile, HEAD_DIM), F32)), True)
        run, acc = lax.fori_loop(0, i, lambda jj, carry: tile_step(i - 1 - jj, carry, False), carry)
        o_ref[...] = acc
        r_ref[...] = run

    return pl.pallas_call(
        body,
        name="sba_fwd",
        grid=(heads, t_len // tile),
        in_specs=[pl.BlockSpec((tile, HEAD_DIM), lambda h, i: (i, h)),
                  pl.BlockSpec((t_len, HEAD_DIM), lambda h, i: (0, h)),
                  pl.BlockSpec((t_len, HEAD_DIM), lambda h, i: (0, h))],
        out_specs=[pl.BlockSpec((tile, HEAD_DIM), lambda h, i: (i, h)),
                   pl.BlockSpec((None, tile, 1), lambda h, i: (h, i, 0))],
        out_shape=[jax.ShapeDtypeStruct((t_len, width), F32), jax.ShapeDtypeStruct((heads, t_len, 1), F32)],
        compiler_params=_params("parallel", "arbitrary"),
    )(q, k, v)


def _sba_backward(q, k, v, total, do):
    t_len, width = q.shape
    heads = width // HEAD_DIM
    tile = min(ATTN_TILE, t_len)
    scale = HEAD_DIM ** -0.5

    def body(q_ref, k_ref, v_ref, r_ref, do_ref, dq_ref, dk_ref, dv_ref):
        i = pl.program_id(1)

        @pl.when(i == 0)
        def _():
            dk_ref[...] = jnp.zeros_like(dk_ref)
            dv_ref[...] = jnp.zeros_like(dv_ref)

        q16 = q_ref[...].astype(BF16)
        do16 = do_ref[...].astype(BF16)
        tot = r_ref[...]
        ri = lax.broadcasted_iota(jnp.int32, (tile, tile), 0)
        ci = lax.broadcasted_iota(jnp.int32, (tile, tile), 1)
        upto = (ri <= ci).astype(BF16)
        before = (ri < ci).astype(BF16)

        def tile_step(j, carry, diagonal):
            keep_left, w_left, dq = carry
            off = pl.multiple_of(j * tile, tile)
            k16 = k_ref[pl.ds(off, tile), :].astype(BF16)
            v16 = v_ref[pl.ds(off, tile), :].astype(BF16)
            mask, log_sig, log_keep = _sba_scores(q16, k16, tile, scale, diagonal)
            suffix = tot - keep_left - _split3_dot(log_keep, upto)
            att = _masked(mask, jnp.exp(log_sig + suffix))
            w = att * _dot_nt(do16, v16)
            d_keep = w_left + _split3_dot(w, before)
            sig = jnp.exp(log_sig)
            dz = _masked(mask, w * (1.0 - sig) - sig * d_keep) * scale
            dz16 = dz.astype(BF16)
            dq = dq + jnp.dot(dz16, k16, preferred_element_type=F32)
            dk_ref[pl.ds(off, tile), :] += _dot_tn(dz16, q16)
            dv_ref[pl.ds(off, tile), :] += _dot_tn(att.astype(BF16), do16)
            keep_left = keep_left + jnp.sum(log_keep, axis=1, keepdims=True)
            w_left = w_left + jnp.sum(w, axis=1, keepdims=True)
            return keep_left, w_left, dq

        zero = jnp.zeros((tile, 1), F32)
        carry = lax.fori_loop(0, i, lambda j, carry: tile_step(j, carry, False), (zero, zero, jnp.zeros((tile, HEAD_DIM), F32)))
        _, _, dq = tile_step(i, carry, True)
        dq_ref[...] = dq

    q_spec = pl.BlockSpec((tile, HEAD_DIM), lambda h, i: (i, h))
    kv_spec = pl.BlockSpec((t_len, HEAD_DIM), lambda h, i: (0, h))
    return pl.pallas_call(
        body,
        name="sba_bwd",
        grid=(heads, t_len // tile),
        in_specs=[q_spec, kv_spec, kv_spec, pl.BlockSpec((None, tile, 1), lambda h, i: (h, i, 0)), q_spec],
        out_specs=[q_spec, kv_spec, kv_spec],
        out_shape=[jax.ShapeDtypeStruct((t_len, width), F32)] * 3,
        compiler_params=_params("parallel", "arbitrary"),
    )(q, k, v, total, do)


@jax.custom_vjp
def _stick_breaking(q, k, v):
    return _sba_forward(q, k, v)[0]


def _stick_breaking_fwd(q, k, v):
    o, total = _sba_forward(q, k, v)
    return o, (q, k, v, total)


def _stick_breaking_bwd(res, do):
    q, k, v, total = res
    return tuple(_sba_backward(q, k, v, total, do))


_stick_breaking.defvjp(_stick_breaking_fwd, _stick_breaking_bwd)


def _contract(a16, b16, ca, cb):
    return lax.dot_general(a16, b16, (((ca,), (cb,)), ((), ())), preferred_element_type=F32)


def _pdot_raw(a, b, ca, cb):
    a_hi = a.astype(BF16)
    a_lo = (a - a_hi.astype(F32)).astype(BF16)
    b_hi = b.astype(BF16)
    b_lo = (b - b_hi.astype(F32)).astype(BF16)
    return _contract(a_hi, b_hi, ca, cb) + _contract(a_hi, b_lo, ca, cb) + _contract(a_lo, b_hi, ca, cb)


def _bdot_raw(a, b, ca, cb):
    return _contract(a.astype(BF16), b.astype(BF16), ca, cb)


def _make_dot(raw):
    @functools.partial(jax.custom_vjp, nondiff_argnums=(2, 3))
    def dot(a, b, ca, cb):
        return raw(a, b, ca, cb)

    def fwd(a, b, ca, cb):
        return raw(a, b, ca, cb), (a, b)

    def bwd(ca, cb, res, g):
        a, b = res
        da = raw(g, b, 1, 1 - cb) if ca == 1 else raw(b, g, 1 - cb, 1)
        db = raw(a, g, 1 - ca, 0) if cb == 0 else raw(g, a, 0, 1 - ca)
        return da, db

    dot.defvjp(fwd, bwd)
    return dot


_bdot = _make_dot(_bdot_raw)


@jax.custom_vjp
def _unit_lower_inverses(lows):
    size = lows[0].shape[0]
    eye = (lax.broadcasted_iota(jnp.int32, (size, size), 0) == lax.broadcasted_iota(jnp.int32, (size, size), 1)).astype(F32)
    invs = [eye - low for low in lows]
    powers = list(lows)
    span = 2
    while span < size:
        powers = [_bdot_raw(p, p, 1, 0) for p in powers]
        invs = [inv + _bdot_raw(inv, p, 1, 0) for inv, p in zip(invs, powers)]
        span *= 2
    resids = [eye - inv - _pdot_raw(low, inv, 1, 0) for low, inv in zip(lows, invs)]
    return [inv + _bdot_raw(inv, r, 1, 0) for inv, r in zip(invs, resids)]


def _unit_lower_inverses_fwd(lows):
    invs = _unit_lower_inverses(lows)
    return invs, invs


def _unit_lower_inverses_bwd(invs, gs):
    left = [_bdot_raw(inv, g, 0, 0) for inv, g in zip(invs, gs)]
    return ([-_bdot_raw(l, inv, 1, 1) for l, inv in zip(left, invs)],)


_unit_lower_inverses.defvjp(_unit_lower_inverses_fwd, _unit_lower_inverses_bwd)


def _gdn_chunks(states, qs, ks, vs, g_rows, b_rows):
    heads = range(len(qs))
    size = qs[0].shape[0]
    ri = lax.broadcasted_iota(jnp.int32, (size, size), 0)
    ci = lax.broadcasted_iota(jnp.int32, (size, size), 1)
    eye, incl, strict = ri == ci, ci <= ri, ci < ri
    square = lambda t: jnp.broadcast_to(t, (size, size))
    g_rowb = [square(g) for g in g_rows]
    g_col = [jnp.sum(jnp.where(eye, g, 0.0), axis=1, keepdims=True) for g in g_rowb]
    b_col = [jnp.sum(jnp.where(eye, square(b), 0.0), axis=1, keepdims=True) for b in b_rows]
    gc_col = [jnp.sum(jnp.where(incl, g, 0.0), axis=1, keepdims=True) for g in g_rowb]
    gc_row = [jnp.sum(jnp.where(ri <= ci, square(g), 0.0), axis=0, keepdims=True) for g in g_col]
    g_last = [jnp.sum(g, axis=1, keepdims=True) for g in g_rows]
    decay = [jnp.where(incl, jnp.exp(jnp.where(incl, gc_col[h] - gc_row[h], 0.0)), 0.0) for h in heads]
    kb = [ks[h] * b_col[h] for h in heads]
    kk = [_bdot(kb[h], ks[h], 1, 1) for h in heads]
    invs = _unit_lower_inverses([jnp.where(strict, kk[h] * decay[h], 0.0) for h in heads])
    grow = [jnp.exp(gc_col[h]) for h in heads]
    u = [_bdot(invs[h], vs[h] * b_col[h], 1, 0) for h in heads]
    w = [_bdot(invs[h], kb[h] * grow[h], 1, 0) for h in heads]
    qk = [_bdot(qs[h], ks[h], 1, 1) for h in heads]
    ws = [_bdot(w[h], states[h], 1, 0) for h in heads]
    qs_state = [_bdot(qs[h] * grow[h], states[h], 1, 0) for h in heads]
    v_new = [u[h] - ws[h] for h in heads]
    inner = [_bdot(qk[h] * decay[h], v_new[h], 1, 0) for h in heads]
    k_dec = [ks[h] * jnp.exp(g_last[h] - gc_col[h]) for h in heads]
    kv = [_bdot(k_dec[h], v_new[h], 0, 0) for h in heads]
    new_states = [states[h] * jnp.exp(g_last[h]) + kv[h] for h in heads]
    return new_states, [qs_state[h] + inner[h] for h in heads]


def _gdn_layout(q):
    t_len, width = q.shape
    heads = width // HEAD_DIM
    group = min(GDN_HEADS_PER_STEP, heads)
    seg = min(GDN_SEGMENT, t_len)
    return heads, group, seg, (heads // group, t_len // seg)


def _gdn_specs(group, seg, order):
    chunks = seg // CHUNK
    seq = pl.BlockSpec((seg, group * HEAD_DIM), lambda h, t: (order(t), h))
    gate = pl.BlockSpec((group, chunks, CHUNK), lambda h, t: (h, order(t), 0))
    states = pl.BlockSpec((group, chunks, HEAD_DIM, HEAD_DIM), lambda h, t: (h, order(t), 0, 0))
    return seq, gate, states


def _gdn_forward(q, k, v, g, beta):
    t_len, width = q.shape
    heads, group, seg, grid = _gdn_layout(q)
    seq, gate, states = _gdn_specs(group, seg, lambda t: t)

    def body(q_ref, k_ref, v_ref, g_ref, b_ref, o_ref, s_ref, state_ref):
        @pl.when(pl.program_id(1) == 0)
        def _():
            state_ref[...] = jnp.zeros_like(state_ref)

        def step(n, carry):
            rows = pl.ds(pl.multiple_of(n * CHUNK, CHUNK), CHUNK)
            cols = [slice(j * HEAD_DIM, (j + 1) * HEAD_DIM) for j in range(group)]
            states = [state_ref[j] for j in range(group)]
            for j in range(group):
                s_ref[j, n] = states[j]
            states, outs = _gdn_chunks(states, [q_ref[rows, c] for c in cols], [k_ref[rows, c] for c in cols],
                                       [v_ref[rows, c] for c in cols], [g_ref[j, pl.ds(n, 1), :] for j in range(group)],
                                       [b_ref[j, pl.ds(n, 1), :] for j in range(group)])
            for j in range(group):
                o_ref[rows, cols[j]] = outs[j]
                state_ref[j] = states[j]
            return carry

        lax.fori_loop(0, seg // CHUNK, step, 0)

    return pl.pallas_call(
        body,
        name="gdn_fwd",
        grid=grid,
        in_specs=[seq, seq, seq, gate, gate],
        out_specs=[seq, states],
        out_shape=[jax.ShapeDtypeStruct((t_len, width), F32),
                   jax.ShapeDtypeStruct((heads, t_len // CHUNK, HEAD_DIM, HEAD_DIM), F32)],
        scratch_shapes=[pltpu.VMEM((group, HEAD_DIM, HEAD_DIM), F32)],
        compiler_params=_params("parallel", "arbitrary"),
    )(q, k, v, g, beta)


def _gdn_backward(q, k, v, g, beta, starts, do):
    t_len, width = q.shape
    heads, group, seg, grid = _gdn_layout(q)
    last = grid[1] - 1
    seq, gate, states = _gdn_specs(group, seg, lambda t: last - t)
    chunks = seg // CHUNK

    def body(q_ref, k_ref, v_ref, g_ref, b_ref, s_ref, do_ref, dq_ref, dk_ref, dv_ref, dg_ref, db_ref, d_state_ref):
        @pl.when(pl.program_id(1) == 0)
        def _():
            d_state_ref[...] = jnp.zeros_like(d_state_ref)

        def step(m, carry):
            n = chunks - 1 - m
            rows = pl.ds(pl.multiple_of(n * CHUNK, CHUNK), CHUNK)
            cols = [slice(j * HEAD_DIM, (j + 1) * HEAD_DIM) for j in range(group)]
            per_head = lambda ref: [ref[rows, c] for c in cols]
            _, pull = jax.vjp(_gdn_chunks, [s_ref[j, n] for j in range(group)], per_head(q_ref), per_head(k_ref),
                              per_head(v_ref), [g_ref[j, pl.ds(n, 1), :] for j in range(group)],
                              [b_ref[j, pl.ds(n, 1), :] for j in range(group)])
            d_states, dq, dk, dv, dg, db = pull(([d_state_ref[j] for j in range(group)], per_head(do_ref)))
            for j in range(group):
                dq_ref[rows, cols[j]] = dq[j]
                dk_ref[rows, cols[j]] = dk[j]
                dv_ref[rows, cols[j]] = dv[j]
                dg_ref[j, pl.ds(n, 1), :] = dg[j]
                db_ref[j, pl.ds(n, 1), :] = db[j]
                d_state_ref[j] = d_states[j]
            return carry

        lax.fori_loop(0, chunks, step, 0)

    return pl.pallas_call(
        body,
        name="gdn_bwd",
        grid=grid,
        in_specs=[seq, seq, seq, gate, gate, states, seq],
        out_specs=[seq, seq, seq, gate, gate],
        out_shape=[jax.ShapeDtypeStruct((t_len, width), F32)] * 3
        + [jax.ShapeDtypeStruct((heads, t_len // CHUNK, CHUNK), F32)] * 2,
        scratch_shapes=[pltpu.VMEM((group, HEAD_DIM, HEAD_DIM), F32)],
        compiler_params=_params("parallel", "arbitrary"),
    )(q, k, v, g, beta, starts, do)


@jax.custom_vjp
def _gated_delta_rule(q, k, v, g, beta):
    return _gdn_forward(q, k, v, g, beta)[0]


def _gated_delta_rule_fwd(q, k, v, g, beta):
    o, starts = _gdn_forward(q, k, v, g, beta)
    return o, (q, k, v, g, beta, starts)


def _gated_delta_rule_bwd(res, do):
    return tuple(_gdn_backward(*res, do))


_gated_delta_rule.defvjp(_gated_delta_rule_fwd, _gated_delta_rule_bwd)


def _rms_norm(x, gain):
    return x * lax.rsqrt(jnp.mean(x * x, axis=-1, keepdims=True) + EPS) * gain


def _layer_norm(x, gain):
    xc = x - jnp.mean(x, axis=-1, keepdims=True)
    return xc * lax.rsqrt(jnp.mean(xc * xc, axis=-1, keepdims=True) + EPS) * gain


def _l2_norm(x):
    return x * lax.rsqrt(jnp.sum(x * x, axis=-1, keepdims=True) + EPS)


def _causal_conv(x, w):
    t_len = x.shape[0]
    xp = jnp.pad(x, ((CONV_WIDTH - 1, 0), (0, 0)))
    return sum(xp[i:i + t_len] * w[i] for i in range(CONV_WIDTH))


def _packed_sections(d_model):
    heads = d_model // 256
    mixer = heads * HEAD_DIM
    gmlp = d_model // 2
    widths = (("gdn_qkv", 3 * mixer), ("gdn_gate", mixer), ("gmlp_uv", 2 * gmlp), ("sba_qkv", 3 * mixer),
              ("gates", N_BRANCHES * d_model), ("gdn_ab", 2 * heads))
    sections, off = {}, 0
    for name, width in widths:
        sections[name] = (off, width)
        off += width
    return sections, _round_up(off, MATMUL_K_TILE)


def _pack_w_in(w, d_model):
    sections, total = _packed_sections(d_model)
    heads = d_model // 256
    mixer = heads * HEAD_DIM
    ref_off = {"gdn_qkv": 0, "gdn_ab": 3 * mixer, "gdn_gate": 3 * mixer + 2 * heads}
    ref_off["gmlp_uv"] = ref_off["gdn_gate"] + mixer
    ref_off["sba_qkv"] = ref_off["gmlp_uv"] + d_model
    ref_off["gates"] = ref_off["sba_qkv"] + 3 * mixer
    parts = [w[..., ref_off[name]:ref_off[name] + width] for name, (_, width) in sections.items()]
    used = sum(width for _, width in sections.values())
    parts.append(jnp.zeros(w.shape[:-1] + (total - used,), w.dtype))
    return jnp.concatenate(parts, axis=-1)


def _unpack_w_in(wp, d_model):
    sections, _ = _packed_sections(d_model)
    order = ("gdn_qkv", "gdn_ab", "gdn_gate", "gmlp_uv", "sba_qkv", "gates")
    return jnp.concatenate([wp[..., sections[n][0]:sections[n][0] + sections[n][1]] for n in order], axis=-1)


def _mixer(h, p, layer):
    t_len, d_model = h.shape
    heads = d_model // 256
    mixer = heads * HEAD_DIM
    gmlp = d_model // 2
    sections, _ = _packed_sections(d_model)
    z = _linear(h, p["w_in"][layer], "w_in")
    cut = lambda name: z[:, sections[name][0]:sections[name][0] + sections[name][1]]
    per_head = lambda t: t.reshape(t_len, heads, HEAD_DIM)

    qkv = jax.nn.silu(_causal_conv(cut("gdn_qkv"), p["conv_w"][layer]))
    qa = _l2_norm(per_head(qkv[:, :mixer])) * HEAD_DIM ** -0.5
    ka = _l2_norm(per_head(qkv[:, mixer:2 * mixer]))
    va = qkv[:, 2 * mixer:]
    ab = cut("gdn_ab")
    to_chunks = lambda t: jnp.transpose(t).reshape(heads, t_len // CHUNK, CHUNK)
    beta = to_chunks(jax.nn.sigmoid(ab[:, heads:]))
    g = to_chunks(-jnp.exp(p["a_log"][layer]) * jax.nn.softplus(ab[:, :heads] + p["dt_bias"][layer]))
    oa = _gated_delta_rule(qa.reshape(t_len, mixer), ka.reshape(t_len, mixer), va, g, beta)
    oa = _rms_norm(per_head(oa), p["gdn_norm_g"][layer]) * jax.nn.silu(per_head(cut("gdn_gate")))
    branch_a = _linear(oa.reshape(t_len, mixer), p["w_out_a"][layer], "w_out_branch", "col")

    uv = jax.nn.gelu(cut("gmlp_uv"), approximate=False)
    u, vb = uv[:, :gmlp], uv[:, gmlp:]
    group_dim = gmlp // GMLP_GROUPS
    vb = _layer_norm(vb, p["gmlp_ln_g"][layer]).reshape(t_len // GMLP_BLOCK, GMLP_BLOCK, GMLP_GROUPS, group_dim)
    pos = jnp.arange(GMLP_BLOCK) // CHUNK
    ws = jnp.where((pos[None, :] <= pos[:, None])[None], p["w_spatial"][layer], 0.0)
    s = jnp.einsum("gts,nsgc->ntgc", ws.astype(BF16), vb.astype(BF16), preferred_element_type=F32)
    s = s + jnp.transpose(p["b_spatial"][layer])[None, :, :, None]
    branch_b = _linear(u * s.reshape(t_len, gmlp), p["w_out_b"][layer], "w_out_branch", "col")

    sba = cut("sba_qkv")
    qc = _rms_norm(per_head(sba[:, :mixer]), p["sba_q_g"][layer]).reshape(t_len, mixer)
    kc = _rms_norm(per_head(sba[:, mixer:2 * mixer]), p["sba_k_g"][layer]).reshape(t_len, mixer)
    oc = _stick_breaking(qc, kc, sba[:, 2 * mixer:])
    branch_c = _linear(oc, p["w_out_c"][layer], "w_out_branch", "col")

    gates = jax.nn.sigmoid(cut("gates"))
    y = (gates[:, :d_model] * branch_a + gates[:, d_model:2 * d_model] * branch_b
         + gates[:, 2 * d_model:] * branch_c)
    return _linear(y, p["w_out"][layer], "w_out", "row")


def _local_loss(x, p, target):
    depth = len(p["w_in"])
    for layer in range(depth):
        x = x + _mixer(_rms_norm(x, p["norm_mix_g"][layer]), p, layer)
        f = _linear(_rms_norm(x, p["norm_mlp_g"][layer]), p["w_ff1"][layer], "w_ff1", "col")
        x = x + _linear(jnp.square(jax.nn.relu(f)), p["w_ff2"][layer], "w_ff2", "row")
    err = jnp.square(x - target)
    return 0.5 * jnp.sum(jnp.mean(err, axis=-1))


ANY = pl.BlockSpec(memory_space=pl.ANY)


def _place():
    x, y, c = lax.axis_index("x"), lax.axis_index("y"), lax.axis_index("c")
    other_chips = [(1 - x, y), (x, 1 - y), (1 - x, 1 - y)]
    return x, y, c, other_chips


def _my_chip():
    return 2 * lax.axis_index("x") + lax.axis_index("y")


def _cast_to_slot(w, layer):
    _, rows, cols = w.shape
    tr = _row_tile(rows, cols, 16)

    def body(w_ref, o_ref):
        o_ref[...] = w_ref[...].astype(BF16)

    return pl.pallas_call(
        body,
        name="cast_to_slot",
        grid=(rows // tr,),
        in_specs=[pl.BlockSpec((None, tr, cols), lambda r: (layer, r, 0))],
        out_specs=pl.BlockSpec((None, tr, cols), lambda r: (_my_chip(), r, 0)),
        out_shape=jax.ShapeDtypeStruct((N_CHIPS, rows, cols), BF16),
        compiler_params=_params("parallel"),
    )(w)


def _remote(src, dst, sems, a, k, to):
    send_sems, recv_sems = sems
    return pltpu.make_async_remote_copy(src_ref=src, dst_ref=dst, send_sem=send_sems.at[a, k],
                                        recv_sem=recv_sems.at[a, k], device_id=to, device_id_type=MESH)


def _sem_pairs(n_arrays, n_copies):
    return [pltpu.SemaphoreType.DMA((n_arrays, n_copies)), pltpu.SemaphoreType.DMA((n_arrays, n_copies))]


def _gather_chips(bufs):
    n = len(bufs)

    def body(*refs):
        outs, sems = refs[n:2 * n], refs[2 * n:]
        x, y, c, _ = _place()
        x_nbr, y_nbr, diag, sibling = (1 - x, y), (x, 1 - y), (1 - x, 1 - y), (x, y, 1 - c)

        def piece(a, chip, core, quarter=None):
            half = bufs[a].shape[1] // 2
            rows = pl.ds(core * half, half) if quarter is None else pl.ds(core * half + quarter * (half // 2), half // 2)
            return outs[a].at[2 * chip[0] + chip[1], rows, :]

        def copy(a, k, ref, to):
            return _remote(ref, ref, sems, a, k, to)

        def sends(a):
            return [(0, piece(a, (x, y), c), (*x_nbr, c)),
                    (1, piece(a, (x, y), c), (*y_nbr, c)),
                    (2, piece(a, x_nbr, c, 0), (*y_nbr, c)),
                    (3, piece(a, y_nbr, c, 1), (*x_nbr, c)),
                    (4, piece(a, x_nbr, c), sibling),
                    (5, piece(a, y_nbr, c), sibling),
                    (6, piece(a, diag, c, 0), sibling),
                    (7, piece(a, diag, c, 1), sibling)]

        def lands(a):
            return [piece(a, x_nbr, c), piece(a, y_nbr, c), piece(a, diag, c, 0), piece(a, diag, c, 1),
                    piece(a, x_nbr, 1 - c), piece(a, y_nbr, 1 - c), piece(a, diag, 1 - c, 0), piece(a, diag, 1 - c, 1)]

        started = []

        def start(a, k):
            _, ref, to = sends(a)[k]
            started.append(copy(a, k, ref, to))
            started[-1].start()

        def arrived(a, k):
            copy(a, k, lands(a)[k], sibling).wait_recv()

        for a in range(n):
            start(a, 0)
            start(a, 1)
        for k, onward in ((0, (2, 4)), (1, (3, 5)), (2, (6,)), (3, (7,))):
            for a in range(n):
                arrived(a, k)
                for nxt in onward:
                    start(a, nxt)
        for a in range(n):
            for k in (4, 5, 6, 7):
                arrived(a, k)
        for cp in started:
            cp.wait_send()

    return pl.pallas_call(
        body,
        name="gather_chips",
        in_specs=[ANY] * n,
        out_specs=[ANY] * n,
        out_shape=[jax.ShapeDtypeStruct(b.shape, b.dtype) for b in bufs],
        input_output_aliases={a: a for a in range(n)},
        scratch_shapes=_sem_pairs(n, 8),
    )(*bufs)


def _pair_exchange(grads):
    n = len(grads)

    def body(*refs):
        ins, outs, sems = refs[:n], refs[n:2 * n], refs[2 * n:]
        x, y, c, _ = _place()
        copies = []
        for a in range(n):
            rows = grads[a].shape[1] // 2
            copies.append(_remote(ins[a].at[:, pl.ds((1 - c) * rows, rows), :], outs[a], sems, a, 0, (x, y, 1 - c)))
        for cp in copies:
            cp.start()
        for cp in copies:
            cp.wait()

    return pl.pallas_call(
        body,
        name="pair_exchange",
        in_specs=[ANY] * n,
        out_specs=[ANY] * n,
        out_shape=[jax.ShapeDtypeStruct((g.shape[0], g.shape[1] // 2, g.shape[2]), g.dtype) for g in grads],
        scratch_shapes=_sem_pairs(n, 1),
    )(*grads)


def _chip_exchange(parts):
    n = len(parts)

    def body(*refs):
        ins, outs, sems = refs[:n], refs[n:2 * n], refs[2 * n:]
        x, y, c, other_chips = _place()
        me = 2 * x + y
        copies = [_remote(ins[a].at[2 * chip[0] + chip[1]], outs[a].at[me], sems, a, j, (*chip, c))
                  for a in range(n) for j, chip in enumerate(other_chips)]
        for cp in copies:
            cp.start()
        for cp in copies:
            cp.wait()

    return pl.pallas_call(
        body,
        name="chip_exchange",
        in_specs=[ANY] * n,
        out_specs=[ANY] * n,
        out_shape=[jax.ShapeDtypeStruct(p.shape, p.dtype) for p in parts],
        scratch_shapes=_sem_pairs(n, 3),
    )(*parts)


def _pair_share(halves):
    n = len(halves)

    def body(*refs):
        outs, sems = refs[n:2 * n], refs[2 * n:]
        x, y, c, _ = _place()
        copies = []
        for a in range(n):
            rows = halves[a].shape[0] // 2
            mine = outs[a].at[pl.ds(c * rows, rows), :]
            copies.append(_remote(mine, mine, sems, a, 0, (x, y, 1 - c)))
        for cp in copies:
            cp.start()
        for cp in copies:
            cp.wait()

    return pl.pallas_call(
        body,
        name="pair_share",
        in_specs=[ANY] * n,
        out_specs=[ANY] * n,
        out_shape=[jax.ShapeDtypeStruct(h.shape, h.dtype) for h in halves],
        input_output_aliases={a: a for a in range(n)},
        scratch_shapes=_sem_pairs(n, 1),
    )(*halves)


def _all_gather_rows(block):
    m_per, n = block.shape

    def body(x_ref, out_ref, send_sems, recv_sems, local_sem):
        x, y, c, other_chips = _place()
        me, sibling = (x, y, c), (x, y, 1 - c)

        def rows(px, py, pc):
            return out_ref.at[pl.ds((4 * px + 2 * py + pc) * m_per, m_per), :]

        def copy(k, blk, to, src=None):
            return pltpu.make_async_remote_copy(src_ref=rows(*blk) if src is None else src, dst_ref=rows(*blk),
                                                send_sem=send_sems.at[k], recv_sem=recv_sems.at[k],
                                                device_id=to, device_id_type=MESH)

        mine = pltpu.make_async_copy(x_ref, rows(*me), local_sem)
        mine.start()
        first = [copy(0, me, sibling, src=x_ref)]
        first += [copy(1 + j, me, (*chip, c), src=x_ref) for j, chip in enumerate(other_chips)]
        for cp in first:
            cp.start()
        passed = [copy(4 + j, (*chip, c), sibling) for j, chip in enumerate(other_chips)]
        for j, chip in enumerate(other_chips):
            copy(1 + j, (*chip, c), me).wait_recv()
            passed[j].start()
        copy(0, sibling, me).wait_recv()
        for j, chip in enumerate(other_chips):
            copy(4 + j, (*chip, 1 - c), me).wait_recv()
        for cp in first + passed:
            cp.wait_send()
        mine.wait()

    return pl.pallas_call(
        body,
        name="all_gather_rows",
        in_specs=[ANY],
        out_specs=ANY,
        out_shape=jax.ShapeDtypeStruct((N_DEV * m_per, n), block.dtype),
        scratch_shapes=[pltpu.SemaphoreType.DMA((7,)), pltpu.SemaphoreType.DMA((7,)), pltpu.SemaphoreType.DMA],
    )(block)


def _row_tile(rows, cols, multiple):
    budget = max(multiple, ELEMENTWISE_BLOCK_ELEMS // _round_up(cols, LANES))
    t = multiple
    while t * 2 <= budget and rows % (t * 2) == 0:
        t *= 2
    return t if rows % t == 0 else rows


def _pair_sum(grad, theirs):
    chips, rows, cols = grad.shape
    half = rows // 2
    tr = _row_tile(half, cols, 16)
    blocks = half // tr

    def body(a_ref, b_ref, o_ref):
        o_ref[...] = (a_ref[...].astype(F32) + b_ref[...].astype(F32)).astype(o_ref.dtype)

    return pl.pallas_call(
        body,
        name="pair_sum",
        grid=(chips, blocks),
        in_specs=[pl.BlockSpec((None, tr, cols), lambda s, r: (s, lax.axis_index("c") * blocks + r, 0)),
                  pl.BlockSpec((None, tr, cols), lambda s, r: (s, r, 0))],
        out_specs=pl.BlockSpec((None, tr, cols), lambda s, r: (s, r, 0)),
        out_shape=jax.ShapeDtypeStruct((chips, half, cols), grad.dtype),
        compiler_params=_params("parallel", "parallel"),
    )(grad, theirs)


def _chip_sum(mine, others):
    chips, half, cols = mine.shape
    tr = _row_tile(half, cols, 16)
    blocks = half // tr

    def body(own_ref, *refs):
        o_ref = refs[-1]
        total = own_ref[...].astype(F32)
        for ref in refs[:-1]:
            total = total + ref[...].astype(F32)
        o_ref[...] = total

    slot = lambda q: pl.BlockSpec((None, tr, cols), lambda r: ((_my_chip() + q) % chips, r, 0))
    return pl.pallas_call(
        body,
        name="chip_sum",
        grid=(blocks,),
        in_specs=[slot(q) for q in range(chips)],
        out_specs=pl.BlockSpec((tr, cols), lambda r: (lax.axis_index("c") * blocks + r, 0)),
        out_shape=jax.ShapeDtypeStruct((2 * half, cols), F32),
        compiler_params=_params("parallel"),
    )(mine, *([others] * (chips - 1)))


def _device_sum(blocks, m_per):
    tr = _row_tile(m_per, LANES, 8)
    per = m_per // tr

    def body(*refs):
        o_ref = refs[-1]
        total = refs[0][...]
        for ref in refs[1:-1]:
            total = total + ref[...]
        o_ref[...] = total

    return pl.pallas_call(
        body,
        name="device_sum",
        grid=(per,),
        in_specs=[pl.BlockSpec((tr, LANES), functools.partial(lambda d, r: (d * per + r, 0), d)) for d in range(N_DEV)],
        out_specs=pl.BlockSpec((tr, LANES), lambda r: (r, 0)),
        out_shape=jax.ShapeDtypeStruct((m_per, LANES), F32),
        compiler_params=_params("parallel"),
    )(*([blocks] * N_DEV))


def _adamw(grads, w, m, v):
    depth, rows, cols = w.shape
    tr = _row_tile(rows, cols, 8) if rows % 8 == 0 else rows
    spec = pl.BlockSpec((None, tr, cols), lambda l, r: (l, r, 0))
    grad_spec = lambda q: pl.BlockSpec((tr, cols), lambda l, r: (jnp.where(l == q, r, 0), 0))

    def body(*refs):
        g_refs, (w_ref, m_ref, v_ref, go_ref, d_ref, mo_ref, vo_ref) = refs[:depth], refs[depth:]
        layer = pl.program_id(0)
        g = g_refs[0][...]
        for q in range(1, depth):
            g = jnp.where(layer == q, g_refs[q][...], g)
        m_new = ADAM_B1 * m_ref[...] + (1.0 - ADAM_B1) * g
        v_new = ADAM_B2 * v_ref[...] + (1.0 - ADAM_B2) * jnp.square(g)
        m_hat = m_new / (1.0 - ADAM_B1 ** ADAM_STEP)
        v_hat = v_new / (1.0 - ADAM_B2 ** ADAM_STEP)
        go_ref[...] = g
        d_ref[...] = -ADAM_LR * (m_hat / (jnp.sqrt(v_hat) + ADAM_EPS) + ADAM_WD * w_ref[...])
        mo_ref[...] = m_new
        vo_ref[...] = v_new

    return pl.pallas_call(
        body,
        name="adamw",
        grid=(depth, rows // tr),
        in_specs=[grad_spec(q) for q in range(depth)] + [spec] * 3,
        out_specs=[spec] * 4,
        out_shape=[jax.ShapeDtypeStruct(w.shape, F32)] * 4,
        compiler_params=_params("parallel", "parallel"),
    )(*grads, w, m, v)


def _join_cols(gathered):
    return jnp.concatenate([gathered[s] for s in range(N_CHIPS)], axis=-1)


def _split_cols(full):
    rows, cols = full.shape
    return jnp.transpose(full.reshape(rows, N_CHIPS, cols // N_CHIPS), (1, 0, 2))


def _reduce_scatter(grads):
    theirs = _pair_exchange(grads)
    pairs = [_pair_sum(g, t) for g, t in zip(grads, theirs)]
    others = _chip_exchange(pairs)
    return _pair_share([_chip_sum(p, o) for p, o in zip(pairs, others)])


def _pack_rows(arrays):
    parts, counts = [], []
    for a in arrays:
        flat = a.reshape(-1).astype(F32)
        n_rows = _round_up(flat.shape[0], 8 * LANES) // LANES
        parts.append(jnp.pad(flat, (0, n_rows * LANES - flat.shape[0])).reshape(n_rows, LANES))
        counts.append(n_rows)
    return jnp.concatenate(parts, axis=0), counts


def _pad_rows(a, rows):
    return jnp.pad(a, ((0, rows - a.shape[0]), (0, 0)))


def _unpack_rows(packed, counts, shapes):
    out, row = [], 0
    for n_rows, shape in zip(counts, shapes):
        size = 1
        for d in shape:
            size *= d
        out.append(packed[row:row + n_rows].reshape(-1)[:size].reshape(shape))
        row += n_rows
    return out


SHARDED = (("w_in", "col"), ("w_out_a", "col"), ("w_out_b", "col"), ("w_out_c", "col"), ("w_out", "row"),
           ("w_ff1", "col"), ("w_ff2", "row"))


def kernel(x, w_in, conv_w, a_log, dt_bias, gdn_norm_g, gmlp_ln_g, w_spatial, b_spatial, sba_q_g, sba_k_g, w_out_a, w_out_b, w_out_c, w_out, norm_mix_g, norm_mlp_g, w_ff1, w_ff2, loss_target, m_w_in, m_conv_w, m_a_log, m_dt_bias, m_gdn_norm_g, m_gmlp_ln_g, m_w_spatial, m_b_spatial, m_sba_q_g, m_sba_k_g, m_w_out_a, m_w_out_b, m_w_out_c, m_w_out, m_norm_mix_g, m_norm_mlp_g, m_w_ff1, m_w_ff2, v_w_in, v_conv_w, v_a_log, v_dt_bias, v_gdn_norm_g, v_gmlp_ln_g, v_w_spatial, v_b_spatial, v_sba_q_g, v_sba_k_g, v_w_out_a, v_w_out_b, v_w_out_c, v_w_out, v_norm_mix_g, v_norm_mlp_g, v_w_ff1, v_w_ff2):
    given = dict(locals())
    weights = {n: given[n] for n in WEIGHT_NAMES}
    depth, d_model = w_in.shape[0], w_in.shape[1]
    layers = range(depth)
    chip = _my_chip()

    gathered = [_gather_chips([_cast_to_slot(weights[name], l) for name, _ in SHARDED]) for l in layers]
    conv_rows = depth * CONV_WIDTH
    conv_cols = conv_w.shape[-1]
    conv_all = _all_gather_rows(_pad_rows(conv_w.reshape(conv_rows, conv_cols), _round_up(conv_rows, 8)))
    conv_all = conv_all.reshape(N_CHIPS, 2, _round_up(conv_rows, 8), conv_cols)[:, 0, :conv_rows]
    conv_full = jnp.concatenate([conv_all[s] for s in range(N_CHIPS)], axis=-1).reshape(depth, CONV_WIDTH, N_CHIPS * conv_cols)

    params = {n: weights[n] for n in REPLICATED_NAMES}
    params["conv_w"] = conv_full
    for i, (name, _) in enumerate(SHARDED):
        params[name] = [gathered[l][i] for l in layers]
    params["w_in"] = [_pack_w_in(_join_cols(gathered[l][0]), d_model) for l in layers]

    loss_local, (grad_x, grads) = jax.value_and_grad(_local_loss, argnums=(0, 1))(x[0], params, loss_target[0])

    summed = []
    for l in layers:
        local = [grads[name][l] for name, _ in SHARDED]
        local[0] = _split_cols(_unpack_w_in(local[0], d_model))
        summed.append(_reduce_scatter(local))
    rep_packed, rep_counts = _pack_rows([grads[n] for n in REPLICATED_NAMES])
    rep_rows = _round_up(rep_packed.shape[0], SMALL_ROW_ALIGN)
    rest_packed, rest_counts = _pack_rows([grads["conv_w"], loss_local])
    m_per = rep_rows + _round_up(rest_packed.shape[0], SMALL_ROW_ALIGN)
    packed = jnp.concatenate([_pad_rows(rep_packed, rep_rows), _pad_rows(rest_packed, m_per - rep_rows)], axis=0)
    total = _device_sum(_all_gather_rows(packed), m_per)
    conv_sum, loss = _unpack_rows(total[rep_rows:], rest_counts, [grads["conv_w"].shape, ()])
    conv_grad = lax.dynamic_slice_in_dim(conv_sum, chip * conv_cols, conv_cols, axis=2)

    out = {}
    for i, (name, _) in enumerate(SHARDED):
        out[name] = _adamw([summed[l][i] for l in layers], weights[name], given["m_" + name], given["v_" + name])
    out["conv_w"] = _adamw([conv_grad[l] for l in layers], conv_w, m_conv_w, v_conv_w)
    rep_shapes = [weights[n].shape for n in REPLICATED_NAMES]
    pack3 = lambda prefix: _pad_rows(_pack_rows([given[prefix + n] for n in REPLICATED_NAMES])[0], rep_rows)[None]
    rep = _adamw([total[:rep_rows]], pack3(""), pack3("m_"), pack3("v_"))
    rep = [_unpack_rows(t[0], rep_counts, rep_shapes) for t in rep]
    for i, name in enumerate(REPLICATED_NAMES):
        out[name] = tuple(t[i] for t in rep)

    results = [loss, grad_x[None]]
    for kind in range(4):
        results += [out[n][kind] for n in WEIGHT_NAMES]
    return tuple(results)
```

```python
import functools

import jax
import jax.numpy as jnp
from jax import lax
from jax.experimental import pallas as pl
from jax.experimental.pallas import tpu as pltpu

F32, BF16 = jnp.float32, jnp.bfloat16
MESH = pl.DeviceIdType.MESH
N_CHIPS = 4
N_DEV = 8

EPS = 1e-6
CHUNK = 64
HEAD_DIM = 128
CONV_WIDTH = 4
GMLP_GROUPS = 8
GMLP_BLOCK = 128
N_BRANCHES = 3
ADAM_LR, ADAM_B1, ADAM_B2, ADAM_EPS, ADAM_WD, ADAM_STEP = 0.001, 0.9, 0.999, 1e-08, 0.01, 10

V7X_VMEM_BYTES = 64 * 1024 * 1024
VMEM_LIMIT = V7X_VMEM_BYTES - 8 * 1024 * 1024
LANES = 128
MATMUL_TILE = 1024
MATMUL_K_TILE = 2048
ATTN_TILE = 256
GDN_HEADS_PER_STEP = 8
SBA_HEADS_PER_STEP = 4
GDN_SEGMENT = 512
ELEMENTWISE_BLOCK_ELEMS = 512 * 1024
SMALL_ROW_ALIGN = 256

WEIGHT_NAMES = ("w_in", "conv_w", "a_log", "dt_bias", "gdn_norm_g", "gmlp_ln_g", "w_spatial", "b_spatial",
                "sba_q_g", "sba_k_g", "w_out_a", "w_out_b", "w_out_c", "w_out", "norm_mix_g", "norm_mlp_g",
                "w_ff1", "w_ff2")
REPLICATED_NAMES = ("a_log", "dt_bias", "gdn_norm_g", "gmlp_ln_g", "w_spatial", "b_spatial", "sba_q_g",
                    "sba_k_g", "norm_mix_g", "norm_mlp_g")


def _round_up(n, m):
    return (n + m - 1) // m * m


def _tile(dim, pref):
    if dim <= pref:
        return dim
    t = pref // LANES * LANES
    while t >= LANES:
        if dim % t == 0:
            return t
        t -= LANES
    return dim


def _params(*semantics):
    return pltpu.CompilerParams(dimension_semantics=semantics, vmem_limit_bytes=VMEM_LIMIT)


_DOT_DIMS = {"nn": ((1,), (0,)), "nt": ((1,), (1,)), "tn": ((0,), (0,))}


def _logical_shape(shape, kind):
    if kind is None:
        return shape
    chips, rows, cols = shape
    return (rows, chips * cols) if kind == "col" else (chips * rows, cols)


def _weight_tiles(shape, kind, row_pref, col_pref):
    rows, cols = _logical_shape(shape, kind)
    tr = _tile(shape[1] if kind == "row" else rows, row_pref)
    tc = _tile(shape[2] if kind == "col" else cols, col_pref)
    return tr, tc


def _weight_spec(shape, kind, tr, tc, pick):
    if kind is None:
        return pl.BlockSpec((tr, tc), pick)
    if kind == "col":
        per = shape[2] // tc
        return pl.BlockSpec((None, tr, tc), lambda i, j, k: (pick(i, j, k)[1] // per, pick(i, j, k)[0], pick(i, j, k)[1] % per))
    per = shape[1] // tr
    return pl.BlockSpec((None, tr, tc), lambda i, j, k: (pick(i, j, k)[0] // per, pick(i, j, k)[0] % per, pick(i, j, k)[1]))


def _matmul(a, b, mode, out_dtype, name, kind=None, out_shape=None):
    if mode == "tn":
        k_dim, m_dim = a.shape
        n_dim = b.shape[1]
        out_shape = (m_dim, n_dim) if kind is None else out_shape
        tm, tn = _weight_tiles(out_shape, kind, MATMUL_TILE, MATMUL_TILE)
        tk = _tile(k_dim, MATMUL_K_TILE)
        a_spec = pl.BlockSpec((tk, tm), lambda i, j, k: (k, i))
        b_spec = pl.BlockSpec((tk, tn), lambda i, j, k: (k, j))
        out_spec = _weight_spec(out_shape, kind, tm, tn, lambda i, j, k: (i, j))
    else:
        m_dim, k_dim = a.shape
        tm = _tile(m_dim, MATMUL_TILE)
        if mode == "nn":
            _, n_dim = _logical_shape(b.shape, kind)
            tk, tn = _weight_tiles(b.shape, kind, MATMUL_K_TILE, MATMUL_TILE)
            b_spec = _weight_spec(b.shape, kind, tk, tn, lambda i, j, k: (k, j))
        else:
            n_dim, _ = _logical_shape(b.shape, kind)
            tn, tk = _weight_tiles(b.shape, kind, MATMUL_TILE, MATMUL_K_TILE)
            b_spec = _weight_spec(b.shape, kind, tn, tk, lambda i, j, k: (j, k))
        a_spec = pl.BlockSpec((tm, tk), lambda i, j, k: (i, k))
        out_shape = (m_dim, n_dim)
        out_spec = pl.BlockSpec((tm, tn), lambda i, j, k: (i, j))
    nk = k_dim // tk
    dims = (_DOT_DIMS[mode], ((), ()))

    def body(a_ref, b_ref, o_ref, *scratch):
        part = lax.dot_general(a_ref[...], b_ref[...], dims, preferred_element_type=F32)
        if nk == 1:
            o_ref[...] = part.astype(out_dtype)
            return
        acc_ref, = scratch
        k = pl.program_id(2)

        @pl.when(k == 0)
        def _():
            acc_ref[...] = part

        @pl.when(k > 0)
        def _():
            acc_ref[...] += part

        @pl.when(k == nk - 1)
        def _():
            o_ref[...] = acc_ref[...].astype(out_dtype)

    return pl.pallas_call(
        body,
        name=name,
        grid=(m_dim // tm, n_dim // tn, nk),
        in_specs=[a_spec, b_spec],
        out_specs=out_spec,
        out_shape=jax.ShapeDtypeStruct(out_shape, out_dtype),
        scratch_shapes=[] if nk == 1 else [pltpu.VMEM((tm, tn), F32)],
        compiler_params=_params("parallel", "parallel", "arbitrary"),
    )(a, b)


@functools.partial(jax.custom_vjp, nondiff_argnums=(2, 3))
def _linear(a, w, name, kind=None):
    return _matmul(a.astype(BF16), w, "nn", F32, name + "_fwd", kind)


def _linear_fwd(a, w, name, kind):
    a16 = a.astype(BF16)
    return _matmul(a16, w, "nn", F32, name + "_fwd", kind), (a16, w)


def _linear_bwd(name, kind, res, g):
    a16, w = res
    g16 = g.astype(BF16)
    da = _matmul(g16, w, "nt", F32, name + "_dgrad", kind)
    dw = _matmul(a16, g16, "tn", BF16, name + "_wgrad", kind, w.shape)
    return da, dw


_linear.defvjp(_linear_fwd, _linear_bwd)


def _split3_dots(xs, ones_mat):
    his = [x.astype(BF16) for x in xs]
    rests = [x - hi.astype(F32) for x, hi in zip(xs, his)]
    mids = [rest.astype(BF16) for rest in rests]
    los = [(rest - mid.astype(F32)).astype(BF16) for rest, mid in zip(rests, mids)]
    dots = lambda parts: [jnp.dot(p, ones_mat, preferred_element_type=F32) for p in parts]
    return [a + b + c for a, b, c in zip(dots(his), dots(mids), dots(los))]


def _dot_nt(a, b):
    return lax.dot_general(a, b, (((1,), (1,)), ((), ())), preferred_element_type=F32)


def _dot_tn(a, b):
    return lax.dot_general(a, b, (((0,), (0,)), ((), ())), preferred_element_type=F32)


def _sba_scores(q16, k16, tile, scale, diagonal):
    z = _dot_nt(q16, k16) * scale
    log_sig = jnp.minimum(z, 0.0) - jnp.log(1.0 + jnp.exp(-jnp.abs(z)))
    if not diagonal:
        return None, log_sig, log_sig - z
    mask = lax.broadcasted_iota(jnp.int32, (tile, tile), 1) < lax.broadcasted_iota(jnp.int32, (tile, tile), 0)
    return mask, log_sig, jnp.where(mask, log_sig - z, 0.0)


def _masked(mask, value):
    return value if mask is None else jnp.where(mask, value, 0.0)


def _sba_layout(q):
    t_len, width = q.shape
    heads = width // HEAD_DIM
    group = min(SBA_HEADS_PER_STEP, heads)
    tile = min(ATTN_TILE, t_len)
    q_spec = pl.BlockSpec((tile, group * HEAD_DIM), lambda h, i: (i, h))
    kv_spec = pl.BlockSpec((t_len, group * HEAD_DIM), lambda h, i: (0, h))
    total_spec = pl.BlockSpec((group, tile, 1), lambda h, i: (h, i, 0))
    return heads, group, tile, (heads // group, t_len // tile), q_spec, kv_spec, total_spec


def _sba_forward(q, k, v):
    t_len, width = q.shape
    heads, group, tile, grid, q_spec, kv_spec, total_spec = _sba_layout(q)
    scale = HEAD_DIM ** -0.5
    cols = [slice(h * HEAD_DIM, (h + 1) * HEAD_DIM) for h in range(group)]

    def body(q_ref, k_ref, v_ref, o_ref, r_ref):
        i = pl.program_id(1)
        q16 = [q_ref[:, c].astype(BF16) for c in cols]
        ri = lax.broadcasted_iota(jnp.int32, (tile, tile), 0)
        ci = lax.broadcasted_iota(jnp.int32, (tile, tile), 1)
        later = (ri > ci).astype(BF16)

        def tile_step(j, carry, diagonal):
            runs, accs = carry
            rows = pl.ds(pl.multiple_of(j * tile, tile), tile)
            k16 = [k_ref[rows, c].astype(BF16) for c in cols]
            v16 = [v_ref[rows, c].astype(BF16) for c in cols]
            scores = [_sba_scores(q16[h], k16[h], tile, scale, diagonal) for h in range(group)]
            within = _split3_dots([log_keep for _, _, log_keep in scores], later)
            atts = [_masked(scores[h][0], jnp.exp(scores[h][1] + within[h] + runs[h])) for h in range(group)]
            accs = [accs[h] + jnp.dot(atts[h].astype(BF16), v16[h], preferred_element_type=F32) for h in range(group)]
            runs = [runs[h] + jnp.sum(scores[h][2], axis=1, keepdims=True) for h in range(group)]
            return runs, accs

        zeros = lambda width: [jnp.zeros((tile, width), F32) for _ in range(group)]
        carry = tile_step(i, (zeros(1), zeros(HEAD_DIM)), True)
        runs, accs = lax.fori_loop(0, i, lambda jj, carry: tile_step(i - 1 - jj, carry, False), carry)
        for h in range(group):
            o_ref[:, cols[h]] = accs[h]
            r_ref[h] = runs[h]

    return pl.pallas_call(
        body,
        name="sba_fwd",
        grid=grid,
        in_specs=[q_spec, kv_spec, kv_spec],
        out_specs=[q_spec, total_spec],
        out_shape=[jax.ShapeDtypeStruct((t_len, width), F32), jax.ShapeDtypeStruct((heads, t_len, 1), F32)],
        compiler_params=_params("parallel", "arbitrary"),
    )(q, k, v)


def _sba_backward(q, k, v, total, do):
    t_len, width = q.shape
    heads, group, tile, grid, q_spec, kv_spec, total_spec = _sba_layout(q)
    scale = HEAD_DIM ** -0.5
    cols = [slice(h * HEAD_DIM, (h + 1) * HEAD_DIM) for h in range(group)]

    def body(q_ref, k_ref, v_ref, r_ref, do_ref, dq_ref, dk_ref, dv_ref):
        i = pl.program_id(1)

        @pl.when(i == 0)
        def _():
            dk_ref[...] = jnp.zeros_like(dk_ref)
            dv_ref[...] = jnp.zeros_like(dv_ref)

        q16 = [q_ref[:, c].astype(BF16) for c in cols]
        do16 = [do_ref[:, c].astype(BF16) for c in cols]
        tot = [r_ref[h] for h in range(group)]
        ri = lax.broadcasted_iota(jnp.int32, (tile, tile), 0)
        ci = lax.broadcasted_iota(jnp.int32, (tile, tile), 1)
        upto = (ri <= ci).astype(BF16)
        before = (ri < ci).astype(BF16)

        def tile_step(j, carry, diagonal):
            keep_left, w_left, dq = carry
            rows = pl.ds(pl.multiple_of(j * tile, tile), tile)
            heads_here = range(group)
            k16 = [k_ref[rows, c].astype(BF16) for c in cols]
            v16 = [v_ref[rows, c].astype(BF16) for c in cols]
            scores = [_sba_scores(q16[h], k16[h], tile, scale, diagonal) for h in heads_here]
            within = _split3_dots([log_keep for _, _, log_keep in scores], upto)
            atts = [_masked(scores[h][0], jnp.exp(scores[h][1] + tot[h] - keep_left[h] - within[h])) for h in heads_here]
            d_att = [_dot_nt(do16[h], v16[h]) for h in heads_here]
            w = [atts[h] * d_att[h] for h in heads_here]
            w_before = _split3_dots(w, before)
            sig = [jnp.exp(scores[h][1]) for h in heads_here]
            dz16 = [(_masked(scores[h][0], w[h] * (1.0 - sig[h]) - sig[h] * (w_left[h] + w_before[h])) * scale).astype(BF16)
                    for h in heads_here]
            dq = [dq[h] + jnp.dot(dz16[h], k16[h], preferred_element_type=F32) for h in heads_here]
            dk = [_dot_tn(dz16[h], q16[h]) for h in heads_here]
            dv = [_dot_tn(atts[h].astype(BF16), do16[h]) for h in heads_here]
            for h in heads_here:
                dk_ref[rows, cols[h]] += dk[h]
                dv_ref[rows, cols[h]] += dv[h]
            keep_left = [keep_left[h] + jnp.sum(scores[h][2], axis=1, keepdims=True) for h in heads_here]
            w_left = [w_left[h] + jnp.sum(w[h], axis=1, keepdims=True) for h in heads_here]
            return keep_left, w_left, dq

        zeros = lambda width: [jnp.zeros((tile, width), F32) for _ in range(group)]
        carry = lax.fori_loop(0, i, lambda j, carry: tile_step(j, carry, False), (zeros(1), zeros(1), zeros(HEAD_DIM)))
        _, _, dq = tile_step(i, carry, True)
        for h in range(group):
            dq_ref[:, cols[h]] = dq[h]

    return pl.pallas_call(
        body,
        name="sba_bwd",
        grid=grid,
        in_specs=[q_spec, kv_spec, kv_spec, total_spec, q_spec],
        out_specs=[q_spec, kv_spec, kv_spec],
        out_shape=[jax.ShapeDtypeStruct((t_len, width), F32)] * 3,
        compiler_params=_params("parallel", "arbitrary"),
    )(q, k, v, total, do)


@jax.custom_vjp
def _stick_breaking(q, k, v):
    return _sba_forward(q, k, v)[0]


def _stick_breaking_fwd(q, k, v):
    o, total = _sba_forward(q, k, v)
    return o, (q, k, v, total)


def _stick_breaking_bwd(res, do):
    q, k, v, total = res
    return tuple(_sba_backward(q, k, v, total, do))


_stick_breaking.defvjp(_stick_breaking_fwd, _stick_breaking_bwd)


def _contract(a16, b16, ca, cb):
    return lax.dot_general(a16, b16, (((ca,), (cb,)), ((), ())), preferred_element_type=F32)


def _pdot_raw(a, b, ca, cb):
    a_hi = a.astype(BF16)
    a_lo = (a - a_hi.astype(F32)).astype(BF16)
    b_hi = b.astype(BF16)
    b_lo = (b - b_hi.astype(F32)).astype(BF16)
    return _contract(a_hi, b_hi, ca, cb) + _contract(a_hi, b_lo, ca, cb) + _contract(a_lo, b_hi, ca, cb)


def _bdot_raw(a, b, ca, cb):
    return _contract(a.astype(BF16), b.astype(BF16), ca, cb)


def _make_dot(raw):
    @functools.partial(jax.custom_vjp, nondiff_argnums=(2, 3))
    def dot(a, b, ca, cb):
        return raw(a, b, ca, cb)

    def fwd(a, b, ca, cb):
        return raw(a, b, ca, cb), (a, b)

    def bwd(ca, cb, res, g):
        a, b = res
        da = raw(g, b, 1, 1 - cb) if ca == 1 else raw(b, g, 1 - cb, 1)
        db = raw(a, g, 1 - ca, 0) if cb == 0 else raw(g, a, 0, 1 - ca)
        return da, db

    dot.defvjp(fwd, bwd)
    return dot


_bdot = _make_dot(_bdot_raw)


@jax.custom_vjp
def _unit_lower_inverses(lows):
    size = lows[0].shape[0]
    eye = (lax.broadcasted_iota(jnp.int32, (size, size), 0) == lax.broadcasted_iota(jnp.int32, (size, size), 1)).astype(F32)
    invs = [eye - low for low in lows]
    powers = list(lows)
    span = 2
    while span < size:
        powers = [_bdot_raw(p, p, 1, 0) for p in powers]
        invs = [inv + _bdot_raw(inv, p, 1, 0) for inv, p in zip(invs, powers)]
        span *= 2
    resids = [eye - inv - _pdot_raw(low, inv, 1, 0) for low, inv in zip(lows, invs)]
    return [inv + _bdot_raw(inv, r, 1, 0) for inv, r in zip(invs, resids)]


def _unit_lower_inverses_fwd(lows):
    invs = _unit_lower_inverses(lows)
    return invs, invs


def _unit_lower_inverses_bwd(invs, gs):
    left = [_bdot_raw(inv, g, 0, 0) for inv, g in zip(invs, gs)]
    return ([-_bdot_raw(l, inv, 1, 1) for l, inv in zip(left, invs)],)


_unit_lower_inverses.defvjp(_unit_lower_inverses_fwd, _unit_lower_inverses_bwd)


def _gdn_chunks(states, qs, ks, vs, g_rows, b_rows):
    heads = range(len(qs))
    size = qs[0].shape[0]
    ri = lax.broadcasted_iota(jnp.int32, (size, size), 0)
    ci = lax.broadcasted_iota(jnp.int32, (size, size), 1)
    eye, incl, strict = ri == ci, ci <= ri, ci < ri
    square = lambda t: jnp.broadcast_to(t, (size, size))
    g_rowb = [square(g) for g in g_rows]
    g_col = [jnp.sum(jnp.where(eye, g, 0.0), axis=1, keepdims=True) for g in g_rowb]
    b_col = [jnp.sum(jnp.where(eye, square(b), 0.0), axis=1, keepdims=True) for b in b_rows]
    gc_col = [jnp.sum(jnp.where(incl, g, 0.0), axis=1, keepdims=True) for g in g_rowb]
    gc_row = [jnp.sum(jnp.where(ri <= ci, square(g), 0.0), axis=0, keepdims=True) for g in g_col]
    g_last = [jnp.sum(g, axis=1, keepdims=True) for g in g_rows]
    decay = [jnp.where(incl, jnp.exp(jnp.where(incl, gc_col[h] - gc_row[h], 0.0)), 0.0) for h in heads]
    kb = [ks[h] * b_col[h] for h in heads]
    kk = [_bdot(kb[h], ks[h], 1, 1) for h in heads]
    invs = _unit_lower_inverses([jnp.where(strict, kk[h] * decay[h], 0.0) for h in heads])
    grow = [jnp.exp(gc_col[h]) for h in heads]
    u = [_bdot(invs[h], vs[h] * b_col[h], 1, 0) for h in heads]
    w = [_bdot(invs[h], kb[h] * grow[h], 1, 0) for h in heads]
    qk = [_bdot(qs[h], ks[h], 1, 1) for h in heads]
    ws = [_bdot(w[h], states[h], 1, 0) for h in heads]
    qs_state = [_bdot(qs[h] * grow[h], states[h], 1, 0) for h in heads]
    v_new = [u[h] - ws[h] for h in heads]
    inner = [_bdot(qk[h] * decay[h], v_new[h], 1, 0) for h in heads]
    k_dec = [ks[h] * jnp.exp(g_last[h] - gc_col[h]) for h in heads]
    kv = [_bdot(k_dec[h], v_new[h], 0, 0) for h in heads]
    new_states = [states[h] * jnp.exp(g_last[h]) + kv[h] for h in heads]
    return new_states, [qs_state[h] + inner[h] for h in heads]


def _gdn_layout(q):
    t_len, width = q.shape
    heads = width // HEAD_DIM
    group = min(GDN_HEADS_PER_STEP, heads)
    seg = min(GDN_SEGMENT, t_len)
    return heads, group, seg, (heads // group, t_len // seg)


def _gdn_specs(group, seg, order):
    chunks = seg // CHUNK
    seq = pl.BlockSpec((seg, group * HEAD_DIM), lambda h, t: (order(t), h))
    gate = pl.BlockSpec((group, chunks, CHUNK), lambda h, t: (h, order(t), 0))
    states = pl.BlockSpec((group, chunks, HEAD_DIM, HEAD_DIM), lambda h, t: (h, order(t), 0, 0))
    return seq, gate, states


def _gdn_forward(q, k, v, g, beta):
    t_len, width = q.shape
    heads, group, seg, grid = _gdn_layout(q)
    seq, gate, states = _gdn_specs(group, seg, lambda t: t)

    def body(q_ref, k_ref, v_ref, g_ref, b_ref, o_ref, s_ref, state_ref):
        @pl.when(pl.program_id(1) == 0)
        def _():
            state_ref[...] = jnp.zeros_like(state_ref)

        def step(n, carry):
            rows = pl.ds(pl.multiple_of(n * CHUNK, CHUNK), CHUNK)
            cols = [slice(j * HEAD_DIM, (j + 1) * HEAD_DIM) for j in range(group)]
            states = [state_ref[j] for j in range(group)]
            for j in range(group):
                s_ref[j, n] = states[j]
            states, outs = _gdn_chunks(states, [q_ref[rows, c] for c in cols], [k_ref[rows, c] for c in cols],
                                       [v_ref[rows, c] for c in cols], [g_ref[j, pl.ds(n, 1), :] for j in range(group)],
                                       [b_ref[j, pl.ds(n, 1), :] for j in range(group)])
            for j in range(group):
                o_ref[rows, cols[j]] = outs[j]
                state_ref[j] = states[j]
            return carry

        lax.fori_loop(0, seg // CHUNK, step, 0)

    return pl.pallas_call(
        body,
        name="gdn_fwd",
        grid=grid,
        in_specs=[seq, seq, seq, gate, gate],
        out_specs=[seq, states],
        out_shape=[jax.ShapeDtypeStruct((t_len, width), F32),
                   jax.ShapeDtypeStruct((heads, t_len // CHUNK, HEAD_DIM, HEAD_DIM), F32)],
        scratch_shapes=[pltpu.VMEM((group, HEAD_DIM, HEAD_DIM), F32)],
        compiler_params=_params("parallel", "arbitrary"),
    )(q, k, v, g, beta)


def _gdn_backward(q, k, v, g, beta, starts, do):
    t_len, width = q.shape
    heads, group, seg, grid = _gdn_layout(q)
    last = grid[1] - 1
    seq, gate, states = _gdn_specs(group, seg, lambda t: last - t)
    chunks = seg // CHUNK

    def body(q_ref, k_ref, v_ref, g_ref, b_ref, s_ref, do_ref, dq_ref, dk_ref, dv_ref, dg_ref, db_ref, d_state_ref):
        @pl.when(pl.program_id(1) == 0)
        def _():
            d_state_ref[...] = jnp.zeros_like(d_state_ref)

        def step(m, carry):
            n = chunks - 1 - m
            rows = pl.ds(pl.multiple_of(n * CHUNK, CHUNK), CHUNK)
            cols = [slice(j * HEAD_DIM, (j + 1) * HEAD_DIM) for j in range(group)]
            per_head = lambda ref: [ref[rows, c] for c in cols]
            _, pull = jax.vjp(_gdn_chunks, [s_ref[j, n] for j in range(group)], per_head(q_ref), per_head(k_ref),
                              per_head(v_ref), [g_ref[j, pl.ds(n, 1), :] for j in range(group)],
                              [b_ref[j, pl.ds(n, 1), :] for j in range(group)])
            d_states, dq, dk, dv, dg, db = pull(([d_state_ref[j] for j in range(group)], per_head(do_ref)))
            for j in range(group):
                dq_ref[rows, cols[j]] = dq[j]
                dk_ref[rows, cols[j]] = dk[j]
                dv_ref[rows, cols[j]] = dv[j]
                dg_ref[j, pl.ds(n, 1), :] = dg[j]
                db_ref[j, pl.ds(n, 1), :] = db[j]
                d_state_ref[j] = d_states[j]
            return carry

        lax.fori_loop(0, chunks, step, 0)

    return pl.pallas_call(
        body,
        name="gdn_bwd",
        grid=grid,
        in_specs=[seq, seq, seq, gate, gate, states, seq],
        out_specs=[seq, seq, seq, gate, gate],
        out_shape=[jax.ShapeDtypeStruct((t_len, width), F32)] * 3
        + [jax.ShapeDtypeStruct((heads, t_len // CHUNK, CHUNK), F32)] * 2,
        scratch_shapes=[pltpu.VMEM((group, HEAD_DIM, HEAD_DIM), F32)],
        compiler_params=_params("parallel", "arbitrary"),
    )(q, k, v, g, beta, starts, do)


@jax.custom_vjp
def _gated_delta_rule(q, k, v, g, beta):
    return _gdn_forward(q, k, v, g, beta)[0]


def _gated_delta_rule_fwd(q, k, v, g, beta):
    o, starts = _gdn_forward(q, k, v, g, beta)
    return o, (q, k, v, g, beta, starts)


def _gated_delta_rule_bwd(res, do):
    return tuple(_gdn_backward(*res, do))


_gated_delta_rule.defvjp(_gated_delta_rule_fwd, _gated_delta_rule_bwd)


def _rms_norm(x, gain):
    return x * lax.rsqrt(jnp.mean(x * x, axis=-1, keepdims=True) + EPS) * gain


def _layer_norm(x, gain):
    xc = x - jnp.mean(x, axis=-1, keepdims=True)
    return xc * lax.rsqrt(jnp.mean(xc * xc, axis=-1, keepdims=True) + EPS) * gain


def _l2_norm(x):
    return x * lax.rsqrt(jnp.sum(x * x, axis=-1, keepdims=True) + EPS)


def _causal_conv(x, w):
    t_len = x.shape[0]
    xp = jnp.pad(x, ((CONV_WIDTH - 1, 0), (0, 0)))
    return sum(xp[i:i + t_len] * w[i] for i in range(CONV_WIDTH))


def _packed_sections(d_model):
    heads = d_model // 256
    mixer = heads * HEAD_DIM
    gmlp = d_model // 2
    widths = (("gdn_qkv", 3 * mixer), ("gdn_gate", mixer), ("gmlp_uv", 2 * gmlp), ("sba_qkv", 3 * mixer),
              ("gates", N_BRANCHES * d_model), ("gdn_ab", 2 * heads))
    sections, off = {}, 0
    for name, width in widths:
        sections[name] = (off, width)
        off += width
    return sections, _round_up(off, MATMUL_K_TILE)


def _pack_w_in(w, d_model):
    sections, total = _packed_sections(d_model)
    heads = d_model // 256
    mixer = heads * HEAD_DIM
    ref_off = {"gdn_qkv": 0, "gdn_ab": 3 * mixer, "gdn_gate": 3 * mixer + 2 * heads}
    ref_off["gmlp_uv"] = ref_off["gdn_gate"] + mixer
    ref_off["sba_qkv"] = ref_off["gmlp_uv"] + d_model
    ref_off["gates"] = ref_off["sba_qkv"] + 3 * mixer
    parts = [w[..., ref_off[name]:ref_off[name] + width] for name, (_, width) in sections.items()]
    used = sum(width for _, width in sections.values())
    parts.append(jnp.zeros(w.shape[:-1] + (total - used,), w.dtype))
    return jnp.concatenate(parts, axis=-1)


def _unpack_w_in(wp, d_model):
    sections, _ = _packed_sections(d_model)
    order = ("gdn_qkv", "gdn_ab", "gdn_gate", "gmlp_uv", "sba_qkv", "gates")
    return jnp.concatenate([wp[..., sections[n][0]:sections[n][0] + sections[n][1]] for n in order], axis=-1)


def _mixer(h, p, layer):
    t_len, d_model = h.shape
    heads = d_model // 256
    mixer = heads * HEAD_DIM
    gmlp = d_model // 2
    sections, _ = _packed_sections(d_model)
    z = _linear(h, p["w_in"][layer], "w_in")
    cut = lambda name: z[:, sections[name][0]:sections[name][0] + sections[name][1]]
    per_head = lambda t: t.reshape(t_len, heads, HEAD_DIM)

    qkv = jax.nn.silu(_causal_conv(cut("gdn_qkv"), p["conv_w"][layer]))
    qa = _l2_norm(per_head(qkv[:, :mixer])) * HEAD_DIM ** -0.5
    ka = _l2_norm(per_head(qkv[:, mixer:2 * mixer]))
    va = qkv[:, 2 * mixer:]
    ab = cut("gdn_ab")
    to_chunks = lambda t: jnp.transpose(t).reshape(heads, t_len // CHUNK, CHUNK)
    beta = to_chunks(jax.nn.sigmoid(ab[:, heads:]))
    g = to_chunks(-jnp.exp(p["a_log"][layer]) * jax.nn.softplus(ab[:, :heads] + p["dt_bias"][layer]))
    oa = _gated_delta_rule(qa.reshape(t_len, mixer), ka.reshape(t_len, mixer), va, g, beta)
    oa = _rms_norm(per_head(oa), p["gdn_norm_g"][layer]) * jax.nn.silu(per_head(cut("gdn_gate")))
    branch_a = _linear(oa.reshape(t_len, mixer), p["w_out_a"][layer], "w_out_branch", "col")

    uv = jax.nn.gelu(cut("gmlp_uv"), approximate=False)
    u, vb = uv[:, :gmlp], uv[:, gmlp:]
    group_dim = gmlp // GMLP_GROUPS
    vb = _layer_norm(vb, p["gmlp_ln_g"][layer]).reshape(t_len // GMLP_BLOCK, GMLP_BLOCK, GMLP_GROUPS, group_dim)
    pos = jnp.arange(GMLP_BLOCK) // CHUNK
    ws = jnp.where((pos[None, :] <= pos[:, None])[None], p["w_spatial"][layer], 0.0)
    s = jnp.einsum("gts,nsgc->ntgc", ws.astype(BF16), vb.astype(BF16), preferred_element_type=F32)
    s = s + jnp.transpose(p["b_spatial"][layer])[None, :, :, None]
    branch_b = _linear(u * s.reshape(t_len, gmlp), p["w_out_b"][layer], "w_out_branch", "col")

    sba = cut("sba_qkv")
    qc = _rms_norm(per_head(sba[:, :mixer]), p["sba_q_g"][layer]).reshape(t_len, mixer)
    kc = _rms_norm(per_head(sba[:, mixer:2 * mixer]), p["sba_k_g"][layer]).reshape(t_len, mixer)
    oc = _stick_breaking(qc, kc, sba[:, 2 * mixer:])
    branch_c = _linear(oc, p["w_out_c"][layer], "w_out_branch", "col")

    gates = jax.nn.sigmoid(cut("gates"))
    y = (gates[:, :d_model] * branch_a + gates[:, d_model:2 * d_model] * branch_b
         + gates[:, 2 * d_model:] * branch_c)
    return _linear(y, p["w_out"][layer], "w_out", "row")


def _local_loss(x, p, target):
    depth = len(p["w_in"])
    for layer in range(depth):
        x = x + _mixer(_rms_norm(x, p["norm_mix_g"][layer]), p, layer)
        f = _linear(_rms_norm(x, p["norm_mlp_g"][layer]), p["w_ff1"][layer], "w_ff1", "col")
        x = x + _linear(jnp.square(jax.nn.relu(f)), p["w_ff2"][layer], "w_ff2", "row")
    err = jnp.square(x - target)
    return 0.5 * jnp.sum(jnp.mean(err, axis=-1))


ANY = pl.BlockSpec(memory_space=pl.ANY)


def _place():
    x, y, c = lax.axis_index("x"), lax.axis_index("y"), lax.axis_index("c")
    other_chips = [(1 - x, y), (x, 1 - y), (1 - x, 1 - y)]
    return x, y, c, other_chips


def _my_chip():
    return 2 * lax.axis_index("x") + lax.axis_index("y")


def _cast_to_slot(w, layer):
    _, rows, cols = w.shape
    tr = _row_tile(rows, cols, 16)

    def body(w_ref, o_ref):
        o_ref[...] = w_ref[...].astype(BF16)

    return pl.pallas_call(
        body,
        name="cast_to_slot",
        grid=(rows // tr,),
        in_specs=[pl.BlockSpec((None, tr, cols), lambda r: (layer, r, 0))],
        out_specs=pl.BlockSpec((None, tr, cols), lambda r: (_my_chip(), r, 0)),
        out_shape=jax.ShapeDtypeStruct((N_CHIPS, rows, cols), BF16),
        compiler_params=_params("parallel"),
    )(w)


def _remote(src, dst, sems, a, k, to):
    send_sems, recv_sems = sems
    return pltpu.make_async_remote_copy(src_ref=src, dst_ref=dst, send_sem=send_sems.at[a, k],
                                        recv_sem=recv_sems.at[a, k], device_id=to, device_id_type=MESH)


def _sem_pairs(n_arrays, n_copies):
    return [pltpu.SemaphoreType.DMA((n_arrays, n_copies)), pltpu.SemaphoreType.DMA((n_arrays, n_copies))]


def _gather_chips(bufs):
    n = len(bufs)

    def body(*refs):
        outs, sems = refs[n:2 * n], refs[2 * n:]
        x, y, c, _ = _place()
        x_nbr, y_nbr, diag, sibling = (1 - x, y), (x, 1 - y), (1 - x, 1 - y), (x, y, 1 - c)

        def piece(a, chip, core, quarter=None):
            half = bufs[a].shape[1] // 2
            rows = pl.ds(core * half, half) if quarter is None else pl.ds(core * half + quarter * (half // 2), half // 2)
            return outs[a].at[2 * chip[0] + chip[1], rows, :]

        def copy(a, k, ref, to):
            return _remote(ref, ref, sems, a, k, to)

        def sends(a):
            return [(0, piece(a, (x, y), c), (*x_nbr, c)),
                    (1, piece(a, (x, y), c), (*y_nbr, c)),
                    (2, piece(a, x_nbr, c, 0), (*y_nbr, c)),
                    (3, piece(a, y_nbr, c, 1), (*x_nbr, c)),
                    (4, piece(a, x_nbr, c), sibling),
                    (5, piece(a, y_nbr, c), sibling),
                    (6, piece(a, diag, c, 0), sibling),
                    (7, piece(a, diag, c, 1), sibling)]

        def lands(a):
            return [piece(a, x_nbr, c), piece(a, y_nbr, c), piece(a, diag, c, 0), piece(a, diag, c, 1),
                    piece(a, x_nbr, 1 - c), piece(a, y_nbr, 1 - c), piece(a, diag, 1 - c, 0), piece(a, diag, 1 - c, 1)]

        started = []

        def start(a, k):
            _, ref, to = sends(a)[k]
            started.append(copy(a, k, ref, to))
            started[-1].start()

        def arrived(a, k):
            copy(a, k, lands(a)[k], sibling).wait_recv()

        for a in range(n):
            start(a, 0)
            start(a, 1)
        for k, onward in ((0, (2, 4)), (1, (3, 5)), (2, (6,)), (3, (7,))):
            for a in range(n):
                arrived(a, k)
                for nxt in onward:
                    start(a, nxt)
        for a in range(n):
            for k in (4, 5, 6, 7):
                arrived(a, k)
        for cp in started:
            cp.wait_send()

    return pl.pallas_call(
        body,
        name="gather_chips",
        in_specs=[ANY] * n,
        out_specs=[ANY] * n,
        out_shape=[jax.ShapeDtypeStruct(b.shape, b.dtype) for b in bufs],
        input_output_aliases={a: a for a in range(n)},
        scratch_shapes=_sem_pairs(n, 8),
    )(*bufs)


def _pair_exchange(grads):
    n = len(grads)

    def body(*refs):
        ins, outs, sems = refs[:n], refs[n:2 * n], refs[2 * n:]
        x, y, c, _ = _place()
        copies = []
        for a in range(n):
            rows = grads[a].shape[1] // 2
            copies.append(_remote(ins[a].at[:, pl.ds((1 - c) * rows, rows), :], outs[a], sems, a, 0, (x, y, 1 - c)))
        for cp in copies:
            cp.start()
        for cp in copies:
            cp.wait()

    return pl.pallas_call(
        body,
        name="pair_exchange",
        in_specs=[ANY] * n,
        out_specs=[ANY] * n,
        out_shape=[jax.ShapeDtypeStruct((g.shape[0], g.shape[1] // 2, g.shape[2]), g.dtype) for g in grads],
        scratch_shapes=_sem_pairs(n, 1),
    )(*grads)


def _chip_exchange(parts):
    n = len(parts)

    def body(*refs):
        ins, outs, sems = refs[:n], refs[n:2 * n], refs[2 * n:]
        x, y, c, other_chips = _place()
        me = 2 * x + y
        copies = [_remote(ins[a].at[2 * chip[0] + chip[1]], outs[a].at[me], sems, a, j, (*chip, c))
                  for a in range(n) for j, chip in enumerate(other_chips)]
        for cp in copies:
            cp.start()
        for cp in copies:
            cp.wait()

    return pl.pallas_call(
        body,
        name="chip_exchange",
        in_specs=[ANY] * n,
        out_specs=[ANY] * n,
        out_shape=[jax.ShapeDtypeStruct(p.shape, p.dtype) for p in parts],
        scratch_shapes=_sem_pairs(n, 3),
    )(*parts)


def _pair_share(halves):
    n = len(halves)

    def body(*refs):
        outs, sems = refs[n:2 * n], refs[2 * n:]
        x, y, c, _ = _place()
        copies = []
        for a in range(n):
            rows = halves[a].shape[0] // 2
            mine = outs[a].at[pl.ds(c * rows, rows), :]
            copies.append(_remote(mine, mine, sems, a, 0, (x, y, 1 - c)))
        for cp in copies:
            cp.start()
        for cp in copies:
            cp.wait()

    return pl.pallas_call(
        body,
        name="pair_share",
        in_specs=[ANY] * n,
        out_specs=[ANY] * n,
        out_shape=[jax.ShapeDtypeStruct(h.shape, h.dtype) for h in halves],
        input_output_aliases={a: a for a in range(n)},
        scratch_shapes=_sem_pairs(n, 1),
    )(*halves)


def _all_gather_rows(block):
    m_per, n = block.shape

    def body(x_ref, out_ref, send_sems, recv_sems, local_sem):
        x, y, c, other_chips = _place()
        me, sibling = (x, y, c), (x, y, 1 - c)

        def rows(px, py, pc):
            return out_ref.at[pl.ds((4 * px + 2 * py + pc) * m_per, m_per), :]

        def copy(k, blk, to, src=None):
            return pltpu.make_async_remote_copy(src_ref=rows(*blk) if src is None else src, dst_ref=rows(*blk),
                                                send_sem=send_sems.at[k], recv_sem=recv_sems.at[k],
                                                device_id=to, device_id_type=MESH)

        mine = pltpu.make_async_copy(x_ref, rows(*me), local_sem)
        mine.start()
        first = [copy(0, me, sibling, src=x_ref)]
        first += [copy(1 + j, me, (*chip, c), src=x_ref) for j, chip in enumerate(other_chips)]
        for cp in first:
            cp.start()
        passed = [copy(4 + j, (*chip, c), sibling) for j, chip in enumerate(other_chips)]
        for j, chip in enumerate(other_chips):
            copy(1 + j, (*chip, c), me).wait_recv()
            passed[j].start()
        copy(0, sibling, me).wait_recv()
        for j, chip in enumerate(other_chips):
            copy(4 + j, (*chip, 1 - c), me).wait_recv()
        for cp in first + passed:
            cp.wait_send()
        mine.wait()

    return pl.pallas_call(
        body,
        name="all_gather_rows",
        in_specs=[ANY],
        out_specs=ANY,
        out_shape=jax.ShapeDtypeStruct((N_DEV * m_per, n), block.dtype),
        scratch_shapes=[pltpu.SemaphoreType.DMA((7,)), pltpu.SemaphoreType.DMA((7,)), pltpu.SemaphoreType.DMA],
    )(block)


def _row_tile(rows, cols, multiple):
    budget = max(multiple, ELEMENTWISE_BLOCK_ELEMS // _round_up(cols, LANES))
    t = multiple
    while t * 2 <= budget and rows % (t * 2) == 0:
        t *= 2
    return t if rows % t == 0 else rows


def _pair_sum(grad, theirs):
    chips, rows, cols = grad.shape
    half = rows // 2
    tr = _row_tile(half, cols, 16)
    blocks = half // tr

    def body(a_ref, b_ref, o_ref):
        o_ref[...] = (a_ref[...].astype(F32) + b_ref[...].astype(F32)).astype(o_ref.dtype)

    return pl.pallas_call(
        body,
        name="pair_sum",
        grid=(chips, blocks),
        in_specs=[pl.BlockSpec((None, tr, cols), lambda s, r: (s, lax.axis_index("c") * blocks + r, 0)),
                  pl.BlockSpec((None, tr, cols), lambda s, r: (s, r, 0))],
        out_specs=pl.BlockSpec((None, tr, cols), lambda s, r: (s, r, 0)),
        out_shape=jax.ShapeDtypeStruct((chips, half, cols), grad.dtype),
        compiler_params=_params("parallel", "parallel"),
    )(grad, theirs)


def _chip_sum(mine, others):
    chips, half, cols = mine.shape
    tr = _row_tile(half, cols, 16)
    blocks = half // tr

    def body(own_ref, *refs):
        o_ref = refs[-1]
        total = own_ref[...].astype(F32)
        for ref in refs[:-1]:
            total = total + ref[...].astype(F32)
        o_ref[...] = total

    slot = lambda q: pl.BlockSpec((None, tr, cols), lambda r: ((_my_chip() + q) % chips, r, 0))
    return pl.pallas_call(
        body,
        name="chip_sum",
        grid=(blocks,),
        in_specs=[slot(q) for q in range(chips)],
        out_specs=pl.BlockSpec((tr, cols), lambda r: (lax.axis_index("c") * blocks + r, 0)),
        out_shape=jax.ShapeDtypeStruct((2 * half, cols), F32),
        compiler_params=_params("parallel"),
    )(mine, *([others] * (chips - 1)))


def _device_sum(blocks, m_per):
    tr = _row_tile(m_per, LANES, 8)
    per = m_per // tr

    def body(*refs):
        o_ref = refs[-1]
        total = refs[0][...]
        for ref in refs[1:-1]:
            total = total + ref[...]
        o_ref[...] = total

    return pl.pallas_call(
        body,
        name="device_sum",
        grid=(per,),
        in_specs=[pl.BlockSpec((tr, LANES), functools.partial(lambda d, r: (d * per + r, 0), d)) for d in range(N_DEV)],
        out_specs=pl.BlockSpec((tr, LANES), lambda r: (r, 0)),
        out_shape=jax.ShapeDtypeStruct((m_per, LANES), F32),
        compiler_params=_params("parallel"),
    )(*([blocks] * N_DEV))


def _adamw(grads, w, m, v):
    depth, rows, cols = w.shape
    tr = _row_tile(rows, cols, 8) if rows % 8 == 0 else rows
    spec = pl.BlockSpec((None, tr, cols), lambda l, r: (l, r, 0))
    grad_spec = lambda q: pl.BlockSpec((tr, cols), lambda l, r: (jnp.where(l == q, r, 0), 0))

    def body(*refs):
        g_refs, (w_ref, m_ref, v_ref, go_ref, d_ref, mo_ref, vo_ref) = refs[:depth], refs[depth:]
        layer = pl.program_id(0)
        g = g_refs[0][...]
        for q in range(1, depth):
            g = jnp.where(layer == q, g_refs[q][...], g)
        m_new = ADAM_B1 * m_ref[...] + (1.0 - ADAM_B1) * g
        v_new = ADAM_B2 * v_ref[...] + (1.0 - ADAM_B2) * jnp.square(g)
        m_hat = m_new / (1.0 - ADAM_B1 ** ADAM_STEP)
        v_hat = v_new / (1.0 - ADAM_B2 ** ADAM_STEP)
        go_ref[...] = g
        d_ref[...] = -ADAM_LR * (m_hat / (jnp.sqrt(v_hat) + ADAM_EPS) + ADAM_WD * w_ref[...])
        mo_ref[...] = m_new
        vo_ref[...] = v_new

    return pl.pallas_call(
        body,
        name="adamw",
        grid=(depth, rows // tr),
        in_specs=[grad_spec(q) for q in range(depth)] + [spec] * 3,
        out_specs=[spec] * 4,
        out_shape=[jax.ShapeDtypeStruct(w.shape, F32)] * 4,
        compiler_params=_params("parallel", "parallel"),
    )(*grads, w, m, v)


def _join_cols(gathered):
    return jnp.concatenate([gathered[s] for s in range(N_CHIPS)], axis=-1)


def _split_cols(full):
    rows, cols = full.shape
    return jnp.transpose(full.reshape(rows, N_CHIPS, cols // N_CHIPS), (1, 0, 2))


def _reduce_scatter(grads):
    theirs = _pair_exchange(grads)
    pairs = [_pair_sum(g, t) for g, t in zip(grads, theirs)]
    others = _chip_exchange(pairs)
    return _pair_share([_chip_sum(p, o) for p, o in zip(pairs, others)])


def _pack_rows(arrays):
    parts, counts = [], []
    for a in arrays:
        flat = a.reshape(-1).astype(F32)
        n_rows = _round_up(flat.shape[0], 8 * LANES) // LANES
        parts.append(jnp.pad(flat, (0, n_rows * LANES - flat.shape[0])).reshape(n_rows, LANES))
        counts.append(n_rows)
    return jnp.concatenate(parts, axis=0), counts


def _pad_rows(a, rows):
    return jnp.pad(a, ((0, rows - a.shape[0]), (0, 0)))


def _unpack_rows(packed, counts, shapes):
    out, row = [], 0
    for n_rows, shape in zip(counts, shapes):
        size = 1
        for d in shape:
            size *= d
        out.append(packed[row:row + n_rows].reshape(-1)[:size].reshape(shape))
        row += n_rows
    return out


SHARDED = (("w_in", "col"), ("w_out_a", "col"), ("w_out_b", "col"), ("w_out_c", "col"), ("w_out", "row"),
           ("w_ff1", "col"), ("w_ff2", "row"))


def kernel(x, w_in, conv_w, a_log, dt_bias, gdn_norm_g, gmlp_ln_g, w_spatial, b_spatial, sba_q_g, sba_k_g, w_out_a, w_out_b, w_out_c, w_out, norm_mix_g, norm_mlp_g, w_ff1, w_ff2, loss_target, m_w_in, m_conv_w, m_a_log, m_dt_bias, m_gdn_norm_g, m_gmlp_ln_g, m_w_spatial, m_b_spatial, m_sba_q_g, m_sba_k_g, m_w_out_a, m_w_out_b, m_w_out_c, m_w_out, m_norm_mix_g, m_norm_mlp_g, m_w_ff1, m_w_ff2, v_w_in, v_conv_w, v_a_log, v_dt_bias, v_gdn_norm_g, v_gmlp_ln_g, v_w_spatial, v_b_spatial, v_sba_q_g, v_sba_k_g, v_w_out_a, v_w_out_b, v_w_out_c, v_w_out, v_norm_mix_g, v_norm_mlp_g, v_w_ff1, v_w_ff2):
    given = dict(locals())
    weights = {n: given[n] for n in WEIGHT_NAMES}
    depth, d_model = w_in.shape[0], w_in.shape[1]
    layers = range(depth)
    chip = _my_chip()

    gathered = [_gather_chips([_cast_to_slot(weights[name], l) for name, _ in SHARDED]) for l in layers]
    conv_rows = depth * CONV_WIDTH
    conv_cols = conv_w.shape[-1]
    conv_all = _all_gather_rows(_pad_rows(conv_w.reshape(conv_rows, conv_cols), _round_up(conv_rows, 8)))
    conv_all = conv_all.reshape(N_CHIPS, 2, _round_up(conv_rows, 8), conv_cols)[:, 0, :conv_rows]
    conv_full = jnp.concatenate([conv_all[s] for s in range(N_CHIPS)], axis=-1).reshape(depth, CONV_WIDTH, N_CHIPS * conv_cols)

    params = {n: weights[n] for n in REPLICATED_NAMES}
    params["conv_w"] = conv_full
    for i, (name, _) in enumerate(SHARDED):
        params[name] = [gathered[l][i] for l in layers]
    params["w_in"] = [_pack_w_in(_join_cols(gathered[l][0]), d_model) for l in layers]

    loss_local, (grad_x, grads) = jax.value_and_grad(_local_loss, argnums=(0, 1))(x[0], params, loss_target[0])

    summed = []
    for l in layers:
        local = [grads[name][l] for name, _ in SHARDED]
        local[0] = _split_cols(_unpack_w_in(local[0], d_model))
        summed.append(_reduce_scatter(local))
    rep_packed, rep_counts = _pack_rows([grads[n] for n in REPLICATED_NAMES])
    rep_rows = _round_up(rep_packed.shape[0], SMALL_ROW_ALIGN)
    rest_packed, rest_counts = _pack_rows([grads["conv_w"], loss_local])
    m_per = rep_rows + _round_up(rest_packed.shape[0], SMALL_ROW_ALIGN)
    packed = jnp.concatenate([_pad_rows(rep_packed, rep_rows), _pad_rows(rest_packed, m_per - rep_rows)], axis=0)
    total = _device_sum(_all_gather_rows(packed), m_per)
    conv_sum, loss = _unpack_rows(total[rep_rows:], rest_counts, [grads["conv_w"].shape, ()])
    conv_grad = lax.dynamic_slice_in_dim(conv_sum, chip * conv_cols, conv_cols, axis=2)

    out = {}
    for i, (name, _) in enumerate(SHARDED):
        out[name] = _adamw([summed[l][i] for l in layers], weights[name], given["m_" + name], given["v_" + name])
    out["conv_w"] = _adamw([conv_grad[l] for l in layers], conv_w, m_conv_w, v_conv_w)
    rep_shapes = [weights[n].shape for n in REPLICATED_NAMES]
    pack3 = lambda prefix: _pad_rows(_pack_rows([given[prefix + n] for n in REPLICATED_NAMES])[0], rep_rows)[None]
    rep = _adamw([total[:rep_rows]], pack3(""), pack3("m_"), pack3("v_"))
    rep = [_unpack_rows(t[0], rep_counts, rep_shapes) for t in rep]
    for i, name in enumerate(REPLICATED_NAMES):
        out[name] = tuple(t[i] for t in rep)

    results = [loss, grad_x[None]]
    for kind in range(4):
        results += [out[n][kind] for n in WEIGHT_NAMES]
    return tuple(results)
```

```python
import functools

import jax
import jax.numpy as jnp
from jax import lax
from jax.experimental import pallas as pl
from jax.experimental.pallas import tpu as pltpu

F32, BF16 = jnp.float32, jnp.bfloat16
MESH = pl.DeviceIdType.MESH
N_CHIPS = 4
N_DEV = 8

EPS = 1e-6
CHUNK = 64
HEAD_DIM = 128
CONV_WIDTH = 4
GMLP_GROUPS = 8
GMLP_BLOCK = 128
N_BRANCHES = 3
ADAM_LR, ADAM_B1, ADAM_B2, ADAM_EPS, ADAM_WD, ADAM_STEP = 0.001, 0.9, 0.999, 1e-08, 0.01, 10

V7X_VMEM_BYTES = 64 * 1024 * 1024
VMEM_LIMIT = V7X_VMEM_BYTES - 8 * 1024 * 1024
LANES = 128
MATMUL_TILE = 1024
MATMUL_K_TILE = 2048
ATTN_TILE = 256
GDN_HEADS_PER_STEP = 8
SBA_HEADS_PER_STEP = 4
GDN_SEGMENT = 512
ELEMENTWISE_BLOCK_ELEMS = 512 * 1024
SMALL_ROW_ALIGN = 256

WEIGHT_NAMES = ("w_in", "conv_w", "a_log", "dt_bias", "gdn_norm_g", "gmlp_ln_g", "w_spatial", "b_spatial",
                "sba_q_g", "sba_k_g", "w_out_a", "w_out_b", "w_out_c", "w_out", "norm_mix_g", "norm_mlp_g",
                "w_ff1", "w_ff2")
REPLICATED_NAMES = ("a_log", "dt_bias", "gdn_norm_g", "gmlp_ln_g", "w_spatial", "b_spatial", "sba_q_g",
                    "sba_k_g", "norm_mix_g", "norm_mlp_g")


def _round_up(n, m):
    return (n + m - 1) // m * m


def _tile(dim, pref):
    if dim <= pref:
        return dim
    t = pref // LANES * LANES
    while t >= LANES:
        if dim % t == 0:
            return t
        t -= LANES
    return dim


def _params(*semantics):
    return pltpu.CompilerParams(dimension_semantics=semantics, vmem_limit_bytes=VMEM_LIMIT)


_DOT_DIMS = {"nn": ((1,), (0,)), "nt": ((1,), (1,)), "tn": ((0,), (0,))}


def _logical_shape(shape, kind):
    if kind is None:
        return shape
    chips, rows, cols = shape
    return (rows, chips * cols) if kind == "col" else (chips * rows, cols)


def _weight_tiles(shape, kind, row_pref, col_pref):
    rows, cols = _logical_shape(shape, kind)
    tr = _tile(shape[1] if kind == "row" else rows, row_pref)
    tc = _tile(shape[2] if kind == "col" else cols, col_pref)
    return tr, tc


def _weight_spec(shape, kind, tr, tc, pick):
    if kind is None:
        return pl.BlockSpec((tr, tc), pick)
    if kind == "col":
        per = shape[2] // tc
        return pl.BlockSpec((None, tr, tc), lambda i, j, k: (pick(i, j, k)[1] // per, pick(i, j, k)[0], pick(i, j, k)[1] % per))
    per = shape[1] // tr
    return pl.BlockSpec((None, tr, tc), lambda i, j, k: (pick(i, j, k)[0] // per, pick(i, j, k)[0] % per, pick(i, j, k)[1]))


def _matmul(a, b, mode, out_dtype, name, kind=None, out_shape=None):
    if mode == "tn":
        k_dim, m_dim = a.shape
        n_dim = b.shape[1]
        out_shape = (m_dim, n_dim) if kind is None else out_shape
        tm, tn = _weight_tiles(out_shape, kind, MATMUL_TILE, MATMUL_TILE)
        tk = _tile(k_dim, MATMUL_K_TILE)
        a_spec = pl.BlockSpec((tk, tm), lambda i, j, k: (k, i))
        b_spec = pl.BlockSpec((tk, tn), lambda i, j, k: (k, j))
        out_spec = _weight_spec(out_shape, kind, tm, tn, lambda i, j, k: (i, j))
    else:
        m_dim, k_dim = a.shape
        tm = _tile(m_dim, MATMUL_TILE)
        if mode == "nn":
            _, n_dim = _logical_shape(b.shape, kind)
            tk, tn = _weight_tiles(b.shape, kind, MATMUL_K_TILE, MATMUL_TILE)
            b_spec = _weight_spec(b.shape, kind, tk, tn, lambda i, j, k: (k, j))
        else:
            n_dim, _ = _logical_shape(b.shape, kind)
            tn, tk = _weight_tiles(b.shape, kind, MATMUL_TILE, MATMUL_K_TILE)
            b_spec = _weight_spec(b.shape, kind, tn, tk, lambda i, j, k: (j, k))
        a_spec = pl.BlockSpec((tm, tk), lambda i, j, k: (i, k))
        out_shape = (m_dim, n_dim)
        out_spec = pl.BlockSpec((tm, tn), lambda i, j, k: (i, j))
    nk = k_dim // tk
    dims = (_DOT_DIMS[mode], ((), ()))

    def body(a_ref, b_ref, o_ref, *scratch):
        part = lax.dot_general(a_ref[...], b_ref[...], dims, preferred_element_type=F32)
        if nk == 1:
            o_ref[...] = part.astype(out_dtype)
            return
        acc_ref, = scratch
        k = pl.program_id(2)

        @pl.when(k == 0)
        def _():
            acc_ref[...] = part

        @pl.when(k > 0)
        def _():
            acc_ref[...] += part

        @pl.when(k == nk - 1)
        def _():
            o_ref[...] = acc_ref[...].astype(out_dtype)

    return pl.pallas_call(
        body,
        name=name,
        grid=(m_dim // tm, n_dim // tn, nk),
        in_specs=[a_spec, b_spec],
        out_specs=out_spec,
        out_shape=jax.ShapeDtypeStruct(out_shape, out_dtype),
        scratch_shapes=[] if nk == 1 else [pltpu.VMEM((tm, tn), F32)],
        compiler_params=_params("parallel", "parallel", "arbitrary"),
    )(a, b)


@functools.partial(jax.custom_vjp, nondiff_argnums=(2, 3))
def _linear(a, w, name, kind=None):
    return _matmul(a.astype(BF16), w, "nn", F32, name + "_fwd", kind)


def _linear_fwd(a, w, name, kind):
    a16 = a.astype(BF16)
    return _matmul(a16, w, "nn", F32, name + "_fwd", kind), (a16, w)


def _linear_bwd(name, kind, res, g):
    a16, w = res
    g16 = g.astype(BF16)
    da = _matmul(g16, w, "nt", F32, name + "_dgrad", kind)
    dw = _matmul(a16, g16, "tn", BF16, name + "_wgrad", kind, w.shape)
    return da, dw


_linear.defvjp(_linear_fwd, _linear_bwd)


def _split3_dots(xs, ones_mat):
    his = [x.astype(BF16) for x in xs]
    rests = [x - hi.astype(F32) for x, hi in zip(xs, his)]
    mids = [rest.astype(BF16) for rest in rests]
    los = [(rest - mid.astype(F32)).astype(BF16) for rest, mid in zip(rests, mids)]
    dots = lambda parts: [jnp.dot(p, ones_mat, preferred_element_type=F32) for p in parts]
    return [a + b + c for a, b, c in zip(dots(his), dots(mids), dots(los))]


def _dot_nt(a, b):
    return lax.dot_general(a, b, (((1,), (1,)), ((), ())), preferred_element_type=F32)


def _dot_tn(a, b):
    return lax.dot_general(a, b, (((0,), (0,)), ((), ())), preferred_element_type=F32)


def _sba_scores(q16, k16, tile, scale, diagonal):
    z = _dot_nt(q16, k16) * scale
    log_sig = jnp.minimum(z, 0.0) - jnp.log(1.0 + jnp.exp(-jnp.abs(z)))
    if not diagonal:
        return None, log_sig, log_sig - z
    mask = lax.broadcasted_iota(jnp.int32, (tile, tile), 1) < lax.broadcasted_iota(jnp.int32, (tile, tile), 0)
    return mask, log_sig, jnp.where(mask, log_sig - z, 0.0)


def _masked(mask, value):
    return value if mask is None else jnp.where(mask, value, 0.0)


def _sba_layout(q):
    t_len, width = q.shape
    heads = width // HEAD_DIM
    group = min(SBA_HEADS_PER_STEP, heads)
    tile = min(ATTN_TILE, t_len)
    q_spec = pl.BlockSpec((tile, group * HEAD_DIM), lambda h, i: (i, h))
    kv_spec = pl.BlockSpec((t_len, group * HEAD_DIM), lambda h, i: (0, h))
    total_spec = pl.BlockSpec((group, tile, 1), lambda h, i: (h, i, 0))
    return heads, group, tile, (heads // group, t_len // tile), q_spec, kv_spec, total_spec


def _sba_forward(q, k, v):
    t_len, width = q.shape
    heads, group, tile, grid, q_spec, kv_spec, total_spec = _sba_layout(q)
    scale = HEAD_DIM ** -0.5
    cols = [slice(h * HEAD_DIM, (h + 1) * HEAD_DIM) for h in range(group)]

    def body(q_ref, k_ref, v_ref, o_ref, r_ref):
        i = pl.program_id(1)
        q16 = [q_ref[:, c].astype(BF16) for c in cols]
        ri = lax.broadcasted_iota(jnp.int32, (tile, tile), 0)
        ci = lax.broadcasted_iota(jnp.int32, (tile, tile), 1)
        later = (ri > ci).astype(BF16)

        def tile_step(j, carry, diagonal):
            runs, accs = carry
            rows = pl.ds(pl.multiple_of(j * tile, tile), tile)
            k16 = [k_ref[rows, c].astype(BF16) for c in cols]
            v16 = [v_ref[rows, c].astype(BF16) for c in cols]
            scores = [_sba_scores(q16[h], k16[h], tile, scale, diagonal) for h in range(group)]
            within = _split3_dots([log_keep for _, _, log_keep in scores], later)
            atts = [_masked(scores[h][0], jnp.exp(scores[h][1] + within[h] + runs[h])) for h in range(group)]
            accs = [accs[h] + jnp.dot(atts[h].astype(BF16), v16[h], preferred_element_type=F32) for h in range(group)]
            runs = [runs[h] + jnp.sum(scores[h][2], axis=1, keepdims=True) for h in range(group)]
            return runs, accs

        zeros = lambda width: [jnp.zeros((tile, width), F32) for _ in range(group)]
        carry = tile_step(i, (zeros(1), zeros(HEAD_DIM)), True)
        runs, accs = lax.fori_loop(0, i, lambda jj, carry: tile_step(i - 1 - jj, carry, False), carry)
        for h in range(group):
            o_ref[:, cols[h]] = accs[h]
            r_ref[h] = runs[h]

    return pl.pallas_call(
        body,
        name="sba_fwd",
        grid=grid,
        in_specs=[q_spec, kv_spec, kv_spec],
        out_specs=[q_spec, total_spec],
        out_shape=[jax.ShapeDtypeStruct((t_len, width), F32), jax.ShapeDtypeStruct((heads, t_len, 1), F32)],
        compiler_params=_params("parallel", "arbitrary"),
    )(q, k, v)


def _sba_backward(q, k, v, total, do):
    t_len, width = q.shape
    heads, group, tile, grid, q_spec, kv_spec, total_spec = _sba_layout(q)
    scale = HEAD_DIM ** -0.5
    cols = [slice(h * HEAD_DIM, (h + 1) * HEAD_DIM) for h in range(group)]

    def body(q_ref, k_ref, v_ref, r_ref, do_ref, dq_ref, dk_ref, dv_ref):
        i = pl.program_id(1)

        @pl.when(i == 0)
        def _():
            dk_ref[...] = jnp.zeros_like(dk_ref)
            dv_ref[...] = jnp.zeros_like(dv_ref)

        q16 = [q_ref[:, c].astype(BF16) for c in cols]
        do16 = [do_ref[:, c].astype(BF16) for c in cols]
        tot = [r_ref[h] for h in range(group)]
        ri = lax.broadcasted_iota(jnp.int32, (tile, tile), 0)
        ci = lax.broadcasted_iota(jnp.int32, (tile, tile), 1)
        upto = (ri <= ci).astype(BF16)
        before = (ri < ci).astype(BF16)

        def tile_step(j, carry, diagonal):
            keep_left, w_left, dq = carry
            rows = pl.ds(pl.multiple_of(j * tile, tile), tile)
            heads_here = range(group)
            k16 = [k_ref[rows, c].astype(BF16) for c in cols]
            v16 = [v_ref[rows, c].astype(BF16) for c in cols]
            scores = [_sba_scores(q16[h], k16[h], tile, scale, diagonal) for h in heads_here]
            within = _split3_dots([log_keep for _, _, log_keep in scores], upto)
            atts = [_masked(scores[h][0], jnp.exp(scores[h][1] + tot[h] - keep_left[h] - within[h])) for h in heads_here]
            d_att = [_dot_nt(do16[h], v16[h]) for h in heads_here]
            w = [atts[h] * d_att[h] for h in heads_here]
            w_before = _split3_dots(w, before)
            sig = [jnp.exp(scores[h][1]) for h in heads_here]
            dz16 = [(_masked(scores[h][0], w[h] * (1.0 - sig[h]) - sig[h] * (w_left[h] + w_before[h])) * scale).astype(BF16)
                    for h in heads_here]
            dq = [dq[h] + jnp.dot(dz16[h], k16[h], preferred_element_type=F32) for h in heads_here]
            dk = [_dot_tn(dz16[h], q16[h]) for h in heads_here]
            dv = [_dot_tn(atts[h].astype(BF16), do16[h]) for h in heads_here]
            for h in heads_here:
                dk_ref[rows, cols[h]] += dk[h]
                dv_ref[rows, cols[h]] += dv[h]
            keep_left = [keep_left[h] + jnp.sum(scores[h][2], axis=1, keepdims=True) for h in heads_here]
            w_left = [w_left[h] + jnp.sum(w[h], axis=1, keepdims=True) for h in heads_here]
            return keep_left, w_left, dq

        zeros = lambda width: [jnp.zeros((tile, width), F32) for _ in range(group)]
        carry = lax.fori_loop(0, i, lambda j, carry: tile_step(j, carry, False), (zeros(1), zeros(1), zeros(HEAD_DIM)))
        _, _, dq = tile_step(i, carry, True)
        for h in range(group):
            dq_ref[:, cols[h]] = dq[h]

    return pl.pallas_call(
        body,
        name="sba_bwd",
        grid=grid,
        in_specs=[q_spec, kv_spec, kv_spec, total_spec, q_spec],
        out_specs=[q_spec, kv_spec, kv_spec],
        out_shape=[jax.ShapeDtypeStruct((t_len, width), F32)] * 3,
        compiler_params=_params("parallel", "arbitrary"),
    )(q, k, v, total, do)


@jax.custom_vjp
def _stick_breaking(q, k, v):
    return _sba_forward(q, k, v)[0]


def _stick_breaking_fwd(q, k, v):
    o, total = _sba_forward(q, k, v)
    return o, (q, k, v, total)


def _stick_breaking_bwd(res, do):
    q, k, v, total = res
    return tuple(_sba_backward(q, k, v, total, do))


_stick_breaking.defvjp(_stick_breaking_fwd, _stick_breaking_bwd)


def _contract(a16, b16, ca, cb):
    return lax.dot_general(a16, b16, (((ca,), (cb,)), ((), ())), preferred_element_type=F32)


def _pdot_raw(a, b, ca, cb):
    a_hi = a.astype(BF16)
    a_lo = (a - a_hi.astype(F32)).astype(BF16)
    b_hi = b.astype(BF16)
    b_lo = (b - b_hi.astype(F32)).astype(BF16)
    return _contract(a_hi, b_hi, ca, cb) + _contract(a_hi, b_lo, ca, cb) + _contract(a_lo, b_hi, ca, cb)


def _bdot_raw(a, b, ca, cb):
    return _contract(a.astype(BF16), b.astype(BF16), ca, cb)


def _make_dot(raw):
    @functools.partial(jax.custom_vjp, nondiff_argnums=(2, 3))
    def dot(a, b, ca, cb):
        return raw(a, b, ca, cb)

    def fwd(a, b, ca, cb):
        return raw(a, b, ca, cb), (a, b)

    def bwd(ca, cb, res, g):
        a, b = res
        da = raw(g, b, 1, 1 - cb) if ca == 1 else raw(b, g, 1 - cb, 1)
        db = raw(a, g, 1 - ca, 0) if cb == 0 else raw(g, a, 0, 1 - ca)
        return da, db

    dot.defvjp(fwd, bwd)
    return dot


_bdot = _make_dot(_bdot_raw)


@jax.custom_vjp
def _unit_lower_inverses(lows):
    size = lows[0].shape[0]
    eye = (lax.broadcasted_iota(jnp.int32, (size, size), 0) == lax.broadcasted_iota(jnp.int32, (size, size), 1)).astype(F32)
    invs = [eye - low for low in lows]
    powers = list(lows)
    span = 2
    while span < size:
        powers = [_bdot_raw(p, p, 1, 0) for p in powers]
        invs = [inv + _bdot_raw(inv, p, 1, 0) for inv, p in zip(invs, powers)]
        span *= 2
    resids = [eye - inv - _pdot_raw(low, inv, 1, 0) for low, inv in zip(lows, invs)]
    return [inv + _bdot_raw(inv, r, 1, 0) for inv, r in zip(invs, resids)]


def _unit_lower_inverses_fwd(lows):
    invs = _unit_lower_inverses(lows)
    return invs, invs


def _unit_lower_inverses_bwd(invs, gs):
    left = [_bdot_raw(inv, g, 0, 0) for inv, g in zip(invs, gs)]
    return ([-_bdot_raw(l, inv, 1, 1) for l, inv in zip(left, invs)],)


_unit_lower_inverses.defvjp(_unit_lower_inverses_fwd, _unit_lower_inverses_bwd)


def _gdn_chunks(states, qs, ks, vs, g_rows, b_rows):
    heads = range(len(qs))
    size = qs[0].shape[0]
    ri = lax.broadcasted_iota(jnp.int32, (size, size), 0)
    ci = lax.broadcasted_iota(jnp.int32, (size, size), 1)
    eye, incl, strict = ri == ci, ci <= ri, ci < ri
    square = lambda t: jnp.broadcast_to(t, (size, size))
    g_rowb = [square(g) for g in g_rows]
    g_col = [jnp.sum(jnp.where(eye, g, 0.0), axis=1, keepdims=True) for g in g_rowb]
    b_col = [jnp.sum(jnp.where(eye, square(b), 0.0), axis=1, keepdims=True) for b in b_rows]
    gc_col = [jnp.sum(jnp.where(incl, g, 0.0), axis=1, keepdims=True) for g in g_rowb]
    gc_row = [jnp.sum(jnp.where(ri <= ci, square(g), 0.0), axis=0, keepdims=True) for g in g_col]
    g_last = [jnp.sum(g, axis=1, keepdims=True) for g in g_rows]
    decay = [jnp.where(incl, jnp.exp(jnp.where(incl, gc_col[h] - gc_row[h], 0.0)), 0.0) for h in heads]
    kb = [ks[h] * b_col[h] for h in heads]
    kk = [_bdot(kb[h], ks[h], 1, 1) for h in heads]
    invs = _unit_lower_inverses([jnp.where(strict, kk[h] * decay[h], 0.0) for h in heads])
    grow = [jnp.exp(gc_col[h]) for h in heads]
    u = [_bdot(invs[h], vs[h] * b_col[h], 1, 0) for h in heads]
    w = [_bdot(invs[h], kb[h] * grow[h], 1, 0) for h in heads]
    qk = [_bdot(qs[h], ks[h], 1, 1) for h in heads]
    ws = [_bdot(w[h], states[h], 1, 0) for h in heads]
    qs_state = [_bdot(qs[h] * grow[h], states[h], 1, 0) for h in heads]
    v_new = [u[h] - ws[h] for h in heads]
    inner = [_bdot(qk[h] * decay[h], v_new[h], 1, 0) for h in heads]
    k_dec = [ks[h] * jnp.exp(g_last[h] - gc_col[h]) for h in heads]
    kv = [_bdot(k_dec[h], v_new[h], 0, 0) for h in heads]
    new_states = [states[h] * jnp.exp(g_last[h]) + kv[h] for h in heads]
    return new_states, [qs_state[h] + inner[h] for h in heads]


def _gdn_layout(q):
    t_len, width = q.shape
    heads = width // HEAD_DIM
    group = min(GDN_HEADS_PER_STEP, heads)
    seg = min(GDN_SEGMENT, t_len)
    return heads, group, seg, (heads // group, t_len // seg)


def _gdn_specs(group, seg, order):
    chunks = seg // CHUNK
    seq = pl.BlockSpec((seg, group * HEAD_DIM), lambda h, t: (order(t), h))
    gate = pl.BlockSpec((group, chunks, CHUNK), lambda h, t: (h, order(t), 0))
    states = pl.BlockSpec((group, chunks, HEAD_DIM, HEAD_DIM), lambda h, t: (h, order(t), 0, 0))
    return seq, gate, states


def _gdn_forward(q, k, v, g, beta):
    t_len, width = q.shape
    heads, group, seg, grid = _gdn_layout(q)
    seq, gate, states = _gdn_specs(group, seg, lambda t: t)

    def body(q_ref, k_ref, v_ref, g_ref, b_ref, o_ref, s_ref, state_ref):
        @pl.when(pl.program_id(1) == 0)
        def _():
            state_ref[...] = jnp.zeros_like(state_ref)

        def step(n, carry):
            rows = pl.ds(pl.multiple_of(n * CHUNK, CHUNK), CHUNK)
            cols = [slice(j * HEAD_DIM, (j + 1) * HEAD_DIM) for j in range(group)]
            states = [state_ref[j] for j in range(group)]
            for j in range(group):
                s_ref[j, n] = states[j]
            states, outs = _gdn_chunks(states, [q_ref[rows, c] for c in cols], [k_ref[rows, c] for c in cols],
                                       [v_ref[rows, c] for c in cols], [g_ref[j, pl.ds(n, 1), :] for j in range(group)],
                                       [b_ref[j, pl.ds(n, 1), :] for j in range(group)])
            for j in range(group):
                o_ref[rows, cols[j]] = outs[j]
                state_ref[j] = states[j]
            return carry

        lax.fori_loop(0, seg // CHUNK, step, 0)

    return pl.pallas_call(
        body,
        name="gdn_fwd",
        grid=grid,
        in_specs=[seq, seq, seq, gate, gate],
        out_specs=[seq, states],
        out_shape=[jax.ShapeDtypeStruct((t_len, width), F32),
                   jax.ShapeDtypeStruct((heads, t_len // CHUNK, HEAD_DIM, HEAD_DIM), F32)],
        scratch_shapes=[pltpu.VMEM((group, HEAD_DIM, HEAD_DIM), F32)],
        compiler_params=_params("parallel", "arbitrary"),
    )(q, k, v, g, beta)


def _gdn_backward(q, k, v, g, beta, starts, do):
    t_len, width = q.shape
    heads, group, seg, grid = _gdn_layout(q)
    last = grid[1] - 1
    seq, gate, states = _gdn_specs(group, seg, lambda t: last - t)
    chunks = seg // CHUNK

    def body(q_ref, k_ref, v_ref, g_ref, b_ref, s_ref, do_ref, dq_ref, dk_ref, dv_ref, dg_ref, db_ref, d_state_ref):
        @pl.when(pl.program_id(1) == 0)
        def _():
            d_state_ref[...] = jnp.zeros_like(d_state_ref)

        def step(m, carry):
            n = chunks - 1 - m
            rows = pl.ds(pl.multiple_of(n * CHUNK, CHUNK), CHUNK)
            cols = [slice(j * HEAD_DIM, (j + 1) * HEAD_DIM) for j in range(group)]
            per_head = lambda ref: [ref[rows, c] for c in cols]
            _, pull = jax.vjp(_gdn_chunks, [s_ref[j, n] for j in range(group)], per_head(q_ref), per_head(k_ref),
                              per_head(v_ref), [g_ref[j, pl.ds(n, 1), :] for j in range(group)],
                              [b_ref[j, pl.ds(n, 1), :] for j in range(group)])
            d_states, dq, dk, dv, dg, db = pull(([d_state_ref[j] for j in range(group)], per_head(do_ref)))
            for j in range(group):
                dq_ref[rows, cols[j]] = dq[j]
                dk_ref[rows, cols[j]] = dk[j]
                dv_ref[rows, cols[j]] = dv[j]
                dg_ref[j, pl.ds(n, 1), :] = dg[j]
                db_ref[j, pl.ds(n, 1), :] = db[j]
                d_state_ref[j] = d_states[j]
            return carry

        lax.fori_loop(0, chunks, step, 0)

    return pl.pallas_call(
        body,
        name="gdn_bwd",
        grid=grid,
        in_specs=[seq, seq, seq, gate, gate, states, seq],
        out_specs=[seq, seq, seq, gate, gate],
        out_shape=[jax.ShapeDtypeStruct((t_len, width), F32)] * 3
        + [jax.ShapeDtypeStruct((heads, t_len // CHUNK, CHUNK), F32)] * 2,
        scratch_shapes=[pltpu.VMEM((group, HEAD_DIM, HEAD_DIM), F32)],
        compiler_params=_params("parallel", "arbitrary"),
    )(q, k, v, g, beta, starts, do)


@jax.custom_vjp
def _gated_delta_rule(q, k, v, g, beta):
    return _gdn_forward(q, k, v, g, beta)[0]


def _gated_delta_rule_fwd(q, k, v, g, beta):
    o, starts = _gdn_forward(q, k, v, g, beta)
    return o, (q, k, v, g, beta, starts)


def _gated_delta_rule_bwd(res, do):
    return tuple(_gdn_backward(*res, do))


_gated_delta_rule.defvjp(_gated_delta_rule_fwd, _gated_delta_rule_bwd)


def _rms_norm(x, gain):
    return x * lax.rsqrt(jnp.mean(x * x, axis=-1, keepdims=True) + EPS) * gain


def _layer_norm(x, gain):
    xc = x - jnp.mean(x, axis=-1, keepdims=True)
    return xc * lax.rsqrt(jnp.mean(xc * xc, axis=-1, keepdims=True) + EPS) * gain


def _l2_norm(x):
    return x * lax.rsqrt(jnp.sum(x * x, axis=-1, keepdims=True) + EPS)


def _causal_conv(x, w):
    t_len = x.shape[0]
    xp = jnp.pad(x, ((CONV_WIDTH - 1, 0), (0, 0)))
    return sum(xp[i:i + t_len] * w[i] for i in range(CONV_WIDTH))


def _packed_sections(d_model):
    heads = d_model // 256
    mixer = heads * HEAD_DIM
    gmlp = d_model // 2
    widths = (("gdn_qkv", 3 * mixer), ("gdn_gate", mixer), ("gmlp_uv", 2 * gmlp), ("sba_qkv", 3 * mixer),
              ("gates", N_BRANCHES * d_model), ("gdn_ab", 2 * heads))
    sections, off = {}, 0
    for name, width in widths:
        sections[name] = (off, width)
        off += width
    return sections, _round_up(off, MATMUL_K_TILE)


def _pack_w_in(w, d_model):
    sections, total = _packed_sections(d_model)
    heads = d_model // 256
    mixer = heads * HEAD_DIM
    ref_off = {"gdn_qkv": 0, "gdn_ab": 3 * mixer, "gdn_gate": 3 * mixer + 2 * heads}
    ref_off["gmlp_uv"] = ref_off["gdn_gate"] + mixer
    ref_off["sba_qkv"] = ref_off["gmlp_uv"] + d_model
    ref_off["gates"] = ref_off["sba_qkv"] + 3 * mixer
    parts = [w[..., ref_off[name]:ref_off[name] + width] for name, (_, width) in sections.items()]
    used = sum(width for _, width in sections.values())
    parts.append(jnp.zeros(w.shape[:-1] + (total - used,), w.dtype))
    return jnp.concatenate(parts, axis=-1)


def _unpack_w_in_shards(wp, d_model):
    sections, _ = _packed_sections(d_model)
    order = ("gdn_qkv", "gdn_ab", "gdn_gate", "gmlp_uv", "sba_qkv", "gates")
    shard = sum(sections[n][1] for n in order) // N_CHIPS
    shards = []
    for s in range(N_CHIPS):
        pieces, ref_off = [], 0
        for name in order:
            packed_off, width = sections[name]
            lo, hi = max(ref_off, s * shard), min(ref_off + width, (s + 1) * shard)
            if lo < hi:
                pieces.append(wp[:, packed_off + lo - ref_off:packed_off + hi - ref_off])
            ref_off += width
        shards.append(jnp.concatenate(pieces, axis=-1))
    return jnp.stack(shards)


def _mixer(h, p, layer):
    t_len, d_model = h.shape
    heads = d_model // 256
    mixer = heads * HEAD_DIM
    gmlp = d_model // 2
    sections, _ = _packed_sections(d_model)
    z = _linear(h, p["w_in"][layer], "w_in")
    cut = lambda name: z[:, sections[name][0]:sections[name][0] + sections[name][1]]
    per_head = lambda t: t.reshape(t_len, heads, HEAD_DIM)

    qkv = jax.nn.silu(_causal_conv(cut("gdn_qkv"), p["conv_w"][layer]))
    qa = _l2_norm(per_head(qkv[:, :mixer])) * HEAD_DIM ** -0.5
    ka = _l2_norm(per_head(qkv[:, mixer:2 * mixer]))
    va = qkv[:, 2 * mixer:]
    ab = cut("gdn_ab")
    to_chunks = lambda t: jnp.transpose(t).reshape(heads, t_len // CHUNK, CHUNK)
    beta = to_chunks(jax.nn.sigmoid(ab[:, heads:]))
    g = to_chunks(-jnp.exp(p["a_log"][layer]) * jax.nn.softplus(ab[:, :heads] + p["dt_bias"][layer]))
    oa = _gated_delta_rule(qa.reshape(t_len, mixer), ka.reshape(t_len, mixer), va, g, beta)
    oa = _rms_norm(per_head(oa), p["gdn_norm_g"][layer]) * jax.nn.silu(per_head(cut("gdn_gate")))
    branch_a = _linear(oa.reshape(t_len, mixer), p["w_out_a"][layer], "w_out_branch", "col")

    uv = jax.nn.gelu(cut("gmlp_uv"), approximate=False)
    u, vb = uv[:, :gmlp], uv[:, gmlp:]
    group_dim = gmlp // GMLP_GROUPS
    vb = _layer_norm(vb, p["gmlp_ln_g"][layer]).reshape(t_len // GMLP_BLOCK, GMLP_BLOCK, GMLP_GROUPS, group_dim)
    pos = jnp.arange(GMLP_BLOCK) // CHUNK
    ws = jnp.where((pos[None, :] <= pos[:, None])[None], p["w_spatial"][layer], 0.0)
    s = jnp.einsum("gts,nsgc->ntgc", ws.astype(BF16), vb.astype(BF16), preferred_element_type=F32)
    s = s + jnp.transpose(p["b_spatial"][layer])[None, :, :, None]
    branch_b = _linear(u * s.reshape(t_len, gmlp), p["w_out_b"][layer], "w_out_branch", "col")

    sba = cut("sba_qkv")
    qc = _rms_norm(per_head(sba[:, :mixer]), p["sba_q_g"][layer]).reshape(t_len, mixer)
    kc = _rms_norm(per_head(sba[:, mixer:2 * mixer]), p["sba_k_g"][layer]).reshape(t_len, mixer)
    oc = _stick_breaking(qc, kc, sba[:, 2 * mixer:])
    branch_c = _linear(oc, p["w_out_c"][layer], "w_out_branch", "col")

    gates = jax.nn.sigmoid(cut("gates"))
    y = (gates[:, :d_model] * branch_a + gates[:, d_model:2 * d_model] * branch_b
         + gates[:, 2 * d_model:] * branch_c)
    return _linear(y, p["w_out"][layer], "w_out", "row")


def _local_loss(x, p, target):
    depth = len(p["w_in"])
    for layer in range(depth):
        x = x + _mixer(_rms_norm(x, p["norm_mix_g"][layer]), p, layer)
        f = _linear(_rms_norm(x, p["norm_mlp_g"][layer]), p["w_ff1"][layer], "w_ff1", "col")
        x = x + _linear(jnp.square(jax.nn.relu(f)), p["w_ff2"][layer], "w_ff2", "row")
    err = jnp.square(x - target)
    return 0.5 * jnp.sum(jnp.mean(err, axis=-1))


ANY = pl.BlockSpec(memory_space=pl.ANY)


def _place():
    x, y, c = lax.axis_index("x"), lax.axis_index("y"), lax.axis_index("c")
    other_chips = [(1 - x, y), (x, 1 - y), (1 - x, 1 - y)]
    return x, y, c, other_chips


def _my_chip():
    return 2 * lax.axis_index("x") + lax.axis_index("y")


def _cast_to_slot(w, layer):
    _, rows, cols = w.shape
    tr = _row_tile(rows, cols, 16)

    def body(w_ref, o_ref):
        o_ref[...] = w_ref[...].astype(BF16)

    return pl.pallas_call(
        body,
        name="cast_to_slot",
        grid=(rows // tr,),
        in_specs=[pl.BlockSpec((None, tr, cols), lambda r: (layer, r, 0))],
        out_specs=pl.BlockSpec((None, tr, cols), lambda r: (_my_chip(), r, 0)),
        out_shape=jax.ShapeDtypeStruct((N_CHIPS, rows, cols), BF16),
        compiler_params=_params("parallel"),
    )(w)


def _remote(src, dst, sems, a, k, to):
    send_sems, recv_sems = sems
    return pltpu.make_async_remote_copy(src_ref=src, dst_ref=dst, send_sem=send_sems.at[a, k],
                                        recv_sem=recv_sems.at[a, k], device_id=to, device_id_type=MESH)


def _sem_pairs(n_arrays, n_copies):
    return [pltpu.SemaphoreType.DMA((n_arrays, n_copies)), pltpu.SemaphoreType.DMA((n_arrays, n_copies))]


def _gather_chips(bufs):
    n = len(bufs)

    def body(*refs):
        outs, sems = refs[n:2 * n], refs[2 * n:]
        x, y, c, _ = _place()
        x_nbr, y_nbr, diag, sibling = (1 - x, y), (x, 1 - y), (1 - x, 1 - y), (x, y, 1 - c)

        def piece(a, chip, core, quarter=None):
            half = bufs[a].shape[1] // 2
            rows = pl.ds(core * half, half) if quarter is None else pl.ds(core * half + quarter * (half // 2), half // 2)
            return outs[a].at[2 * chip[0] + chip[1], rows, :]

        def copy(a, k, ref, to):
            return _remote(ref, ref, sems, a, k, to)

        def sends(a):
            return [(0, piece(a, (x, y), c), (*x_nbr, c)),
                    (1, piece(a, (x, y), c), (*y_nbr, c)),
                    (2, piece(a, x_nbr, c, 0), (*y_nbr, c)),
                    (3, piece(a, y_nbr, c, 1), (*x_nbr, c)),
                    (4, piece(a, x_nbr, c), sibling),
                    (5, piece(a, y_nbr, c), sibling),
                    (6, piece(a, diag, c, 0), sibling),
                    (7, piece(a, diag, c, 1), sibling)]

        def lands(a):
            return [piece(a, x_nbr, c), piece(a, y_nbr, c), piece(a, diag, c, 0), piece(a, diag, c, 1),
                    piece(a, x_nbr, 1 - c), piece(a, y_nbr, 1 - c), piece(a, diag, 1 - c, 0), piece(a, diag, 1 - c, 1)]

        started = []

        def start(a, k):
            _, ref, to = sends(a)[k]
            started.append(copy(a, k, ref, to))
            started[-1].start()

        def arrived(a, k):
            copy(a, k, lands(a)[k], sibling).wait_recv()

        for a in range(n):
            start(a, 0)
            start(a, 1)
        for k, onward in ((0, (2, 4)), (1, (3, 5)), (2, (6,)), (3, (7,))):
            for a in range(n):
                arrived(a, k)
                for nxt in onward:
                    start(a, nxt)
        for a in range(n):
            for k in (4, 5, 6, 7):
                arrived(a, k)
        for cp in started:
            cp.wait_send()

    return pl.pallas_call(
        body,
        name="gather_chips",
        in_specs=[ANY] * n,
        out_specs=[ANY] * n,
        out_shape=[jax.ShapeDtypeStruct(b.shape, b.dtype) for b in bufs],
        input_output_aliases={a: a for a in range(n)},
        scratch_shapes=_sem_pairs(n, 8),
    )(*bufs)


def _pair_exchange(grads):
    n = len(grads)

    def body(*refs):
        ins, outs, sems = refs[:n], refs[n:2 * n], refs[2 * n:]
        x, y, c, _ = _place()
        copies = []
        for a in range(n):
            rows = grads[a].shape[1] // 2
            copies.append(_remote(ins[a].at[:, pl.ds((1 - c) * rows, rows), :], outs[a], sems, a, 0, (x, y, 1 - c)))
        for cp in copies:
            cp.start()
        for cp in copies:
            cp.wait()

    return pl.pallas_call(
        body,
        name="pair_exchange",
        in_specs=[ANY] * n,
        out_specs=[ANY] * n,
        out_shape=[jax.ShapeDtypeStruct((g.shape[0], g.shape[1] // 2, g.shape[2]), g.dtype) for g in grads],
        scratch_shapes=_sem_pairs(n, 1),
    )(*grads)


def _chip_exchange(parts):
    n = len(parts)

    def body(*refs):
        ins, outs, sems = refs[:n], refs[n:2 * n], refs[2 * n:]
        x, y, c, other_chips = _place()
        me = 2 * x + y
        copies = [_remote(ins[a].at[2 * chip[0] + chip[1]], outs[a].at[me], sems, a, j, (*chip, c))
                  for a in range(n) for j, chip in enumerate(other_chips)]
        for cp in copies:
            cp.start()
        for cp in copies:
            cp.wait()

    return pl.pallas_call(
        body,
        name="chip_exchange",
        in_specs=[ANY] * n,
        out_specs=[ANY] * n,
        out_shape=[jax.ShapeDtypeStruct(p.shape, p.dtype) for p in parts],
        scratch_shapes=_sem_pairs(n, 3),
    )(*parts)


def _pair_share(halves):
    n = len(halves)

    def body(*refs):
        outs, sems = refs[n:2 * n], refs[2 * n:]
        x, y, c, _ = _place()
        copies = []
        for a in range(n):
            rows = halves[a].shape[0] // 2
            mine = outs[a].at[pl.ds(c * rows, rows), :]
            copies.append(_remote(mine, mine, sems, a, 0, (x, y, 1 - c)))
        for cp in copies:
            cp.start()
        for cp in copies:
            cp.wait()

    return pl.pallas_call(
        body,
        name="pair_share",
        in_specs=[ANY] * n,
        out_specs=[ANY] * n,
        out_shape=[jax.ShapeDtypeStruct(h.shape, h.dtype) for h in halves],
        input_output_aliases={a: a for a in range(n)},
        scratch_shapes=_sem_pairs(n, 1),
    )(*halves)


def _all_gather_rows(block):
    m_per, n = block.shape

    def body(x_ref, out_ref, send_sems, recv_sems, local_sem):
        x, y, c, other_chips = _place()
        me, sibling = (x, y, c), (x, y, 1 - c)

        def rows(px, py, pc):
            return out_ref.at[pl.ds((4 * px + 2 * py + pc) * m_per, m_per), :]

        def copy(k, blk, to, src=None):
            return pltpu.make_async_remote_copy(src_ref=rows(*blk) if src is None else src, dst_ref=rows(*blk),
                                                send_sem=send_sems.at[k], recv_sem=recv_sems.at[k],
                                                device_id=to, device_id_type=MESH)

        mine = pltpu.make_async_copy(x_ref, rows(*me), local_sem)
        mine.start()
        first = [copy(0, me, sibling, src=x_ref)]
        first += [copy(1 + j, me, (*chip, c), src=x_ref) for j, chip in enumerate(other_chips)]
        for cp in first:
            cp.start()
        passed = [copy(4 + j, (*chip, c), sibling) for j, chip in enumerate(other_chips)]
        for j, chip in enumerate(other_chips):
            copy(1 + j, (*chip, c), me).wait_recv()
            passed[j].start()
        copy(0, sibling, me).wait_recv()
        for j, chip in enumerate(other_chips):
            copy(4 + j, (*chip, 1 - c), me).wait_recv()
        for cp in first + passed:
            cp.wait_send()
        mine.wait()

    return pl.pallas_call(
        body,
        name="all_gather_rows",
        in_specs=[ANY],
        out_specs=ANY,
        out_shape=jax.ShapeDtypeStruct((N_DEV * m_per, n), block.dtype),
        scratch_shapes=[pltpu.SemaphoreType.DMA((7,)), pltpu.SemaphoreType.DMA((7,)), pltpu.SemaphoreType.DMA],
    )(block)


def _row_tile(rows, cols, multiple):
    budget = max(multiple, ELEMENTWISE_BLOCK_ELEMS // _round_up(cols, LANES))
    t = multiple
    while t * 2 <= budget and rows % (t * 2) == 0:
        t *= 2
    return t if rows % t == 0 else rows


def _pair_sum(grad, theirs):
    chips, rows, cols = grad.shape
    half = rows // 2
    tr = _row_tile(half, cols, 16)
    blocks = half // tr

    def body(a_ref, b_ref, o_ref):
        o_ref[...] = (a_ref[...].astype(F32) + b_ref[...].astype(F32)).astype(o_ref.dtype)

    return pl.pallas_call(
        body,
        name="pair_sum",
        grid=(chips, blocks),
        in_specs=[pl.BlockSpec((None, tr, cols), lambda s, r: (s, lax.axis_index("c") * blocks + r, 0)),
                  pl.BlockSpec((None, tr, cols), lambda s, r: (s, r, 0))],
        out_specs=pl.BlockSpec((None, tr, cols), lambda s, r: (s, r, 0)),
        out_shape=jax.ShapeDtypeStruct((chips, half, cols), grad.dtype),
        compiler_params=_params("parallel", "parallel"),
    )(grad, theirs)


def _chip_sum(mine, others):
    chips, half, cols = mine.shape
    tr = _row_tile(half, cols, 16)
    blocks = half // tr

    def body(own_ref, *refs):
        o_ref = refs[-1]
        total = own_ref[...].astype(F32)
        for ref in refs[:-1]:
            total = total + ref[...].astype(F32)
        o_ref[...] = total

    slot = lambda q: pl.BlockSpec((None, tr, cols), lambda r: ((_my_chip() + q) % chips, r, 0))
    return pl.pallas_call(
        body,
        name="chip_sum",
        grid=(blocks,),
        in_specs=[slot(q) for q in range(chips)],
        out_specs=pl.BlockSpec((tr, cols), lambda r: (lax.axis_index("c") * blocks + r, 0)),
        out_shape=jax.ShapeDtypeStruct((2 * half, cols), F32),
        compiler_params=_params("parallel"),
    )(mine, *([others] * (chips - 1)))


def _device_sum(blocks, m_per):
    tr = _row_tile(m_per, LANES, 8)
    per = m_per // tr

    def body(*refs):
        o_ref = refs[-1]
        total = refs[0][...]
        for ref in refs[1:-1]:
            total = total + ref[...]
        o_ref[...] = total

    return pl.pallas_call(
        body,
        name="device_sum",
        grid=(per,),
        in_specs=[pl.BlockSpec((tr, LANES), functools.partial(lambda d, r: (d * per + r, 0), d)) for d in range(N_DEV)],
        out_specs=pl.BlockSpec((tr, LANES), lambda r: (r, 0)),
        out_shape=jax.ShapeDtypeStruct((m_per, LANES), F32),
        compiler_params=_params("parallel"),
    )(*([blocks] * N_DEV))


def _adamw(grads, w, m, v):
    depth, rows, cols = w.shape
    tr = _row_tile(rows, cols, 8) if rows % 8 == 0 else rows
    spec = pl.BlockSpec((None, tr, cols), lambda l, r: (l, r, 0))
    grad_spec = lambda q: pl.BlockSpec((tr, cols), lambda l, r: (jnp.where(l == q, r, 0), 0))

    def body(*refs):
        g_refs, (w_ref, m_ref, v_ref, go_ref, d_ref, mo_ref, vo_ref) = refs[:depth], refs[depth:]
        layer = pl.program_id(0)
        g = g_refs[0][...]
        for q in range(1, depth):
            g = jnp.where(layer == q, g_refs[q][...], g)
        m_new = ADAM_B1 * m_ref[...] + (1.0 - ADAM_B1) * g
        v_new = ADAM_B2 * v_ref[...] + (1.0 - ADAM_B2) * jnp.square(g)
        m_hat = m_new / (1.0 - ADAM_B1 ** ADAM_STEP)
        v_hat = v_new / (1.0 - ADAM_B2 ** ADAM_STEP)
        go_ref[...] = g
        d_ref[...] = -ADAM_LR * (m_hat / (jnp.sqrt(v_hat) + ADAM_EPS) + ADAM_WD * w_ref[...])
        mo_ref[...] = m_new
        vo_ref[...] = v_new

    return pl.pallas_call(
        body,
        name="adamw",
        grid=(depth, rows // tr),
        in_specs=[grad_spec(q) for q in range(depth)] + [spec] * 3,
        out_specs=[spec] * 4,
        out_shape=[jax.ShapeDtypeStruct(w.shape, F32)] * 4,
        compiler_params=_params("parallel", "parallel"),
    )(*grads, w, m, v)


def _divisor_tile(dim, limit):
    t = max(1, min(dim, limit))
    while dim % t:
        t -= 1
    return t


def _adamw_columns_major(grads, w, m, v):
    depth, rows, cols = w.shape
    view = lambda t: jnp.transpose(t, (2, 0, 1))
    back = lambda t: jnp.transpose(t, (1, 2, 0))
    grad = jnp.stack([jnp.transpose(g) for g in grads], axis=1)
    tc = _divisor_tile(cols, max(1, ELEMENTWISE_BLOCK_ELEMS // (2 * depth * rows)))
    spec = pl.BlockSpec((tc, depth, rows), lambda c: (c, 0, 0))

    def body(g_ref, w_ref, m_ref, v_ref, d_ref, mo_ref, vo_ref):
        g = g_ref[...]
        m_new = ADAM_B1 * m_ref[...] + (1.0 - ADAM_B1) * g
        v_new = ADAM_B2 * v_ref[...] + (1.0 - ADAM_B2) * jnp.square(g)
        m_hat = m_new / (1.0 - ADAM_B1 ** ADAM_STEP)
        v_hat = v_new / (1.0 - ADAM_B2 ** ADAM_STEP)
        d_ref[...] = -ADAM_LR * (m_hat / (jnp.sqrt(v_hat) + ADAM_EPS) + ADAM_WD * w_ref[...])
        mo_ref[...] = m_new
        vo_ref[...] = v_new

    delta, m_new, v_new = pl.pallas_call(
        body,
        name="adamw_columns_major",
        grid=(cols // tc,),
        in_specs=[spec] * 4,
        out_specs=[spec] * 3,
        out_shape=[jax.ShapeDtypeStruct((cols, depth, rows), F32)] * 3,
        compiler_params=_params("parallel"),
    )(grad, view(w), view(m), view(v))
    return back(grad), back(delta), back(m_new), back(v_new)


def _join_cols(gathered):
    return jnp.concatenate([gathered[s] for s in range(N_CHIPS)], axis=-1)


def _reduce_scatter(grads):
    theirs = _pair_exchange(grads)
    pairs = [_pair_sum(g, t) for g, t in zip(grads, theirs)]
    others = _chip_exchange(pairs)
    return _pair_share([_chip_sum(p, o) for p, o in zip(pairs, others)])


def _pack_rows(arrays):
    parts, counts = [], []
    for a in arrays:
        flat = a.reshape(-1).astype(F32)
        n_rows = _round_up(flat.shape[0], 8 * LANES) // LANES
        parts.append(jnp.pad(flat, (0, n_rows * LANES - flat.shape[0])).reshape(n_rows, LANES))
        counts.append(n_rows)
    return jnp.concatenate(parts, axis=0), counts


def _pad_rows(a, rows):
    return jnp.pad(a, ((0, rows - a.shape[0]), (0, 0)))


def _unpack_rows(packed, counts, shapes):
    out, row = [], 0
    for n_rows, shape in zip(counts, shapes):
        size = 1
        for d in shape:
            size *= d
        out.append(packed[row:row + n_rows].reshape(-1)[:size].reshape(shape))
        row += n_rows
    return out


SHARDED = (("w_in", "col"), ("w_out_a", "col"), ("w_out_b", "col"), ("w_out_c", "col"), ("w_out", "row"),
           ("w_ff1", "col"), ("w_ff2", "row"))


def kernel(x, w_in, conv_w, a_log, dt_bias, gdn_norm_g, gmlp_ln_g, w_spatial, b_spatial, sba_q_g, sba_k_g, w_out_a, w_out_b, w_out_c, w_out, norm_mix_g, norm_mlp_g, w_ff1, w_ff2, loss_target, m_w_in, m_conv_w, m_a_log, m_dt_bias, m_gdn_norm_g, m_gmlp_ln_g, m_w_spatial, m_b_spatial, m_sba_q_g, m_sba_k_g, m_w_out_a, m_w_out_b, m_w_out_c, m_w_out, m_norm_mix_g, m_norm_mlp_g, m_w_ff1, m_w_ff2, v_w_in, v_conv_w, v_a_log, v_dt_bias, v_gdn_norm_g, v_gmlp_ln_g, v_w_spatial, v_b_spatial, v_sba_q_g, v_sba_k_g, v_w_out_a, v_w_out_b, v_w_out_c, v_w_out, v_norm_mix_g, v_norm_mlp_g, v_w_ff1, v_w_ff2):
    given = dict(locals())
    weights = {n: given[n] for n in WEIGHT_NAMES}
    depth, d_model = w_in.shape[0], w_in.shape[1]
    layers = range(depth)
    chip = _my_chip()

    gathered = [_gather_chips([_cast_to_slot(weights[name], l) for name, _ in SHARDED]) for l in layers]
    conv_rows = depth * CONV_WIDTH
    conv_cols = conv_w.shape[-1]
    conv_all = _all_gather_rows(_pad_rows(conv_w.reshape(conv_rows, conv_cols), _round_up(conv_rows, 8)))
    conv_all = conv_all.reshape(N_CHIPS, 2, _round_up(conv_rows, 8), conv_cols)[:, 0, :conv_rows]
    conv_full = jnp.concatenate([conv_all[s] for s in range(N_CHIPS)], axis=-1).reshape(depth, CONV_WIDTH, N_CHIPS * conv_cols)

    params = {n: weights[n] for n in REPLICATED_NAMES}
    params["conv_w"] = conv_full
    for i, (name, _) in enumerate(SHARDED):
        params[name] = [gathered[l][i] for l in layers]
    params["w_in"] = [_pack_w_in(_join_cols(gathered[l][0]), d_model) for l in layers]

    loss_local, (grad_x, grads) = jax.value_and_grad(_local_loss, argnums=(0, 1))(x[0], params, loss_target[0])

    summed = []
    for l in layers:
        local = [grads[name][l] for name, _ in SHARDED]
        local[0] = _unpack_w_in_shards(local[0], d_model)
        summed.append(_reduce_scatter(local))
    rep_packed, rep_counts = _pack_rows([grads[n] for n in REPLICATED_NAMES])
    rep_rows = _round_up(rep_packed.shape[0], SMALL_ROW_ALIGN)
    rest_packed, rest_counts = _pack_rows([grads["conv_w"], loss_local])
    m_per = rep_rows + _round_up(rest_packed.shape[0], SMALL_ROW_ALIGN)
    packed = jnp.concatenate([_pad_rows(rep_packed, rep_rows), _pad_rows(rest_packed, m_per - rep_rows)], axis=0)
    total = _device_sum(_all_gather_rows(packed), m_per)
    conv_sum, loss = _unpack_rows(total[rep_rows:], rest_counts, [grads["conv_w"].shape, ()])
    conv_grad = lax.dynamic_slice_in_dim(conv_sum, chip * conv_cols, conv_cols, axis=2)

    out = {}
    for i, (name, _) in enumerate(SHARDED):
        update = _adamw_columns_major if weights[name].shape[-1] % LANES else _adamw
        out[name] = update([summed[l][i] for l in layers], weights[name], given["m_" + name], given["v_" + name])
    out["conv_w"] = _adamw([conv_grad[l] for l in layers], conv_w, m_conv_w, v_conv_w)
    rep_shapes = [weights[n].shape for n in REPLICATED_NAMES]
    pack3 = lambda prefix: _pad_rows(_pack_rows([given[prefix + n] for n in REPLICATED_NAMES])[0], rep_rows)[None]
    rep = _adamw([total[:rep_rows]], pack3(""), pack3("m_"), pack3("v_"))
    rep = [_unpack_rows(t[0], rep_counts, rep_shapes) for t in rep]
    for i, name in enumerate(REPLICATED_NAMES):
        out[name] = tuple(t[i] for t in rep)

    results = [loss, grad_x[None]]
    for kind in range(4):
        results += [out[n][kind] for n in WEIGHT_NAMES]
    return tuple(results)
```

```python
import functools

import jax
import jax.numpy as jnp
from jax import lax
from jax.experimental import pallas as pl
from jax.experimental.pallas import tpu as pltpu

F32, BF16 = jnp.float32, jnp.bfloat16
MESH = pl.DeviceIdType.MESH
N_CHIPS = 4
N_DEV = 8

EPS = 1e-6
CHUNK = 64
HEAD_DIM = 128
CONV_WIDTH = 4
GMLP_GROUPS = 8
GMLP_BLOCK = 128
N_BRANCHES = 3
ADAM_LR, ADAM_B1, ADAM_B2, ADAM_EPS, ADAM_WD, ADAM_STEP = 0.001, 0.9, 0.999, 1e-08, 0.01, 10

V7X_VMEM_BYTES = 64 * 1024 * 1024
VMEM_LIMIT = V7X_VMEM_BYTES - 8 * 1024 * 1024
LANES = 128
MATMUL_TILE = 1024
MATMUL_K_TILE = 2048
ATTN_TILE = 256
GDN_HEADS_PER_STEP = 8
SBA_HEADS_PER_STEP = 4
GDN_SEGMENT = 512
ELEMENTWISE_BLOCK_ELEMS = 512 * 1024
SMALL_ROW_ALIGN = 256

WEIGHT_NAMES = ("w_in", "conv_w", "a_log", "dt_bias", "gdn_norm_g", "gmlp_ln_g", "w_spatial", "b_spatial",
                "sba_q_g", "sba_k_g", "w_out_a", "w_out_b", "w_out_c", "w_out", "norm_mix_g", "norm_mlp_g",
                "w_ff1", "w_ff2")
REPLICATED_NAMES = ("a_log", "dt_bias", "gdn_norm_g", "gmlp_ln_g", "w_spatial", "b_spatial", "sba_q_g",
                    "sba_k_g", "norm_mix_g", "norm_mlp_g")


def _round_up(n, m):
    return (n + m - 1) // m * m


def _tile(dim, pref):
    if dim <= pref:
        return dim
    t = pref // LANES * LANES
    while t >= LANES:
        if dim % t == 0:
            return t
        t -= LANES
    return dim


def _params(*semantics):
    return pltpu.CompilerParams(dimension_semantics=semantics, vmem_limit_bytes=VMEM_LIMIT)


_DOT_DIMS = {"nn": ((1,), (0,)), "nt": ((1,), (1,)), "tn": ((0,), (0,))}


def _logical_shape(shape, kind):
    if kind is None:
        return shape
    chips, rows, cols = shape
    return (rows, chips * cols) if kind == "col" else (chips * rows, cols)


def _weight_tiles(shape, kind, row_pref, col_pref):
    rows, cols = _logical_shape(shape, kind)
    tr = _tile(shape[1] if kind == "row" else rows, row_pref)
    tc = _tile(shape[2] if kind == "col" else cols, col_pref)
    return tr, tc


def _weight_spec(shape, kind, tr, tc, pick):
    if kind is None:
        return pl.BlockSpec((tr, tc), pick)
    if kind == "col":
        per = shape[2] // tc
        return pl.BlockSpec((None, tr, tc), lambda i, j, k: (pick(i, j, k)[1] // per, pick(i, j, k)[0], pick(i, j, k)[1] % per))
    per = shape[1] // tr
    return pl.BlockSpec((None, tr, tc), lambda i, j, k: (pick(i, j, k)[0] // per, pick(i, j, k)[0] % per, pick(i, j, k)[1]))


def _matmul(a, b, mode, out_dtype, name, kind=None, out_shape=None):
    if mode == "tn":
        k_dim, m_dim = a.shape
        n_dim = b.shape[1]
        out_shape = (m_dim, n_dim) if kind is None else out_shape
        tm, tn = _weight_tiles(out_shape, kind, MATMUL_TILE, MATMUL_TILE)
        tk = _tile(k_dim, MATMUL_K_TILE)
        a_spec = pl.BlockSpec((tk, tm), lambda i, j, k: (k, i))
        b_spec = pl.BlockSpec((tk, tn), lambda i, j, k: (k, j))
        out_spec = _weight_spec(out_shape, kind, tm, tn, lambda i, j, k: (i, j))
    else:
        m_dim, k_dim = a.shape
        tm = _tile(m_dim, MATMUL_TILE)
        if mode == "nn":
            _, n_dim = _logical_shape(b.shape, kind)
            tk, tn = _weight_tiles(b.shape, kind, MATMUL_K_TILE, MATMUL_TILE)
            b_spec = _weight_spec(b.shape, kind, tk, tn, lambda i, j, k: (k, j))
        else:
            n_dim, _ = _logical_shape(b.shape, kind)
            tn, tk = _weight_tiles(b.shape, kind, MATMUL_TILE, MATMUL_K_TILE)
            b_spec = _weight_spec(b.shape, kind, tn, tk, lambda i, j, k: (j, k))
        a_spec = pl.BlockSpec((tm, tk), lambda i, j, k: (i, k))
        out_shape = (m_dim, n_dim)
        out_spec = pl.BlockSpec((tm, tn), lambda i, j, k: (i, j))
    nk = k_dim // tk
    dims = (_DOT_DIMS[mode], ((), ()))

    def body(a_ref, b_ref, o_ref, *scratch):
        part = lax.dot_general(a_ref[...], b_ref[...], dims, preferred_element_type=F32)
        if nk == 1:
            o_ref[...] = part.astype(out_dtype)
            return
        acc_ref, = scratch
        k = pl.program_id(2)

        @pl.when(k == 0)
        def _():
            acc_ref[...] = part

        @pl.when(k > 0)
        def _():
            acc_ref[...] += part

        @pl.when(k == nk - 1)
        def _():
            o_ref[...] = acc_ref[...].astype(out_dtype)

    return pl.pallas_call(
        body,
        name=name,
        grid=(m_dim // tm, n_dim // tn, nk),
        in_specs=[a_spec, b_spec],
        out_specs=out_spec,
        out_shape=jax.ShapeDtypeStruct(out_shape, out_dtype),
        scratch_shapes=[] if nk == 1 else [pltpu.VMEM((tm, tn), F32)],
        compiler_params=_params("parallel", "parallel", "arbitrary"),
    )(a, b)


@functools.partial(jax.custom_vjp, nondiff_argnums=(2, 3))
def _linear(a, w, name, kind=None):
    return _matmul(a.astype(BF16), w, "nn", F32, name + "_fwd", kind)


def _linear_fwd(a, w, name, kind):
    a16 = a.astype(BF16)
    return _matmul(a16, w, "nn", F32, name + "_fwd", kind), (a16, w)


def _linear_bwd(name, kind, res, g):
    a16, w = res
    g16 = g.astype(BF16)
    da = _matmul(g16, w, "nt", F32, name + "_dgrad", kind)
    dw = _matmul(a16, g16, "tn", BF16, name + "_wgrad", kind, w.shape)
    return da, dw


_linear.defvjp(_linear_fwd, _linear_bwd)


def _split3_dots(xs, ones_mat):
    his = [x.astype(BF16) for x in xs]
    rests = [x - hi.astype(F32) for x, hi in zip(xs, his)]
    mids = [rest.astype(BF16) for rest in rests]
    los = [(rest - mid.astype(F32)).astype(BF16) for rest, mid in zip(rests, mids)]
    dots = lambda parts: [jnp.dot(p, ones_mat, preferred_element_type=F32) for p in parts]
    return [a + b + c for a, b, c in zip(dots(his), dots(mids), dots(los))]


def _dot_nt(a, b):
    return lax.dot_general(a, b, (((1,), (1,)), ((), ())), preferred_element_type=F32)


def _dot_tn(a, b):
    return lax.dot_general(a, b, (((0,), (0,)), ((), ())), preferred_element_type=F32)


def _sba_scores(q16, k16, tile, scale, diagonal):
    z = _dot_nt(q16, k16) * scale
    log_sig = jnp.minimum(z, 0.0) - jnp.log(1.0 + jnp.exp(-jnp.abs(z)))
    if not diagonal:
        return None, log_sig, log_sig - z
    mask = lax.broadcasted_iota(jnp.int32, (tile, tile), 1) < lax.broadcasted_iota(jnp.int32, (tile, tile), 0)
    return mask, log_sig, jnp.where(mask, log_sig - z, 0.0)


def _masked(mask, value):
    return value if mask is None else jnp.where(mask, value, 0.0)


def _sba_layout(q):
    t_len, width = q.shape
    heads = width // HEAD_DIM
    group = min(SBA_HEADS_PER_STEP, heads)
    tile = min(ATTN_TILE, t_len)
    q_spec = pl.BlockSpec((tile, group * HEAD_DIM), lambda h, i: (i, h))
    kv_spec = pl.BlockSpec((t_len, group * HEAD_DIM), lambda h, i: (0, h))
    total_spec = pl.BlockSpec((group, tile, 1), lambda h, i: (h, i, 0))
    return heads, group, tile, (heads // group, t_len // tile), q_spec, kv_spec, total_spec


def _sba_forward(q, k, v):
    t_len, width = q.shape
    heads, group, tile, grid, q_spec, kv_spec, total_spec = _sba_layout(q)
    scale = HEAD_DIM ** -0.5
    cols = [slice(h * HEAD_DIM, (h + 1) * HEAD_DIM) for h in range(group)]

    def body(q_ref, k_ref, v_ref, o_ref, r_ref):
        i = pl.program_id(1)
        q16 = [q_ref[:, c].astype(BF16) for c in cols]
        ri = lax.broadcasted_iota(jnp.int32, (tile, tile), 0)
        ci = lax.broadcasted_iota(jnp.int32, (tile, tile), 1)
        later = (ri > ci).astype(BF16)

        def tile_step(j, carry, diagonal):
            runs, accs = carry
            rows = pl.ds(pl.multiple_of(j * tile, tile), tile)
            k16 = [k_ref[rows, c].astype(BF16) for c in cols]
            v16 = [v_ref[rows, c].astype(BF16) for c in cols]
            scores = [_sba_scores(q16[h], k16[h], tile, scale, diagonal) for h in range(group)]
            within = _split3_dots([log_keep for _, _, log_keep in scores], later)
            atts = [_masked(scores[h][0], jnp.exp(scores[h][1] + within[h] + runs[h])) for h in range(group)]
            accs = [accs[h] + jnp.dot(atts[h].astype(BF16), v16[h], preferred_element_type=F32) for h in range(group)]
            runs = [runs[h] + jnp.sum(scores[h][2], axis=1, keepdims=True) for h in range(group)]
            return runs, accs

        zeros = lambda width: [jnp.zeros((tile, width), F32) for _ in range(group)]
        carry = tile_step(i, (zeros(1), zeros(HEAD_DIM)), True)
        runs, accs = lax.fori_loop(0, i, lambda jj, carry: tile_step(i - 1 - jj, carry, False), carry)
        for h in range(group):
            o_ref[:, cols[h]] = accs[h]
            r_ref[h] = runs[h]

    return pl.pallas_call(
        body,
        name="sba_fwd",
        grid=grid,
        in_specs=[q_spec, kv_spec, kv_spec],
        out_specs=[q_spec, total_spec],
        out_shape=[jax.ShapeDtypeStruct((t_len, width), F32), jax.ShapeDtypeStruct((heads, t_len, 1), F32)],
        compiler_params=_params("parallel", "arbitrary"),
    )(q, k, v)


def _sba_backward(q, k, v, total, do):
    t_len, width = q.shape
    heads, group, tile, grid, q_spec, kv_spec, total_spec = _sba_layout(q)
    scale = HEAD_DIM ** -0.5
    cols = [slice(h * HEAD_DIM, (h + 1) * HEAD_DIM) for h in range(group)]

    def body(q_ref, k_ref, v_ref, r_ref, do_ref, dq_ref, dk_ref, dv_ref):
        i = pl.program_id(1)

        @pl.when(i == 0)
        def _():
            dk_ref[...] = jnp.zeros_like(dk_ref)
            dv_ref[...] = jnp.zeros_like(dv_ref)

        q16 = [q_ref[:, c].astype(BF16) for c in cols]
        do16 = [do_ref[:, c].astype(BF16) for c in cols]
        tot = [r_ref[h] for h in range(group)]
        ri = lax.broadcasted_iota(jnp.int32, (tile, tile), 0)
        ci = lax.broadcasted_iota(jnp.int32, (tile, tile), 1)
        upto = (ri <= ci).astype(BF16)
        before = (ri < ci).astype(BF16)

        def tile_step(j, carry, diagonal):
            keep_left, w_left, dq = carry
            rows = pl.ds(pl.multiple_of(j * tile, tile), tile)
            heads_here = range(group)
            k16 = [k_ref[rows, c].astype(BF16) for c in cols]
            v16 = [v_ref[rows, c].astype(BF16) for c in cols]
            scores = [_sba_scores(q16[h], k16[h], tile, scale, diagonal) for h in heads_here]
            within = _split3_dots([log_keep for _, _, log_keep in scores], upto)
            atts = [_masked(scores[h][0], jnp.exp(scores[h][1] + tot[h] - keep_left[h] - within[h])) for h in heads_here]
            d_att = [_dot_nt(do16[h], v16[h]) for h in heads_here]
            w = [atts[h] * d_att[h] for h in heads_here]
            w_before = _split3_dots(w, before)
            sig = [jnp.exp(scores[h][1]) for h in heads_here]
            dz16 = [(_masked(scores[h][0], w[h] * (1.0 - sig[h]) - sig[h] * (w_left[h] + w_before[h])) * scale).astype(BF16)
                    for h in heads_here]
            dq = [dq[h] + jnp.dot(dz16[h], k16[h], preferred_element_type=F32) for h in heads_here]
            dk = [_dot_tn(dz16[h], q16[h]) for h in heads_here]
            dv = [_dot_tn(atts[h].astype(BF16), do16[h]) for h in heads_here]
            for h in heads_here:
                dk_ref[rows, cols[h]] += dk[h]
                dv_ref[rows, cols[h]] += dv[h]
            keep_left = [keep_left[h] + jnp.sum(scores[h][2], axis=1, keepdims=True) for h in heads_here]
            w_left = [w_left[h] + jnp.sum(w[h], axis=1, keepdims=True) for h in heads_here]
            return keep_left, w_left, dq

        zeros = lambda width: [jnp.zeros((tile, width), F32) for _ in range(group)]
        carry = lax.fori_loop(0, i, lambda j, carry: tile_step(j, carry, False), (zeros(1), zeros(1), zeros(HEAD_DIM)))
        _, _, dq = tile_step(i, carry, True)
        for h in range(group):
            dq_ref[:, cols[h]] = dq[h]

    return pl.pallas_call(
        body,
        name="sba_bwd",
        grid=grid,
        in_specs=[q_spec, kv_spec, kv_spec, total_spec, q_spec],
        out_specs=[q_spec, kv_spec, kv_spec],
        out_shape=[jax.ShapeDtypeStruct((t_len, width), F32)] * 3,
        compiler_params=_params("parallel", "arbitrary"),
    )(q, k, v, total, do)


@jax.custom_vjp
def _stick_breaking(q, k, v):
    return _sba_forward(q, k, v)[0]


def _stick_breaking_fwd(q, k, v):
    o, total = _sba_forward(q, k, v)
    return o, (q, k, v, total)


def _stick_breaking_bwd(res, do):
    q, k, v, total = res
    return tuple(_sba_backward(q, k, v, total, do))


_stick_breaking.defvjp(_stick_breaking_fwd, _stick_breaking_bwd)


def _contract(a16, b16, ca, cb):
    return lax.dot_general(a16, b16, (((ca,), (cb,)), ((), ())), preferred_element_type=F32)


def _pdot_raw(a, b, ca, cb):
    a_hi = a.astype(BF16)
    a_lo = (a - a_hi.astype(F32)).astype(BF16)
    b_hi = b.astype(BF16)
    b_lo = (b - b_hi.astype(F32)).astype(BF16)
    return _contract(a_hi, b_hi, ca, cb) + _contract(a_hi, b_lo, ca, cb) + _contract(a_lo, b_hi, ca, cb)


def _bdot_raw(a, b, ca, cb):
    return _contract(a.astype(BF16), b.astype(BF16), ca, cb)


def _make_dot(raw):
    @functools.partial(jax.custom_vjp, nondiff_argnums=(2, 3))
    def dot(a, b, ca, cb):
        return raw(a, b, ca, cb)

    def fwd(a, b, ca, cb):
        return raw(a, b, ca, cb), (a, b)

    def bwd(ca, cb, res, g):
        a, b = res
        da = raw(g, b, 1, 1 - cb) if ca == 1 else raw(b, g, 1 - cb, 1)
        db = raw(a, g, 1 - ca, 0) if cb == 0 else raw(g, a, 0, 1 - ca)
        return da, db

    dot.defvjp(fwd, bwd)
    return dot


_bdot = _make_dot(_bdot_raw)


@jax.custom_vjp
def _unit_lower_inverses(lows):
    size = lows[0].shape[0]
    eye = (lax.broadcasted_iota(jnp.int32, (size, size), 0) == lax.broadcasted_iota(jnp.int32, (size, size), 1)).astype(F32)
    invs = [eye - low for low in lows]
    powers = list(lows)
    span = 2
    while span < size:
        powers = [_bdot_raw(p, p, 1, 0) for p in powers]
        invs = [inv + _bdot_raw(inv, p, 1, 0) for inv, p in zip(invs, powers)]
        span *= 2
    resids = [eye - inv - _pdot_raw(low, inv, 1, 0) for low, inv in zip(lows, invs)]
    return [inv + _bdot_raw(inv, r, 1, 0) for inv, r in zip(invs, resids)]


def _unit_lower_inverses_fwd(lows):
    invs = _unit_lower_inverses(lows)
    return invs, invs


def _unit_lower_inverses_bwd(invs, gs):
    left = [_bdot_raw(inv, g, 0, 0) for inv, g in zip(invs, gs)]
    return ([-_bdot_raw(l, inv, 1, 1) for l, inv in zip(left, invs)],)


_unit_lower_inverses.defvjp(_unit_lower_inverses_fwd, _unit_lower_inverses_bwd)


def _gdn_chunks(states, qs, ks, vs, g_rows, b_rows):
    heads = range(len(qs))
    size = qs[0].shape[0]
    ri = lax.broadcasted_iota(jnp.int32, (size, size), 0)
    ci = lax.broadcasted_iota(jnp.int32, (size, size), 1)
    eye, incl, strict = ri == ci, ci <= ri, ci < ri
    square = lambda t: jnp.broadcast_to(t, (size, size))
    g_rowb = [square(g) for g in g_rows]
    g_col = [jnp.sum(jnp.where(eye, g, 0.0), axis=1, keepdims=True) for g in g_rowb]
    b_col = [jnp.sum(jnp.where(eye, square(b), 0.0), axis=1, keepdims=True) for b in b_rows]
    gc_col = [jnp.sum(jnp.where(incl, g, 0.0), axis=1, keepdims=True) for g in g_rowb]
    gc_row = [jnp.sum(jnp.where(ri <= ci, square(g), 0.0), axis=0, keepdims=True) for g in g_col]
    g_last = [jnp.sum(g, axis=1, keepdims=True) for g in g_rows]
    decay = [jnp.where(incl, jnp.exp(jnp.where(incl, gc_col[h] - gc_row[h], 0.0)), 0.0) for h in heads]
    kb = [ks[h] * b_col[h] for h in heads]
    kk = [_bdot(kb[h], ks[h], 1, 1) for h in heads]
    invs = _unit_lower_inverses([jnp.where(strict, kk[h] * decay[h], 0.0) for h in heads])
    grow = [jnp.exp(gc_col[h]) for h in heads]
    u = [_bdot(invs[h], vs[h] * b_col[h], 1, 0) for h in heads]
    w = [_bdot(invs[h], kb[h] * grow[h], 1, 0) for h in heads]
    qk = [_bdot(qs[h], ks[h], 1, 1) for h in heads]
    ws = [_bdot(w[h], states[h], 1, 0) for h in heads]
    qs_state = [_bdot(qs[h] * grow[h], states[h], 1, 0) for h in heads]
    v_new = [u[h] - ws[h] for h in heads]
    inner = [_bdot(qk[h] * decay[h], v_new[h], 1, 0) for h in heads]
    k_dec = [ks[h] * jnp.exp(g_last[h] - gc_col[h]) for h in heads]
    kv = [_bdot(k_dec[h], v_new[h], 0, 0) for h in heads]
    new_states = [states[h] * jnp.exp(g_last[h]) + kv[h] for h in heads]
    return new_states, [qs_state[h] + inner[h] for h in heads]


def _gdn_layout(q):
    t_len, width = q.shape
    heads = width // HEAD_DIM
    group = min(GDN_HEADS_PER_STEP, heads)
    seg = min(GDN_SEGMENT, t_len)
    return heads, group, seg, (heads // group, t_len // seg)


def _gdn_specs(group, seg, order):
    chunks = seg // CHUNK
    seq = pl.BlockSpec((seg, group * HEAD_DIM), lambda h, t: (order(t), h))
    gate = pl.BlockSpec((group, chunks, CHUNK), lambda h, t: (h, order(t), 0))
    states = pl.BlockSpec((group, chunks, HEAD_DIM, HEAD_DIM), lambda h, t: (h, order(t), 0, 0))
    return seq, gate, states


def _gdn_forward(q, k, v, g, beta):
    t_len, width = q.shape
    heads, group, seg, grid = _gdn_layout(q)
    seq, gate, states = _gdn_specs(group, seg, lambda t: t)

    def body(q_ref, k_ref, v_ref, g_ref, b_ref, o_ref, s_ref, state_ref):
        @pl.when(pl.program_id(1) == 0)
        def _():
            state_ref[...] = jnp.zeros_like(state_ref)

        def step(n, carry):
            rows = pl.ds(pl.multiple_of(n * CHUNK, CHUNK), CHUNK)
            cols = [slice(j * HEAD_DIM, (j + 1) * HEAD_DIM) for j in range(group)]
            states = [state_ref[j] for j in range(group)]
            for j in range(group):
                s_ref[j, n] = states[j]
            states, outs = _gdn_chunks(states, [q_ref[rows, c] for c in cols], [k_ref[rows, c] for c in cols],
                                       [v_ref[rows, c] for c in cols], [g_ref[j, pl.ds(n, 1), :] for j in range(group)],
                                       [b_ref[j, pl.ds(n, 1), :] for j in range(group)])
            for j in range(group):
                o_ref[rows, cols[j]] = outs[j]
                state_ref[j] = states[j]
            return carry

        lax.fori_loop(0, seg // CHUNK, step, 0)

    return pl.pallas_call(
        body,
        name="gdn_fwd",
        grid=grid,
        in_specs=[seq, seq, seq, gate, gate],
        out_specs=[seq, states],
        out_shape=[jax.ShapeDtypeStruct((t_len, width), F32),
                   jax.ShapeDtypeStruct((heads, t_len // CHUNK, HEAD_DIM, HEAD_DIM), F32)],
        scratch_shapes=[pltpu.VMEM((group, HEAD_DIM, HEAD_DIM), F32)],
        compiler_params=_params("parallel", "arbitrary"),
    )(q, k, v, g, beta)


def _gdn_backward(q, k, v, g, beta, starts, do):
    t_len, width = q.shape
    heads, group, seg, grid = _gdn_layout(q)
    last = grid[1] - 1
    seq, gate, states = _gdn_specs(group, seg, lambda t: last - t)
    chunks = seg // CHUNK

    def body(q_ref, k_ref, v_ref, g_ref, b_ref, s_ref, do_ref, dq_ref, dk_ref, dv_ref, dg_ref, db_ref, d_state_ref):
        @pl.when(pl.program_id(1) == 0)
        def _():
            d_state_ref[...] = jnp.zeros_like(d_state_ref)

        def step(m, carry):
            n = chunks - 1 - m
            rows = pl.ds(pl.multiple_of(n * CHUNK, CHUNK), CHUNK)
            cols = [slice(j * HEAD_DIM, (j + 1) * HEAD_DIM) for j in range(group)]
            per_head = lambda ref: [ref[rows, c] for c in cols]
            _, pull = jax.vjp(_gdn_chunks, [s_ref[j, n] for j in range(group)], per_head(q_ref), per_head(k_ref),
                              per_head(v_ref), [g_ref[j, pl.ds(n, 1), :] for j in range(group)],
                              [b_ref[j, pl.ds(n, 1), :] for j in range(group)])
            d_states, dq, dk, dv, dg, db = pull(([d_state_ref[j] for j in range(group)], per_head(do_ref)))
            for j in range(group):
                dq_ref[rows, cols[j]] = dq[j]
                dk_ref[rows, cols[j]] = dk[j]
                dv_ref[rows, cols[j]] = dv[j]
                dg_ref[j, pl.ds(n, 1), :] = dg[j]
                db_ref[j, pl.ds(n, 1), :] = db[j]
                d_state_ref[j] = d_states[j]
            return carry

        lax.fori_loop(0, chunks, step, 0)

    return pl.pallas_call(
        body,
        name="gdn_bwd",
        grid=grid,
        in_specs=[seq, seq, seq, gate, gate, states, seq],
        out_specs=[seq, seq, seq, gate, gate],
        out_shape=[jax.ShapeDtypeStruct((t_len, width), F32)] * 3
        + [jax.ShapeDtypeStruct((heads, t_len // CHUNK, CHUNK), F32)] * 2,
        scratch_shapes=[pltpu.VMEM((group, HEAD_DIM, HEAD_DIM), F32)],
        compiler_params=_params("parallel", "arbitrary"),
    )(q, k, v, g, beta, starts, do)


@jax.custom_vjp
def _gated_delta_rule(q, k, v, g, beta):
    return _gdn_forward(q, k, v, g, beta)[0]


def _gated_delta_rule_fwd(q, k, v, g, beta):
    o, starts = _gdn_forward(q, k, v, g, beta)
    return o, (q, k, v, g, beta, starts)


def _gated_delta_rule_bwd(res, do):
    return tuple(_gdn_backward(*res, do))


_gated_delta_rule.defvjp(_gated_delta_rule_fwd, _gated_delta_rule_bwd)


def _rms_norm(x, gain):
    return x * lax.rsqrt(jnp.mean(x * x, axis=-1, keepdims=True) + EPS) * gain


def _layer_norm(x, gain):
    xc = x - jnp.mean(x, axis=-1, keepdims=True)
    return xc * lax.rsqrt(jnp.mean(xc * xc, axis=-1, keepdims=True) + EPS) * gain


def _l2_norm(x):
    return x * lax.rsqrt(jnp.sum(x * x, axis=-1, keepdims=True) + EPS)


def _causal_conv(x, w):
    t_len = x.shape[0]
    xp = jnp.pad(x, ((CONV_WIDTH - 1, 0), (0, 0)))
    return sum(xp[i:i + t_len] * w[i] for i in range(CONV_WIDTH))


def _packed_sections(d_model):
    heads = d_model // 256
    mixer = heads * HEAD_DIM
    gmlp = d_model // 2
    widths = (("gdn_qkv", 3 * mixer), ("gdn_gate", mixer), ("gmlp_uv", 2 * gmlp), ("sba_qkv", 3 * mixer),
              ("gates", N_BRANCHES * d_model), ("gdn_ab", 2 * heads))
    sections, off = {}, 0
    for name, width in widths:
        sections[name] = (off, width)
        off += width
    return sections, _round_up(off, MATMUL_K_TILE)


def _pack_w_in(w, d_model):
    sections, total = _packed_sections(d_model)
    heads = d_model // 256
    mixer = heads * HEAD_DIM
    ref_off = {"gdn_qkv": 0, "gdn_ab": 3 * mixer, "gdn_gate": 3 * mixer + 2 * heads}
    ref_off["gmlp_uv"] = ref_off["gdn_gate"] + mixer
    ref_off["sba_qkv"] = ref_off["gmlp_uv"] + d_model
    ref_off["gates"] = ref_off["sba_qkv"] + 3 * mixer
    parts = [w[..., ref_off[name]:ref_off[name] + width] for name, (_, width) in sections.items()]
    used = sum(width for _, width in sections.values())
    parts.append(jnp.zeros(w.shape[:-1] + (total - used,), w.dtype))
    return jnp.concatenate(parts, axis=-1)


def _unpack_w_in_shards(wp, d_model):
    sections, _ = _packed_sections(d_model)
    order = ("gdn_qkv", "gdn_ab", "gdn_gate", "gmlp_uv", "sba_qkv", "gates")
    shard = sum(sections[n][1] for n in order) // N_CHIPS
    shards = []
    for s in range(N_CHIPS):
        pieces, ref_off = [], 0
        for name in order:
            packed_off, width = sections[name]
            lo, hi = max(ref_off, s * shard), min(ref_off + width, (s + 1) * shard)
            if lo < hi:
                pieces.append(wp[:, packed_off + lo - ref_off:packed_off + hi - ref_off])
            ref_off += width
        shards.append(jnp.concatenate(pieces, axis=-1))
    return jnp.stack(shards)


def _mixer(h, p, layer):
    t_len, d_model = h.shape
    heads = d_model // 256
    mixer = heads * HEAD_DIM
    gmlp = d_model // 2
    sections, _ = _packed_sections(d_model)
    z = _linear(h, p["w_in"][layer], "w_in")
    cut = lambda name: z[:, sections[name][0]:sections[name][0] + sections[name][1]]
    per_head = lambda t: t.reshape(t_len, heads, HEAD_DIM)

    qkv = jax.nn.silu(_causal_conv(cut("gdn_qkv"), p["conv_w"][layer]))
    qa = _l2_norm(per_head(qkv[:, :mixer])) * HEAD_DIM ** -0.5
    ka = _l2_norm(per_head(qkv[:, mixer:2 * mixer]))
    va = qkv[:, 2 * mixer:]
    ab = cut("gdn_ab")
    to_chunks = lambda t: jnp.transpose(t).reshape(heads, t_len // CHUNK, CHUNK)
    beta = to_chunks(jax.nn.sigmoid(ab[:, heads:]))
    g = to_chunks(-jnp.exp(p["a_log"][layer]) * jax.nn.softplus(ab[:, :heads] + p["dt_bias"][layer]))
    oa = _gated_delta_rule(qa.reshape(t_len, mixer), ka.reshape(t_len, mixer), va, g, beta)
    oa = _rms_norm(per_head(oa), p["gdn_norm_g"][layer]) * jax.nn.silu(per_head(cut("gdn_gate")))
    branch_a = _linear(oa.reshape(t_len, mixer), p["w_out_a"][layer], "w_out_branch", "col")

    uv = jax.nn.gelu(cut("gmlp_uv"), approximate=False)
    u, vb = uv[:, :gmlp], uv[:, gmlp:]
    group_dim = gmlp // GMLP_GROUPS
    vb = _layer_norm(vb, p["gmlp_ln_g"][layer]).reshape(t_len // GMLP_BLOCK, GMLP_BLOCK, GMLP_GROUPS, group_dim)
    pos = jnp.arange(GMLP_BLOCK) // CHUNK
    ws = jnp.where((pos[None, :] <= pos[:, None])[None], p["w_spatial"][layer], 0.0)
    s = jnp.einsum("gts,nsgc->ntgc", ws.astype(BF16), vb.astype(BF16), preferred_element_type=F32)
    s = s + jnp.transpose(p["b_spatial"][layer])[None, :, :, None]
    branch_b = _linear(u * s.reshape(t_len, gmlp), p["w_out_b"][layer], "w_out_branch", "col")

    sba = cut("sba_qkv")
    qc = _rms_norm(per_head(sba[:, :mixer]), p["sba_q_g"][layer]).reshape(t_len, mixer)
    kc = _rms_norm(per_head(sba[:, mixer:2 * mixer]), p["sba_k_g"][layer]).reshape(t_len, mixer)
    oc = _stick_breaking(qc, kc, sba[:, 2 * mixer:])
    branch_c = _linear(oc, p["w_out_c"][layer], "w_out_branch", "col")

    gates = jax.nn.sigmoid(cut("gates"))
    y = (gates[:, :d_model] * branch_a + gates[:, d_model:2 * d_model] * branch_b
         + gates[:, 2 * d_model:] * branch_c)
    return _linear(y, p["w_out"][layer], "w_out", "row")


def _local_loss(x, p, target):
    depth = len(p["w_in"])
    for layer in range(depth):
        x = x + _mixer(_rms_norm(x, p["norm_mix_g"][layer]), p, layer)
        f = _linear(_rms_norm(x, p["norm_mlp_g"][layer]), p["w_ff1"][layer], "w_ff1", "col")
        x = x + _linear(jnp.square(jax.nn.relu(f)), p["w_ff2"][layer], "w_ff2", "row")
    err = jnp.square(x - target)
    return 0.5 * jnp.sum(jnp.mean(err, axis=-1))


ANY = pl.BlockSpec(memory_space=pl.ANY)


def _place():
    x, y, c = lax.axis_index("x"), lax.axis_index("y"), lax.axis_index("c")
    other_chips = [(1 - x, y), (x, 1 - y), (1 - x, 1 - y)]
    return x, y, c, other_chips


def _my_chip():
    return 2 * lax.axis_index("x") + lax.axis_index("y")


def _cast_to_slot(w, layer):
    _, rows, cols = w.shape
    tr = _row_tile(rows, cols, 16)

    def body(w_ref, o_ref):
        o_ref[...] = w_ref[...].astype(BF16)

    return pl.pallas_call(
        body,
        name="cast_to_slot",
        grid=(rows // tr,),
        in_specs=[pl.BlockSpec((None, tr, cols), lambda r: (layer, r, 0))],
        out_specs=pl.BlockSpec((None, tr, cols), lambda r: (_my_chip(), r, 0)),
        out_shape=jax.ShapeDtypeStruct((N_CHIPS, rows, cols), BF16),
        compiler_params=_params("parallel"),
    )(w)


def _remote(src, dst, sems, a, k, to):
    send_sems, recv_sems = sems
    return pltpu.make_async_remote_copy(src_ref=src, dst_ref=dst, send_sem=send_sems.at[a, k],
                                        recv_sem=recv_sems.at[a, k], device_id=to, device_id_type=MESH)


def _sem_pairs(n_arrays, n_copies):
    return [pltpu.SemaphoreType.DMA((n_arrays, n_copies)), pltpu.SemaphoreType.DMA((n_arrays, n_copies))]


def _gather_chips(bufs):
    n = len(bufs)

    def body(*refs):
        outs, sems = refs[n:2 * n], refs[2 * n:]
        x, y, c, _ = _place()
        x_nbr, y_nbr, diag, sibling = (1 - x, y), (x, 1 - y), (1 - x, 1 - y), (x, y, 1 - c)

        def piece(a, chip, core, quarter=None):
            half = bufs[a].shape[1] // 2
            rows = pl.ds(core * half, half) if quarter is None else pl.ds(core * half + quarter * (half // 2), half // 2)
            return outs[a].at[2 * chip[0] + chip[1], rows, :]

        def copy(a, k, ref, to):
            return _remote(ref, ref, sems, a, k, to)

        def sends(a):
            return [(0, piece(a, (x, y), c), (*x_nbr, c)),
                    (1, piece(a, (x, y), c), (*y_nbr, c)),
                    (2, piece(a, x_nbr, c, 0), (*y_nbr, c)),
                    (3, piece(a, y_nbr, c, 1), (*x_nbr, c)),
                    (4, piece(a, x_nbr, c), sibling),
                    (5, piece(a, y_nbr, c), sibling),
                    (6, piece(a, diag, c, 0), sibling),
                    (7, piece(a, diag, c, 1), sibling)]

        def lands(a):
            return [piece(a, x_nbr, c), piece(a, y_nbr, c), piece(a, diag, c, 0), piece(a, diag, c, 1),
                    piece(a, x_nbr, 1 - c), piece(a, y_nbr, 1 - c), piece(a, diag, 1 - c, 0), piece(a, diag, 1 - c, 1)]

        started = []

        def start(a, k):
            _, ref, to = sends(a)[k]
            started.append(copy(a, k, ref, to))
            started[-1].start()

        def arrived(a, k):
            copy(a, k, lands(a)[k], sibling).wait_recv()

        for a in range(n):
            start(a, 0)
            start(a, 1)
        for k, onward in ((0, (2, 4)), (1, (3, 5)), (2, (6,)), (3, (7,))):
            for a in range(n):
                arrived(a, k)
                for nxt in onward:
                    start(a, nxt)
        for a in range(n):
            for k in (4, 5, 6, 7):
                arrived(a, k)
        for cp in started:
            cp.wait_send()

    return pl.pallas_call(
        body,
        name="gather_chips",
        in_specs=[ANY] * n,
        out_specs=[ANY] * n,
        out_shape=[jax.ShapeDtypeStruct(b.shape, b.dtype) for b in bufs],
        input_output_aliases={a: a for a in range(n)},
        scratch_shapes=_sem_pairs(n, 8),
    )(*bufs)


def _staged_exchange(grads, parts, halves):
    n_g, n_p, n_h = len(grads), len(parts), len(halves)
    n_in = n_g + n_p + n_h

    def body(*refs):
        ins, outs, sems = refs[:n_in], refs[n_in:2 * n_in], refs[2 * n_in:]
        x, y, c, other_chips = _place()
        me, sibling = 2 * x + y, (x, y, 1 - c)
        copies = []
        for a in range(n_p):
            for j, chip in enumerate(other_chips):
                copies.append(_remote(ins[n_g + a].at[2 * chip[0] + chip[1]], outs[n_g + a].at[me], sems[2:4], a, j, (*chip, c)))
        for a in range(n_g):
            rows = grads[a].shape[1] // 2
            copies.append(_remote(ins[a].at[:, pl.ds((1 - c) * rows, rows), :], outs[a], sems[0:2], a, 0, sibling))
        for a in range(n_h):
            rows = halves[a].shape[0] // 2
            mine = outs[n_g + n_p + a].at[pl.ds(c * rows, rows), :]
            copies.append(_remote(mine, mine, sems[4:6], a, 0, sibling))
        for cp in copies:
            cp.start()
        for cp in copies:
            cp.wait()

    half_shape = lambda g: (g.shape[0], g.shape[1] // 2, g.shape[2])
    return pl.pallas_call(
        body,
        name="staged_exchange",
        in_specs=[ANY] * n_in,
        out_specs=[ANY] * n_in,
        out_shape=[jax.ShapeDtypeStruct(half_shape(g), g.dtype) for g in grads]
        + [jax.ShapeDtypeStruct(t.shape, t.dtype) for t in parts + halves],
        input_output_aliases={n_g + n_p + a: n_g + n_p + a for a in range(n_h)},
        scratch_shapes=_sem_pairs(max(n_g, 1), 1) + _sem_pairs(max(n_p, 1), 3) + _sem_pairs(max(n_h, 1), 1),
    )(*grads, *parts, *halves)


def _all_gather_rows(block):
    m_per, n = block.shape

    def body(x_ref, out_ref, send_sems, recv_sems, local_sem):
        x, y, c, other_chips = _place()
        me, sibling = (x, y, c), (x, y, 1 - c)

        def rows(px, py, pc):
            return out_ref.at[pl.ds((4 * px + 2 * py + pc) * m_per, m_per), :]

        def copy(k, blk, to, src=None):
            return pltpu.make_async_remote_copy(src_ref=rows(*blk) if src is None else src, dst_ref=rows(*blk),
                                                send_sem=send_sems.at[k], recv_sem=recv_sems.at[k],
                                                device_id=to, device_id_type=MESH)

        mine = pltpu.make_async_copy(x_ref, rows(*me), local_sem)
        mine.start()
        first = [copy(0, me, sibling, src=x_ref)]
        first += [copy(1 + j, me, (*chip, c), src=x_ref) for j, chip in enumerate(other_chips)]
        for cp in first:
            cp.start()
        passed = [copy(4 + j, (*chip, c), sibling) for j, chip in enumerate(other_chips)]
        for j, chip in enumerate(other_chips):
            copy(1 + j, (*chip, c), me).wait_recv()
            passed[j].start()
        copy(0, sibling, me).wait_recv()
        for j, chip in enumerate(other_chips):
            copy(4 + j, (*chip, 1 - c), me).wait_recv()
        for cp in first + passed:
            cp.wait_send()
        mine.wait()

    return pl.pallas_call(
        body,
        name="all_gather_rows",
        in_specs=[ANY],
        out_specs=ANY,
        out_shape=jax.ShapeDtypeStruct((N_DEV * m_per, n), block.dtype),
        scratch_shapes=[pltpu.SemaphoreType.DMA((7,)), pltpu.SemaphoreType.DMA((7,)), pltpu.SemaphoreType.DMA],
    )(block)


def _row_tile(rows, cols, multiple):
    budget = max(multiple, ELEMENTWISE_BLOCK_ELEMS // _round_up(cols, LANES))
    t = multiple
    while t * 2 <= budget and rows % (t * 2) == 0:
        t *= 2
    return t if rows % t == 0 else rows


def _pair_sum(grad, theirs):
    chips, rows, cols = grad.shape
    half = rows // 2
    tr = _row_tile(half, cols, 16)
    blocks = half // tr

    def body(a_ref, b_ref, o_ref):
        o_ref[...] = (a_ref[...].astype(F32) + b_ref[...].astype(F32)).astype(o_ref.dtype)

    return pl.pallas_call(
        body,
        name="pair_sum",
        grid=(chips, blocks),
        in_specs=[pl.BlockSpec((None, tr, cols), lambda s, r: (s, lax.axis_index("c") * blocks + r, 0)),
                  pl.BlockSpec((None, tr, cols), lambda s, r: (s, r, 0))],
        out_specs=pl.BlockSpec((None, tr, cols), lambda s, r: (s, r, 0)),
        out_shape=jax.ShapeDtypeStruct((chips, half, cols), grad.dtype),
        compiler_params=_params("parallel", "parallel"),
    )(grad, theirs)


def _chip_sum(mine, others):
    chips, half, cols = mine.shape
    tr = _row_tile(half, cols, 16)
    blocks = half // tr

    def body(own_ref, *refs):
        o_ref = refs[-1]
        total = own_ref[...].astype(F32)
        for ref in refs[:-1]:
            total = total + ref[...].astype(F32)
        o_ref[...] = total

    slot = lambda q: pl.BlockSpec((None, tr, cols), lambda r: ((_my_chip() + q) % chips, r, 0))
    return pl.pallas_call(
        body,
        name="chip_sum",
        grid=(blocks,),
        in_specs=[slot(q) for q in range(chips)],
        out_specs=pl.BlockSpec((tr, cols), lambda r: (lax.axis_index("c") * blocks + r, 0)),
        out_shape=jax.ShapeDtypeStruct((2 * half, cols), F32),
        compiler_params=_params("parallel"),
    )(mine, *([others] * (chips - 1)))


def _device_sum(blocks, m_per):
    tr = _row_tile(m_per, LANES, 8)
    per = m_per // tr

    def body(*refs):
        o_ref = refs[-1]
        total = refs[0][...]
        for ref in refs[1:-1]:
            total = total + ref[...]
        o_ref[...] = total

    return pl.pallas_call(
        body,
        name="device_sum",
        grid=(per,),
        in_specs=[pl.BlockSpec((tr, LANES), functools.partial(lambda d, r: (d * per + r, 0), d)) for d in range(N_DEV)],
        out_specs=pl.BlockSpec((tr, LANES), lambda r: (r, 0)),
        out_shape=jax.ShapeDtypeStruct((m_per, LANES), F32),
        compiler_params=_params("parallel"),
    )(*([blocks] * N_DEV))


def _adamw(grads, w, m, v):
    depth, rows, cols = w.shape
    tr = _row_tile(rows, cols, 8) if rows % 8 == 0 else rows
    spec = pl.BlockSpec((None, tr, cols), lambda l, r: (l, r, 0))
    grad_spec = lambda q: pl.BlockSpec((tr, cols), lambda l, r: (jnp.where(l == q, r, 0), 0))

    def body(*refs):
        g_refs, (w_ref, m_ref, v_ref, go_ref, d_ref, mo_ref, vo_ref) = refs[:depth], refs[depth:]
        layer = pl.program_id(0)
        g = g_refs[0][...]
        for q in range(1, depth):
            g = jnp.where(layer == q, g_refs[q][...], g)
        m_new = ADAM_B1 * m_ref[...] + (1.0 - ADAM_B1) * g
        v_new = ADAM_B2 * v_ref[...] + (1.0 - ADAM_B2) * jnp.square(g)
        m_hat = m_new / (1.0 - ADAM_B1 ** ADAM_STEP)
        v_hat = v_new / (1.0 - ADAM_B2 ** ADAM_STEP)
        go_ref[...] = g
        d_ref[...] = -ADAM_LR * (m_hat / (jnp.sqrt(v_hat) + ADAM_EPS) + ADAM_WD * w_ref[...])
        mo_ref[...] = m_new
        vo_ref[...] = v_new

    return pl.pallas_call(
        body,
        name="adamw",
        grid=(depth, rows // tr),
        in_specs=[grad_spec(q) for q in range(depth)] + [spec] * 3,
        out_specs=[spec] * 4,
        out_shape=[jax.ShapeDtypeStruct(w.shape, F32)] * 4,
        compiler_params=_params("parallel", "parallel"),
    )(*grads, w, m, v)


def _divisor_tile(dim, limit):
    t = max(1, min(dim, limit))
    while dim % t:
        t -= 1
    return t


def _adamw_columns_major(grads, w, m, v):
    depth, rows, cols = w.shape
    view = lambda t: jnp.transpose(t, (2, 0, 1))
    back = lambda t: jnp.transpose(t, (1, 2, 0))
    grad = jnp.stack([jnp.transpose(g) for g in grads], axis=1)
    tc = _divisor_tile(cols, max(1, ELEMENTWISE_BLOCK_ELEMS // (2 * depth * rows)))
    spec = pl.BlockSpec((tc, depth, rows), lambda c: (c, 0, 0))

    def body(g_ref, w_ref, m_ref, v_ref, d_ref, mo_ref, vo_ref):
        g = g_ref[...]
        m_new = ADAM_B1 * m_ref[...] + (1.0 - ADAM_B1) * g
        v_new = ADAM_B2 * v_ref[...] + (1.0 - ADAM_B2) * jnp.square(g)
        m_hat = m_new / (1.0 - ADAM_B1 ** ADAM_STEP)
        v_hat = v_new / (1.0 - ADAM_B2 ** ADAM_STEP)
        d_ref[...] = -ADAM_LR * (m_hat / (jnp.sqrt(v_hat) + ADAM_EPS) + ADAM_WD * w_ref[...])
        mo_ref[...] = m_new
        vo_ref[...] = v_new

    delta, m_new, v_new = pl.pallas_call(
        body,
        name="adamw_columns_major",
        grid=(cols // tc,),
        in_specs=[spec] * 4,
        out_specs=[spec] * 3,
        out_shape=[jax.ShapeDtypeStruct((cols, depth, rows), F32)] * 3,
        compiler_params=_params("parallel"),
    )(grad, view(w), view(m), view(v))
    return back(grad), back(delta), back(m_new), back(v_new)


def _join_cols(gathered):
    return jnp.concatenate([gathered[s] for s in range(N_CHIPS)], axis=-1)


def _reduce_scatter_layers(local):
    depth = len(local)
    pairs, halves, summed = {}, {}, [None] * depth
    for stage in range(depth + 2):
        grads = local[stage] if stage < depth else []
        parts = pairs[stage - 1] if 1 <= stage <= depth else []
        done = halves[stage - 2] if stage >= 2 else []
        moved = _staged_exchange(grads, parts, done)
        theirs, others, whole = moved[:len(grads)], moved[len(grads):len(grads) + len(parts)], moved[len(grads) + len(parts):]
        if grads:
            pairs[stage] = [_pair_sum(g, t) for g, t in zip(grads, theirs)]
        if parts:
            halves[stage - 1] = [_chip_sum(p, o) for p, o in zip(parts, others)]
        if done:
            summed[stage - 2] = list(whole)
    return summed


def _pack_rows(arrays):
    parts, counts = [], []
    for a in arrays:
        flat = a.reshape(-1).astype(F32)
        n_rows = _round_up(flat.shape[0], 8 * LANES) // LANES
        parts.append(jnp.pad(flat, (0, n_rows * LANES - flat.shape[0])).reshape(n_rows, LANES))
        counts.append(n_rows)
    return jnp.concatenate(parts, axis=0), counts


def _pad_rows(a, rows):
    return jnp.pad(a, ((0, rows - a.shape[0]), (0, 0)))


def _unpack_rows(packed, counts, shapes):
    out, row = [], 0
    for n_rows, shape in zip(counts, shapes):
        size = 1
        for d in shape:
            size *= d
        out.append(packed[row:row + n_rows].reshape(-1)[:size].reshape(shape))
        row += n_rows
    return out


SHARDED = (("w_in", "col"), ("w_out_a", "col"), ("w_out_b", "col"), ("w_out_c", "col"), ("w_out", "row"),
           ("w_ff1", "col"), ("w_ff2", "row"))


def kernel(x, w_in, conv_w, a_log, dt_bias, gdn_norm_g, gmlp_ln_g, w_spatial, b_spatial, sba_q_g, sba_k_g, w_out_a, w_out_b, w_out_c, w_out, norm_mix_g, norm_mlp_g, w_ff1, w_ff2, loss_target, m_w_in, m_conv_w, m_a_log, m_dt_bias, m_gdn_norm_g, m_gmlp_ln_g, m_w_spatial, m_b_spatial, m_sba_q_g, m_sba_k_g, m_w_out_a, m_w_out_b, m_w_out_c, m_w_out, m_norm_mix_g, m_norm_mlp_g, m_w_ff1, m_w_ff2, v_w_in, v_conv_w, v_a_log, v_dt_bias, v_gdn_norm_g, v_gmlp_ln_g, v_w_spatial, v_b_spatial, v_sba_q_g, v_sba_k_g, v_w_out_a, v_w_out_b, v_w_out_c, v_w_out, v_norm_mix_g, v_norm_mlp_g, v_w_ff1, v_w_ff2):
    given = dict(locals())
    weights = {n: given[n] for n in WEIGHT_NAMES}
    depth, d_model = w_in.shape[0], w_in.shape[1]
    layers = range(depth)
    chip = _my_chip()

    gathered = [_gather_chips([_cast_to_slot(weights[name], l) for name, _ in SHARDED]) for l in layers]
    conv_rows = depth * CONV_WIDTH
    conv_cols = conv_w.shape[-1]
    conv_all = _all_gather_rows(_pad_rows(conv_w.reshape(conv_rows, conv_cols), _round_up(conv_rows, 8)))
    conv_all = conv_all.reshape(N_CHIPS, 2, _round_up(conv_rows, 8), conv_cols)[:, 0, :conv_rows]
    conv_full = jnp.concatenate([conv_all[s] for s in range(N_CHIPS)], axis=-1).reshape(depth, CONV_WIDTH, N_CHIPS * conv_cols)

    params = {n: weights[n] for n in REPLICATED_NAMES}
    params["conv_w"] = conv_full
    for i, (name, _) in enumerate(SHARDED):
        params[name] = [gathered[l][i] for l in layers]
    params["w_in"] = [_pack_w_in(_join_cols(gathered[l][0]), d_model) for l in layers]

    loss_local, (grad_x, grads) = jax.value_and_grad(_local_loss, argnums=(0, 1))(x[0], params, loss_target[0])

    local = [[grads[name][l] for name, _ in SHARDED] for l in layers]
    for l in layers:
        local[l][0] = _unpack_w_in_shards(local[l][0], d_model)
    summed = _reduce_scatter_layers(local)
    rep_packed, rep_counts = _pack_rows([grads[n] for n in REPLICATED_NAMES])
    rep_rows = _round_up(rep_packed.shape[0], SMALL_ROW_ALIGN)
    rest_packed, rest_counts = _pack_rows([grads["conv_w"], loss_local])
    m_per = rep_rows + _round_up(rest_packed.shape[0], SMALL_ROW_ALIGN)
    packed = jnp.concatenate([_pad_rows(rep_packed, rep_rows), _pad_rows(rest_packed, m_per - rep_rows)], axis=0)
    total = _device_sum(_all_gather_rows(packed), m_per)
    conv_sum, loss = _unpack_rows(total[rep_rows:], rest_counts, [grads["conv_w"].shape, ()])
    conv_grad = lax.dynamic_slice_in_dim(conv_sum, chip * conv_cols, conv_cols, axis=2)

    out = {}
    for i, (name, _) in enumerate(SHARDED):
        update = _adamw_columns_major if weights[name].shape[-1] % LANES else _adamw
        out[name] = update([summed[l][i] for l in layers], weights[name], given["m_" + name], given["v_" + name])
    out["conv_w"] = _adamw([conv_grad[l] for l in layers], conv_w, m_conv_w, v_conv_w)
    rep_shapes = [weights[n].shape for n in REPLICATED_NAMES]
    pack3 = lambda prefix: _pad_rows(_pack_rows([given[prefix + n] for n in REPLICATED_NAMES])[0], rep_rows)[None]
    rep = _adamw([total[:rep_rows]], pack3(""), pack3("m_"), pack3("v_"))
    rep = [_unpack_rows(t[0], rep_counts, rep_shapes) for t in rep]
    for i, name in enumerate(REPLICATED_NAMES):
        out[name] = tuple(t[i] for t in rep)

    results = [loss, grad_x[None]]
    for kind in range(4):
        results += [out[n][kind] for n in WEIGHT_NAMES]
    return tuple(results)
```
